```python
import math
import jax, jax.numpy as jnp
from jax import lax
import numpy as np

D_MODEL = 1024
BATCH = 16
SEQ = 2048
DEPTH = 2

HEAD_DIM = 64
N_HEADS = D_MODEL // HEAD_DIM
A_HEADS = N_HEADS // 2
A_BRANCHES = ((128, 1), (512, 4), (2048, 16))
B_HEADS = N_HEADS - A_HEADS
B_Q_RANK = D_MODEL // 4
IDX_HEADS = 8
IDX_DIM = 32
TOPK_MAX = 256
C_HEADS = N_HEADS
C_KV_HEADS = C_HEADS // 8
C_WINDOW = 128
D_FF = 2816
NUM_BUCKETS = 32
MAX_DISTANCE = 2048
BLOCK = 128
EPS = 1e-6
N_EVEN = (DEPTH + 1) // 2
N_ODD = DEPTH // 2
A_W = A_HEADS * HEAD_DIM
B_W = B_HEADS * HEAD_DIM
EVEN_SIZES = (A_W, A_W, A_W, B_Q_RANK, B_W, B_W, IDX_DIM, IDX_HEADS)
EVEN_IN = sum(EVEN_SIZES)
ODD_SIZES = (C_HEADS * HEAD_DIM, C_KV_HEADS * HEAD_DIM, C_KV_HEADS * HEAD_DIM)
ODD_IN = sum(ODD_SIZES)

kernel_name = "hybrid_dilated_dsa_swa_macaron"

F32 = jnp.float32


def _offsets(sizes):
    out, acc = [], 0
    for s in sizes[:-1]:
        acc += s
        out.append(acc)
    return out


def rmsnorm(x, g):
    xf = x.astype(F32)
    y = xf * lax.rsqrt(jnp.mean(xf * xf, axis=-1, keepdims=True) + EPS)
    return (y * g.astype(F32)).astype(x.dtype)


def swiglu(x, w1, w3, w2):
    return (jax.nn.silu(x @ w1) * (x @ w3)) @ w2


def t5_bucket(dist):
    dist = jnp.maximum(dist, 0)
    max_exact = NUM_BUCKETS // 2
    d_f = jnp.maximum(dist, 1).astype(F32)
    large = max_exact + (jnp.log(d_f / max_exact) / math.log(MAX_DISTANCE / max_exact)
                         * (NUM_BUCKETS - max_exact)).astype(jnp.int32)
    large = jnp.minimum(large, NUM_BUCKETS - 1)
    return jnp.where(dist < max_exact, dist, large)


def rel_bias(table, dist):
    return table.astype(F32)[t5_bucket(dist)]


def local_dist():
    return jnp.arange(BLOCK)[:, None] + BLOCK - jnp.arange(2 * BLOCK)[None, :]


def banded_attention(q, k, v, bias, max_dist):
    n, L, H, dh = q.shape
    hk = k.shape[2]
    g = H // hk
    nb = -(-L // BLOCK)
    lp = nb * BLOCK
    pad = lp - L
    qb = jnp.pad(q, ((0, 0), (0, pad), (0, 0), (0, 0))).reshape(n, nb, BLOCK, hk, g, dh)

    def windows(t):
        tp = jnp.pad(t, ((0, 0), (BLOCK, pad), (0, 0), (0, 0))).reshape(n, nb + 1, BLOCK, hk, dh)
        return jnp.concatenate([tp[:, :-1], tp[:, 1:]], axis=2)

    kw, vw = windows(k), windows(v)
    s = jnp.einsum('nbqhgd,nbshd->nbhgqs', qb, kw, preferred_element_type=F32) * (dh ** -0.5)
    s = s + bias.reshape(hk, g, BLOCK, 2 * BLOCK)[None, None]
    loc = local_dist()
    key_abs = jnp.arange(nb)[:, None, None] * BLOCK + jnp.arange(2 * BLOCK)[None, None, :] - BLOCK
    mask = (loc >= 0)[None] & (loc <= max_dist)[None] & (key_abs >= 0)
    s = jnp.where(mask[None, :, None, None], s, -jnp.inf)
    m = jnp.max(s, axis=-1, keepdims=True)
    p = jnp.exp(s - m)
    den = jnp.sum(p, axis=-1)
    o = jnp.einsum('nbhgqs,nbshd->nbqhgd', p, vw.astype(F32))
    o = o / jnp.transpose(den, (0, 1, 4, 2, 3))[..., None]
    lse = jnp.transpose(m[..., 0] + jnp.log(den), (0, 1, 4, 2, 3))
    o = o.reshape(n, lp, H, dh)[:, :L]
    lse = lse.reshape(n, lp, H)[:, :L]
    return o, lse


def dilated_mixture(q, k, v, table):
    bn, S, H, dh = q.shape
    outs, lses = [], []
    for window, dil in A_BRANCHES:
        ls = S // dil

        def fold(t):
            return t.reshape(bn, ls, dil, H, dh).transpose(0, 2, 1, 3, 4).reshape(bn * dil, ls, H, dh)

        bias = jnp.moveaxis(rel_bias(table, dil * local_dist()), -1, 0)
        o, lse = banded_attention(fold(q), fold(k), fold(v), bias, window // dil)
        outs.append(o.reshape(bn, dil, ls, H, dh).transpose(0, 2, 1, 3, 4).reshape(bn, S, H, dh))
        lses.append(lse.reshape(bn, dil, ls, H).transpose(0, 2, 1, 3).reshape(bn, S, H))
    wts = jax.nn.softmax(jnp.stack(lses), axis=0)
    return jnp.einsum('rbsh,rbshd->bshd', wts, jnp.stack(outs))


def dsa_attention(q, k, v, q_idx, k_idx, w_idx, table, topk):
    bn, S, H, dh = q.shape
    nb = S // BLOCK
    idx_scale = (IDX_DIM ** -0.5) * (IDX_HEADS ** -0.5)
    spos = jnp.arange(S)

    def to_blocks(t):
        return jnp.swapaxes(t.reshape((bn, nb, BLOCK) + t.shape[2:]), 0, 1)

    def one_block(args):
        qb, qib, wb, t0 = args
        tpos = t0 + jnp.arange(BLOCK)
        sc = jax.nn.relu(jnp.einsum('bqhd,bsd->bqhs', qib, k_idx, preferred_element_type=F32))
        score = jnp.einsum('bqhs,bqh->bqs', sc, wb.astype(F32)) * idx_scale
        score = jnp.where((spos[None, :] <= tpos[:, None])[None], score, -jnp.inf)
        _, idx = lax.top_k(score, topk)
        ks = jax.vmap(lambda kk, ii: kk[ii])(k, idx)
        vs = jax.vmap(lambda vv, ii: vv[ii])(v, idx)
        dist = tpos[None, :, None] - idx
        logits = jnp.einsum('bqhd,bqkhd->bqhk', qb, ks, preferred_element_type=F32) * (dh ** -0.5)
        logits = logits + jnp.moveaxis(rel_bias(table, dist), -1, 2)
        logits = jnp.where((dist >= 0)[:, :, None, :], logits, -jnp.inf)
        p = jax.nn.softmax(logits, axis=-1)
        return jnp.einsum('bqhk,bqkhd->bqhd', p, vs.astype(F32))

    t0s = jnp.arange(nb, dtype=jnp.int32) * BLOCK
    out = lax.map(one_block, (to_blocks(q), to_blocks(q_idx), to_blocks(w_idx), t0s))
    return jnp.swapaxes(out, 0, 1).reshape(bn, S, H, dh)


def hybrid_ab_mixer(u, table, w_in, cq_g, wq_b, wq_idx, w_out):
    bn, S, _ = u.shape
    proj = u @ w_in
    qa, ka, va, cq, kb, vb, k_idx, w_idx = jnp.split(proj, _offsets(EVEN_SIZES), axis=-1)
    hs = (bn, S, A_HEADS, HEAD_DIM)
    o_a = dilated_mixture(qa.reshape(hs), ka.reshape(hs), va.reshape(hs), table[:, :A_HEADS])
    cq = rmsnorm(cq, cq_g)
    hb = (bn, S, B_HEADS, HEAD_DIM)
    qb = (cq @ wq_b).reshape(hb)
    q_idx = (cq @ wq_idx).reshape(bn, S, IDX_HEADS, IDX_DIM)
    topk = min(TOPK_MAX, S // 4)
    o_b = dsa_attention(qb, kb.reshape(hb), vb.reshape(hb), q_idx, k_idx, w_idx,
                        table[:, A_HEADS:], topk)
    o = jnp.concatenate([o_a.reshape(bn, S, A_W), o_b.reshape(bn, S, B_W)], axis=-1)
    return o.astype(u.dtype) @ w_out


def swa_sink_mixer(u, table, w_in, sinks, w_out):
    bn, S, _ = u.shape
    q, k, v = jnp.split(u @ w_in, _offsets(ODD_SIZES), axis=-1)
    q = q.reshape(bn, S, C_HEADS, HEAD_DIM)
    k = k.reshape(bn, S, C_KV_HEADS, HEAD_DIM)
    v = v.reshape(bn, S, C_KV_HEADS, HEAD_DIM)
    bias = jnp.moveaxis(rel_bias(table, local_dist()), -1, 0)
    o, lse = banded_attention(q, k, v, bias, C_WINDOW - 1)
    o = o * jax.nn.sigmoid(lse - sinks.astype(F32))[..., None]
    return o.reshape(bn, S, C_HEADS * HEAD_DIM).astype(u.dtype) @ w_out


def setup_inputs(seed: int = 0) -> dict:
    key = jax.random.key(seed)
    ks = jax.random.split(key, 20)
    nrm = lambda k, shape, scale: jax.random.normal(k, shape, F32) * scale
    return {
        "x": nrm(ks[0], (BATCH, SEQ, D_MODEL), 1.0),
        "norm_g": 1.0 + nrm(ks[1], (DEPTH, 3, D_MODEL), 0.02),
        "final_g": 1.0 + nrm(ks[2], (D_MODEL,), 0.02),
        "rel_bias_table": nrm(ks[3], (NUM_BUCKETS, N_HEADS), 0.2),
        "ffn_w1": nrm(ks[4], (DEPTH, 2, D_MODEL, D_FF), D_MODEL ** -0.5),
        "ffn_w3": nrm(ks[5], (DEPTH, 2, D_MODEL, D_FF), D_MODEL ** -0.5),
        "ffn_w2": nrm(ks[6], (DEPTH, 2, D_FF, D_MODEL), D_FF ** -0.5),
        "hyb_w_in": nrm(ks[7], (N_EVEN, D_MODEL, EVEN_IN), D_MODEL ** -0.5),
        "hyb_cq_g": 1.0 + nrm(ks[8], (N_EVEN, B_Q_RANK), 0.02),
        "hyb_wq_b": nrm(ks[9], (N_EVEN, B_Q_RANK, B_W), B_Q_RANK ** -0.5),
        "hyb_wq_idx": nrm(ks[10], (N_EVEN, B_Q_RANK, IDX_HEADS * IDX_DIM), B_Q_RANK ** -0.5),
        "hyb_w_out": nrm(ks[11], (N_EVEN, A_W + B_W, D_MODEL), (A_W + B_W) ** -0.5),
        "swa_w_in": nrm(ks[12], (N_ODD, D_MODEL, ODD_IN), D_MODEL ** -0.5),
        "swa_sinks": nrm(ks[13], (N_ODD, C_HEADS), 1.0),
        "swa_w_out": nrm(ks[14], (N_ODD, C_HEADS * HEAD_DIM, D_MODEL), (C_HEADS * HEAD_DIM) ** -0.5),
    }


def reference(x, norm_g, final_g, rel_bias_table, ffn_w1, ffn_w3, ffn_w2, hyb_w_in, hyb_cq_g,
              hyb_wq_b, hyb_wq_idx, hyb_w_out, swa_w_in, swa_sinks, swa_w_out):
    h = x
    for layer in range(DEPTH):
        g = norm_g[layer]
        h = h + 0.5 * swiglu(rmsnorm(h, g[0]), ffn_w1[layer, 0], ffn_w3[layer, 0], ffn_w2[layer, 0])
        u = rmsnorm(h, g[1])
        i = layer // 2
        if layer % 2 == 0:
            mix = hybrid_ab_mixer(u, rel_bias_table, hyb_w_in[i], hyb_cq_g[i], hyb_wq_b[i],
                                  hyb_wq_idx[i], hyb_w_out[i])
        else:
            mix = swa_sink_mixer(u, rel_bias_table, swa_w_in[i], swa_sinks[i], swa_w_out[i])
        h = h + mix
        h = h + 0.5 * swiglu(rmsnorm(h, g[2]), ffn_w1[layer, 1], ffn_w3[layer, 1], ffn_w2[layer, 1])
    return rmsnorm(h, final_g)
```

```python
import functools
import math

import numpy as np
import jax
import jax.numpy as jnp
from jax import lax
from jax.experimental import pallas as pl
from jax.experimental.pallas import tpu as pltpu

F32 = jnp.float32
BF16 = jnp.bfloat16
I32 = jnp.int32

EPS = 1e-6
HEAD_DIM = 64
BLOCK = 128
LANES = 128
NUM_BUCKETS = 32
MAX_DISTANCE = 2048
A_BRANCHES = ((128, 1), (512, 4), (2048, 16))
C_WINDOW = 128
TOPK_MAX = 256
IDX_HEADS = 8
IDX_DIM = 32
NEG = -1e30
INT_MIN = -(2 ** 31)
KEY_NEG_INF = (0xFF800000 - (1 << 32)) ^ 0x7FFFFFFF
VMEM_LIMIT = 56 * 1024 * 1024

TM = 512


def _cparams(n_axes):
    return pltpu.CompilerParams(
        dimension_semantics=("arbitrary",) * n_axes, vmem_limit_bytes=VMEM_LIMIT)


def _rms(x, g):
    return x * lax.rsqrt(jnp.mean(x * x, axis=-1, keepdims=True) + EPS) * g


def _dot(a, b):
    return jnp.dot(a, b, preferred_element_type=F32)


def _dot_nt(a, b):
    return lax.dot_general(a, b, (((1,), (1,)), ((), ())), preferred_element_type=F32)


def _lane_id_bf16(shape):
    return lax.broadcasted_iota(I32, shape, 1).astype(F32).astype(BF16)


def _rows(start, size):
    if isinstance(start, int):
        return pl.ds(start, size)
    return pl.ds(pl.multiple_of(start, size), size)


def _const_spec(shape):
    nd = len(shape)
    return pl.BlockSpec(shape, lambda *_: (0,) * nd, pipeline_mode=pl.Buffered(1))


def _t5_bucket_np(dist):
    dist = np.maximum(dist, 0)
    max_exact = NUM_BUCKETS // 2
    d_f = np.maximum(dist, 1).astype(np.float32)
    large = max_exact + (np.log(d_f / max_exact) / math.log(MAX_DISTANCE / max_exact)
                         * (NUM_BUCKETS - max_exact)).astype(np.int32)
    large = np.minimum(large, NUM_BUCKETS - 1)
    return np.where(dist < max_exact, dist, large).astype(np.int32)


def _band_bucket_idx(dil, max_dist):
    loc = np.arange(BLOCK)[:, None] + BLOCK - np.arange(2 * BLOCK)[None, :]
    valid = (loc >= 0) & (loc <= max_dist)
    return np.where(valid, _t5_bucket_np(dil * loc), -1).astype(np.int32)


def _dense_bucket_idx(seq):
    nb = seq // BLOCK
    d = np.arange(nb)[:, None, None] * BLOCK + np.arange(BLOCK)[None, :, None] - np.arange(BLOCK)[None, None, :]
    return np.where(d >= 0, _t5_bucket_np(d), -1).astype(np.int32).reshape(seq, BLOCK)


def _bias_kernel(tab_ref, idx_ref, o_ref, *, head0):
    h = pl.program_id(0) + head0
    idx = idx_ref[...]
    out = jnp.full(idx.shape, NEG, F32)
    for b in range(NUM_BUCKETS):
        out = jnp.where(idx == b, tab_ref[b, h], out)
    o_ref[0] = out


def _build_bias(table, idx_np, head0, n_heads):
    r, c = idx_np.shape
    return pl.pallas_call(
        functools.partial(_bias_kernel, head0=head0),
        grid=(n_heads,),
        in_specs=[pl.BlockSpec(memory_space=pltpu.SMEM),
                  pl.BlockSpec((r, c), lambda h: (0, 0))],
        out_specs=pl.BlockSpec((1, r, c), lambda h: (h, 0, 0)),
        out_shape=jax.ShapeDtypeStruct((n_heads, r, c), F32),
        compiler_params=_cparams(1),
        name="bias_build",
    )(table.astype(F32), jnp.asarray(idx_np))


def _ffn_kernel(x_ref, g_ref, w1_ref, w3_ref, w2_ref, fg_ref, o_ref, *, final):
    x = x_ref[...]
    xn = _rms(x, g_ref[...]).astype(BF16)
    h1 = _dot(xn, w1_ref[...])
    h3 = _dot(xn, w3_ref[...])
    act = (h1 * (1.0 / (1.0 + jnp.exp(-h1)))) * h3
    y = _dot(act.astype(BF16), w2_ref[...])
    out = x + 0.5 * y
    if final:
        out = _rms(out, fg_ref[...])
    o_ref[...] = out


def _ffn(h, g, w1, w3, w2, final_g, final):
    t, d = h.shape
    dff = w1.shape[1]
    return pl.pallas_call(
        functools.partial(_ffn_kernel, final=final),
        grid=(t // TM,),
        in_specs=[pl.BlockSpec((TM, d), lambda i: (i, 0)),
                  _const_spec((1, d)),
                  _const_spec((d, dff)), _const_spec((d, dff)), _const_spec((dff, d)),
                  _const_spec((1, d))],
        out_specs=pl.BlockSpec((TM, d), lambda i: (i, 0)),
        out_shape=jax.ShapeDtypeStruct((t, d), F32),
        compiler_params=_cparams(1),
        name="ffn",
    )(h, g.reshape(1, d), w1.astype(BF16), w3.astype(BF16), w2.astype(BF16), final_g.reshape(1, d))


def _fold_store(scr, blk, outs, n_slab):
    o1, o4, o16 = outs
    o1[...] = blk.astype(BF16)
    for c in range(n_slab):
        scr[c] = blk[:, c * LANES:(c + 1) * LANES]
    for o, dil in ((o4, 4), (o16, 16)):
        rows = TM // dil
        for r in range(dil):
            for c in range(n_slab):
                o[0, r, :, c * LANES:(c + 1) * LANES] = scr[c, pl.ds(r, rows, stride=dil), :].astype(BF16)


def _proj_even_kernel(h_ref, g_ref, w_ref, cqg_ref, wqb_ref, wqi_ref,
                      q1, k1, v1, q4, k4, v4, q16, k16, v16, qb, kb, vb, qi, ki, wi, scr,
                      *, aw, bw, rank):
    u = _rms(h_ref[...], g_ref[...]).astype(BF16)
    p = _dot(u, w_ref[...])
    n_slab = aw // LANES
    scale = HEAD_DIM ** -0.5
    _fold_store(scr, p[:, 0:aw] * scale, (q1, q4, q16), n_slab)
    _fold_store(scr, p[:, aw:2 * aw], (k1, k4, k16), n_slab)
    _fold_store(scr, p[:, 2 * aw:3 * aw], (v1, v4, v16), n_slab)
    c0 = 3 * aw
    kb[...] = p[:, c0:c0 + bw].astype(BF16)
    vb[...] = p[:, c0 + bw:c0 + 2 * bw].astype(BF16)
    c1 = c0 + 2 * bw
    cq = _rms(p[:, c1:c1 + rank], cqg_ref[...]).astype(BF16)
    qb[...] = (_dot(cq, wqb_ref[...]) * scale).astype(BF16)
    qi[...] = _dot(cq, wqi_ref[...]).astype(BF16)
    c2 = c1 + rank
    nidx = IDX_HEADS * IDX_DIM
    ki[...] = p[:, c2:c2 + nidx].astype(BF16)
    idx_scale = (IDX_DIM ** -0.5) * (IDX_HEADS ** -0.5)
    wi[...] = p[:, c2 + nidx:c2 + nidx + LANES] * idx_scale


def _proj_even(h, g, w_in, cq_g, wq_b, wq_idx, batch, seq):
    t, d = h.shape
    bw = wq_b.shape[1]
    rank = wq_b.shape[0]
    aw = (w_in.shape[1] - rank - 2 * bw - IDX_DIM - IDX_HEADS) // 3
    nidx = IDX_HEADS * IDX_DIM
    offs = np.cumsum([0, aw, aw, aw, rank, bw, bw, IDX_DIM, IDX_HEADS])
    col = lambda i: w_in[:, offs[i]:offs[i + 1]]
    w_kidx = jnp.tile(col(6), (1, IDX_HEADS))
    w_widx = jnp.pad(col(7), ((0, 0), (0, LANES - IDX_HEADS)))
    w_all = jnp.concatenate([col(0), col(1), col(2), col(4), col(5), col(3), w_kidx, w_widx], axis=1).astype(BF16)
    ncol = w_all.shape[1]
    spt = seq // TM
    tok = lambda width: pl.BlockSpec((TM, width), lambda i: (i, 0))
    fold = lambda dil: pl.BlockSpec((1, dil, TM // dil, aw), lambda i: (i // spt, 0, i % spt, 0))
    sh = jax.ShapeDtypeStruct
    nat = sh((t, aw), BF16)
    f4 = sh((batch, 4, seq // 4, aw), BF16)
    f16 = sh((batch, 16, seq // 16, aw), BF16)
    return pl.pallas_call(
        functools.partial(_proj_even_kernel, aw=aw, bw=bw, rank=rank),
        grid=(t // TM,),
        in_specs=[tok(d), _const_spec((1, d)), _const_spec((d, ncol)), _const_spec((1, rank)),
                  _const_spec((rank, bw)), _const_spec((rank, nidx))],
        out_specs=[tok(aw), tok(aw), tok(aw), fold(4), fold(4), fold(4), fold(16), fold(16), fold(16),
                   tok(bw), tok(bw), tok(bw), tok(nidx), tok(nidx), tok(LANES)],
        out_shape=[nat, nat, nat, f4, f4, f4, f16, f16, f16,
                   sh((t, bw), BF16), sh((t, bw), BF16), sh((t, bw), BF16),
                   sh((t, nidx), BF16), sh((t, nidx), BF16), sh((t, LANES), F32)],
        scratch_shapes=[pltpu.VMEM((aw // LANES, TM, LANES), F32)],
        compiler_params=_cparams(1),
        name="proj_even",
    )(h, g.reshape(1, d), w_all, cq_g.reshape(1, rank), wq_b.astype(BF16), wq_idx.astype(BF16))


def _band_block(q, kw, vw, bias_pair, lo):
    res = []
    lane_b = _lane_id_bf16(q.shape)
    for hh in range(2):
        qh = jnp.where((lane_b < HEAD_DIM) if hh == 0 else (lane_b >= HEAD_DIM), q, jnp.zeros_like(q))
        s = _dot_nt(qh, kw) + bias_pair(hh)
        m = jnp.max(s, axis=-1, keepdims=True)
        p = jnp.exp(s - m)
        den = jnp.sum(p, axis=-1, keepdims=True)
        acc = _dot(p.astype(BF16), vw)
        res.append((acc, m, den))
    (a0, m0, d0), (a1, m1, d1) = res
    shape = a0.shape
    return (jnp.where(lo, a0, a1),
            jnp.where(lo, jnp.broadcast_to(m0, shape), jnp.broadcast_to(m1, shape)),
            jnp.where(lo, jnp.broadcast_to(d0, shape), jnp.broadcast_to(d1, shape)))


def _band_branch(q_at, k_at, v_at, bias_ref, br, nblk, lo, emit):
    own = lambda hh: bias_ref[br, hh, :, BLOCK:2 * BLOCK]
    both = lambda hh: bias_ref[br, hh]
    emit(0, *_band_block(q_at(0, BLOCK), k_at(0, BLOCK), v_at(0, BLOCK), own, lo))
    if nblk > 1:
        def body(n, carry):
            q0 = pl.multiple_of(n * BLOCK, BLOCK)
            k0 = pl.multiple_of((n - 1) * BLOCK, BLOCK)
            emit(n, *_band_block(q_at(q0, BLOCK), k_at(k0, 2 * BLOCK), v_at(k0, 2 * BLOCK), both, lo))
            return carry

        lax.fori_loop(1, nblk, body, 0)


def _attn_a_kernel(q1, k1, v1, q4, k4, v4, q16, k16, v16, bias_ref, o_ref, acc_s, m_s, den_s, *, seq):
    lo = lax.broadcasted_iota(I32, (BLOCK, LANES), 1) < HEAD_DIM
    refs = ((q1, k1, v1), (q4, k4, v4), (q16, k16, v16))
    for br, (_, dil) in enumerate(A_BRANCHES):
        qr, kr, vr = refs[br]
        nblk = seq // dil // BLOCK
        for r in range(dil):
            if dil == 1:
                at = lambda ref: (lambda s, n: ref[0, pl.ds(s, n), :])
            else:
                at = lambda ref, r=r: (lambda s, n: ref[0, r, pl.ds(s, n), :])

            def emit(n, acc, m, den, br=br, dil=dil, r=r):
                if dil == 1:
                    rows = _rows(n * BLOCK, BLOCK)
                else:
                    rows = pl.ds(n * (BLOCK * dil) + r, BLOCK, stride=dil)
                acc_s[br, rows, :] = acc
                m_s[br, rows, :] = m
                den_s[br, rows, :] = den

            _band_branch(at(qr), at(kr), at(vr), bias_ref, br, nblk, lo, emit)

    chunk = 2 * BLOCK

    def merge(c, carry):
        rows = pl.ds(pl.multiple_of(c * chunk, chunk), chunk)
        ms = [m_s[br, rows, :] for br in range(3)]
        mx = jnp.maximum(jnp.maximum(ms[0], ms[1]), ms[2])
        num = jnp.zeros((chunk, LANES), F32)
        den = jnp.zeros((chunk, LANES), F32)
        for br in range(3):
            e = jnp.exp(ms[br] - mx)
            num = num + e * acc_s[br, rows, :]
            den = den + e * den_s[br, rows, :]
        o_ref[0, rows, :] = (num / den).astype(BF16)
        return carry

    lax.fori_loop(0, seq // chunk, merge, 0)


def _attn_a(qkv, bias, batch, seq):
    q1, k1, v1, q4, k4, v4, q16, k16, v16 = qkv
    aw = q1.shape[-1]
    npair = aw // LANES
    nat = pl.BlockSpec((1, seq, LANES), lambda b, p: (b, 0, p))
    fold = lambda dil: pl.BlockSpec((1, dil, seq // dil, LANES), lambda b, p: (b, 0, 0, p))
    r3 = lambda a: a.reshape(batch, seq, aw)
    return pl.pallas_call(
        functools.partial(_attn_a_kernel, seq=seq),
        grid=(batch, npair),
        in_specs=[nat, nat, nat, fold(4), fold(4), fold(4), fold(16), fold(16), fold(16),
                  pl.BlockSpec((3, 2, BLOCK, 2 * BLOCK), lambda b, p: (0, p, 0, 0))],
        out_specs=nat,
        out_shape=jax.ShapeDtypeStruct((batch, seq, aw), BF16),
        scratch_shapes=[pltpu.VMEM((3, seq, LANES), F32)] * 3,
        compiler_params=_cparams(2),
        name="attn_dilated",
    )(r3(q1), r3(k1), r3(v1), q4, k4, v4, q16, k16, v16, bias)


def _attn_c_kernel(q_ref, k_ref, v_ref, bias_ref, sink_ref, o_ref, *, seq):
    lo = lax.broadcasted_iota(I32, (BLOCK, LANES), 1) < HEAD_DIM
    at = lambda ref: (lambda s, n: ref[0, pl.ds(s, n), :])
    sink = sink_ref[0]

    def emit(n, acc, m, den):
        rows = _rows(n * BLOCK, BLOCK)
        o_ref[0, rows, :] = (acc / (den + jnp.exp(sink - m))).astype(BF16)

    _band_branch(at(q_ref), at(k_ref), at(v_ref), bias_ref, 0, seq // BLOCK, lo, emit)


def _attn_c(q, k2, v2, bias, sink_rows, batch, seq, group):
    npair = q.shape[-1] // LANES
    pairs_per_kv = group // 2
    return pl.pallas_call(
        functools.partial(_attn_c_kernel, seq=seq),
        grid=(batch, npair),
        in_specs=[pl.BlockSpec((1, seq, LANES), lambda b, p: (b, 0, p)),
                  pl.BlockSpec((1, seq, LANES), lambda b, p: (b, 0, p // pairs_per_kv)),
                  pl.BlockSpec((1, seq, LANES), lambda b, p: (b, 0, p // pairs_per_kv)),
                  pl.BlockSpec((1, 2, BLOCK, 2 * BLOCK), lambda b, p: (0, p, 0, 0)),
                  pl.BlockSpec((1, 1, LANES), lambda b, p: (p, 0, 0))],
        out_specs=pl.BlockSpec((1, seq, LANES), lambda b, p: (b, 0, p)),
        out_shape=jax.ShapeDtypeStruct((batch, seq, q.shape[-1]), BF16),
        compiler_params=_cparams(2),
        name="attn_swa",
    )(q, k2, v2, bias, sink_rows)


def _dsa_kernel(qi_ref, wi_ref, ki_ref, qb_ref, kb_ref, vb_ref, bias_ref, o_ref, sk_ref, mask_ref,
                *, seq, topk, n_heads):
    i = pl.program_id(1)
    nkb = seq // BLOCK
    qi = qi_ref[...]
    kt = ki_ref[0]
    w = wi_ref[...]
    lane_i = _lane_id_bf16(qi.shape)
    score = jnp.zeros((BLOCK, seq), F32)
    for h in range(IDX_HEADS):
        in_head = (lane_i >= h * IDX_DIM) & (lane_i < (h + 1) * IDX_DIM)
        qh = jnp.where(in_head, qi, jnp.zeros_like(qi))
        score = score + jnp.maximum(_dot_nt(qh, kt), 0.0) * w[:, h:h + 1]

    tpos = i * BLOCK + lax.broadcasted_iota(I32, (BLOCK, seq), 0)
    spos = lax.broadcasted_iota(I32, (BLOCK, seq), 1)
    causal = spos <= tpos
    score = jnp.where(score == 0.0, 0.0, score)
    score = jnp.where(causal, score, -jnp.inf)
    bits = lax.bitcast_convert_type(score, I32)
    sk = bits ^ (lax.shift_right_arithmetic(bits, 31) & 0x7FFFFFFF)
    sk_ref[...] = sk

    def count(pred):
        tot = jnp.zeros((BLOCK, BLOCK), F32)
        for j in range(nkb):
            tot = tot + pred(sk_ref[:, j * BLOCK:(j + 1) * BLOCK]).astype(F32)
        return jnp.sum(tot, axis=-1, keepdims=True)

    kf = float(topk)

    def search(it, ans):
        cand = ans | lax.shift_left(jnp.int32(1), 31 - it)
        cand_s = cand ^ INT_MIN
        cnt = count(lambda x: x >= cand_s)
        return jnp.where(cnt >= kf, cand, ans)

    ans = lax.fori_loop(0, 32, search, jnp.zeros((BLOCK, 1), I32))
    thr = ans ^ INT_MIN
    n_gt = count(lambda x: x > thr)
    n_eq = count(lambda x: x == thr)
    need = kf - n_gt
    sk = sk_ref[...]
    mask_ref[...] = jnp.where((sk >= thr) & causal, 0.0, NEG)

    excess = jnp.where((n_eq > need) & (thr > KEY_NEG_INF), 1.0, 0.0)

    @pl.when(jnp.max(excess) > 0.0)
    def _():
        r_i = lax.broadcasted_iota(I32, (BLOCK, BLOCK), 0)
        c_i = lax.broadcasted_iota(I32, (BLOCK, BLOCK), 1)
        upper = jnp.where(r_i <= c_i, 1.0, 0.0).astype(BF16)
        carry = jnp.zeros((BLOCK, 1), F32)
        for j in range(nkb):
            cols = slice(j * BLOCK, (j + 1) * BLOCK)
            skj = sk_ref[:, cols]
            eq = skj == thr
            rank = _dot(jnp.where(eq, 1.0, 0.0).astype(BF16), upper) + carry
            keep = (skj > thr) | (eq & (rank <= need))
            cj = (c_i + j * BLOCK) <= (r_i + i * BLOCK)
            mask_ref[:, cols] = jnp.where(keep & cj, 0.0, NEG)
            carry = carry + jnp.sum(jnp.where(eq, 1.0, 0.0), axis=-1, keepdims=True)

    lo = lax.broadcasted_iota(I32, (BLOCK, LANES), 1) < HEAD_DIM
    for pr in range(n_heads // 2):
        cols = slice(pr * LANES, (pr + 1) * LANES)
        q = qb_ref[:, cols]
        kk = kb_ref[0, :, cols]
        vv = vb_ref[0, :, cols]
        outs = []
        lane_b = _lane_id_bf16(q.shape)
        for hh in range(2):
            h = 2 * pr + hh
            qh = jnp.where((lane_b < HEAD_DIM) if hh == 0 else (lane_b >= HEAD_DIM), q, jnp.zeros_like(q))
            tiles = [bias_ref[h, jnp.maximum(i - j, 0)] for j in range(nkb)]
            s = _dot_nt(qh, kk) + jnp.concatenate(tiles, axis=1) + mask_ref[...]
            m = jnp.max(s, axis=-1, keepdims=True)
            p = jnp.exp(s - m)
            den = jnp.sum(p, axis=-1, keepdims=True)
            outs.append(_dot(p.astype(BF16), vv) / den)
        o_ref[:, cols] = jnp.where(lo, outs[0], outs[1]).astype(BF16)


def _dsa(qi, wi, ki, qb, kb, vb, bias, batch, seq, topk):
    bw = qb.shape[-1]
    nidx = qi.shape[-1]
    nb = seq // BLOCK
    n_heads = bw // HEAD_DIM
    qrow = lambda width: pl.BlockSpec((BLOCK, width), lambda b, i: (b * nb + i, 0))
    full = lambda width: pl.BlockSpec((1, seq, width), lambda b, i: (b, 0, 0))
    return pl.pallas_call(
        functools.partial(_dsa_kernel, seq=seq, topk=topk, n_heads=n_heads),
        grid=(batch, nb),
        in_specs=[qrow(nidx), qrow(LANES), full(nidx), qrow(bw), full(bw), full(bw),
                  _const_spec((n_heads, nb, BLOCK, BLOCK))],
        out_specs=qrow(bw),
        out_shape=jax.ShapeDtypeStruct((batch * seq, bw), BF16),
        scratch_shapes=[pltpu.VMEM((BLOCK, seq), I32), pltpu.VMEM((BLOCK, seq), F32)],
        compiler_params=_cparams(2),
        name="dsa",
    )(qi, wi, ki.reshape(batch, seq, nidx), qb, kb.reshape(batch, seq, bw), vb.reshape(batch, seq, bw), bias)


def _outproj_kernel(*refs, n_in):
    h_ref, o_ref = refs[0], refs[-1]
    acc = h_ref[...]
    for a_ref, w_ref in zip(refs[1:1 + n_in], refs[1 + n_in:1 + 2 * n_in]):
        acc = acc + _dot(a_ref[...], w_ref[...])
    o_ref[...] = acc


def _outproj(h, parts, w_out):
    t, d = h.shape
    ws, r0 = [], 0
    for a in parts:
        ws.append(w_out[r0:r0 + a.shape[1]].astype(BF16))
        r0 += a.shape[1]
    tok = lambda width: pl.BlockSpec((TM, width), lambda i: (i, 0))
    return pl.pallas_call(
        functools.partial(_outproj_kernel, n_in=len(parts)),
        grid=(t // TM,),
        in_specs=[tok(d)] + [tok(a.shape[1]) for a in parts] + [_const_spec(w.shape) for w in ws],
        out_specs=tok(d),
        out_shape=jax.ShapeDtypeStruct((t, d), F32),
        compiler_params=_cparams(1),
        name="outproj",
    )(h, *parts, *ws)


def _proj_odd_kernel(h_ref, g_ref, w_ref, q_ref, k_ref, v_ref, *, qw, kw):
    u = _rms(h_ref[...], g_ref[...]).astype(BF16)
    p = _dot(u, w_ref[...])
    q_ref[...] = (p[:, 0:qw] * (HEAD_DIM ** -0.5)).astype(BF16)
    k_ref[...] = p[:, qw:qw + kw].astype(BF16)
    v_ref[...] = p[:, qw + kw:qw + 2 * kw].astype(BF16)


def _proj_odd(h, g, w_in, qw, kvw):
    t, d = h.shape
    dup = lambda w: jnp.repeat(w.reshape(d, kvw // HEAD_DIM, 1, HEAD_DIM), 2, axis=2).reshape(d, 2 * kvw)
    w_all = jnp.concatenate([w_in[:, :qw], dup(w_in[:, qw:qw + kvw]), dup(w_in[:, qw + kvw:])], axis=1).astype(BF16)
    tok = lambda width: pl.BlockSpec((TM, width), lambda i: (i, 0))
    sh = jax.ShapeDtypeStruct
    return pl.pallas_call(
        functools.partial(_proj_odd_kernel, qw=qw, kw=2 * kvw),
        grid=(t // TM,),
        in_specs=[tok(d), _const_spec((1, d)), _const_spec(w_all.shape)],
        out_specs=[tok(qw), tok(2 * kvw), tok(2 * kvw)],
        out_shape=[sh((t, qw), BF16), sh((t, 2 * kvw), BF16), sh((t, 2 * kvw), BF16)],
        compiler_params=_cparams(1),
        name="proj_odd",
    )(h, g.reshape(1, d), w_all)


def kernel(x, norm_g, final_g, rel_bias_table, ffn_w1, ffn_w3, ffn_w2, hyb_w_in, hyb_cq_g, hyb_wq_b,
           hyb_wq_idx, hyb_w_out, swa_w_in, swa_sinks, swa_w_out):
    batch, seq, d = x.shape
    depth = norm_g.shape[0]
    n_heads = d // HEAD_DIM
    a_heads = n_heads // 2
    topk = min(TOPK_MAX, seq // 4)
    assert seq % (BLOCK * 16) == 0 and (batch * seq) % TM == 0 and seq % TM == 0

    bias_a = _build_bias(
        rel_bias_table,
        np.concatenate([_band_bucket_idx(dil, win // dil) for win, dil in A_BRANCHES], axis=0),
        0, a_heads).reshape(a_heads, 3, BLOCK, 2 * BLOCK).transpose(1, 0, 2, 3)
    bias_b = _build_bias(rel_bias_table, _dense_bucket_idx(seq), a_heads, n_heads - a_heads)
    bias_b = bias_b.reshape(n_heads - a_heads, seq // BLOCK, BLOCK, BLOCK)
    bias_c = _build_bias(rel_bias_table, _band_bucket_idx(1, C_WINDOW - 1), 0, n_heads)[None]

    h = x.reshape(batch * seq, d)
    for layer in range(depth):
        g = norm_g[layer]
        i = layer // 2
        h = _ffn(h, g[0], ffn_w1[layer, 0], ffn_w3[layer, 0], ffn_w2[layer, 0], final_g, False)
        if layer % 2 == 0:
            outs = _proj_even(h, g[1], hyb_w_in[i], hyb_cq_g[i], hyb_wq_b[i], hyb_wq_idx[i], batch, seq)
            o_a = _attn_a(outs[:9], bias_a, batch, seq)
            qb, kb, vb, qi, ki, wi = outs[9:]
            o_b = _dsa(qi, wi, ki, qb, kb, vb, bias_b, batch, seq, topk)
            h = _outproj(h, [o_a.reshape(batch * seq, -1), o_b], hyb_w_out[i])
        else:
            kvw = (swa_w_in.shape[-1] - d) // 2
            q, k2, v2 = _proj_odd(h, g[1], swa_w_in[i], d, kvw)
            group = n_heads // (kvw // HEAD_DIM)
            sink_rows = jnp.repeat(swa_sinks[i].astype(F32), HEAD_DIM).reshape(n_heads // 2, 1, LANES)
            r3 = lambda a: a.reshape(batch, seq, a.shape[-1])
            o_c = _attn_c(r3(q), r3(k2), r3(v2), bias_c, sink_rows, batch, seq, group)
            h = _outproj(h, [o_c.reshape(batch * seq, -1)], swa_w_out[i])
        last = layer == depth - 1
        h = _ffn(h, g[2], ffn_w1[layer, 1], ffn_w3[layer, 1], ffn_w2[layer, 1], final_g, last)
    return h.reshape(batch, seq, d)
```

```python
import functools
import math

import numpy as np
import jax
import jax.numpy as jnp
from jax import lax
from jax.experimental import pallas as pl
from jax.experimental.pallas import tpu as pltpu

F32 = jnp.float32
BF16 = jnp.bfloat16
I32 = jnp.int32

EPS = 1e-6
HEAD_DIM = 64
BLOCK = 128
LANES = 128
NUM_BUCKETS = 32
MAX_DISTANCE = 2048
A_BRANCHES = ((128, 1), (512, 4), (2048, 16))
C_WINDOW = 128
TOPK_MAX = 256
IDX_HEADS = 8
IDX_DIM = 32
NEG = -1e30
M_INIT = -3e38
INT_MIN = -(2 ** 31)
KEY_NEG_INF = (0xFF800000 - (1 << 32)) ^ 0x7FFFFFFF
VMEM_LIMIT = 56 * 1024 * 1024

TM = 512
CW = 4 * BLOCK


def _cparams(n_axes):
    return pltpu.CompilerParams(
        dimension_semantics=("arbitrary",) * n_axes, vmem_limit_bytes=VMEM_LIMIT)


def _rms(x, g):
    return x * lax.rsqrt(jnp.mean(x * x, axis=-1, keepdims=True) + EPS) * g


def _dot(a, b):
    return jnp.dot(a, b, preferred_element_type=F32)


def _dot_nt(a, b):
    return lax.dot_general(a, b, (((1,), (1,)), ((), ())), preferred_element_type=F32)


def _lane_id_bf16(shape):
    return lax.broadcasted_iota(I32, shape, 1).astype(F32).astype(BF16)


def _rows(start, size):
    if isinstance(start, int):
        return pl.ds(start, size)
    return pl.ds(pl.multiple_of(start, size), size)


def _const_spec(shape):
    nd = len(shape)
    return pl.BlockSpec(shape, lambda *_: (0,) * nd, pipeline_mode=pl.Buffered(1))


def _t5_bucket_np(dist):
    dist = np.maximum(dist, 0)
    max_exact = NUM_BUCKETS // 2
    d_f = np.maximum(dist, 1).astype(np.float32)
    large = max_exact + (np.log(d_f / max_exact) / math.log(MAX_DISTANCE / max_exact)
                         * (NUM_BUCKETS - max_exact)).astype(np.int32)
    large = np.minimum(large, NUM_BUCKETS - 1)
    return np.where(dist < max_exact, dist, large).astype(np.int32)


def _band_bucket_idx(dil, max_dist):
    loc = np.arange(BLOCK)[:, None] + BLOCK - np.arange(2 * BLOCK)[None, :]
    valid = (loc >= 0) & (loc <= max_dist)
    return np.where(valid, _t5_bucket_np(dil * loc), -1).astype(np.int32)


def _dense_bucket_idx(seq):
    nb = seq // BLOCK
    d = np.arange(nb)[:, None, None] * BLOCK + np.arange(BLOCK)[None, :, None] - np.arange(BLOCK)[None, None, :]
    return np.where(d >= 0, _t5_bucket_np(d), -1).astype(np.int32).reshape(seq, BLOCK)


def _bias_kernel(tab_ref, idx_ref, o_ref, *, head0):
    h = pl.program_id(0) + head0
    idx = idx_ref[...]
    out = jnp.full(idx.shape, NEG, F32)
    for b in range(NUM_BUCKETS):
        out = jnp.where(idx == b, tab_ref[b, h], out)
    o_ref[0] = out


def _build_bias(table, idx_np, head0, n_heads):
    r, c = idx_np.shape
    return pl.pallas_call(
        functools.partial(_bias_kernel, head0=head0),
        grid=(n_heads,),
        in_specs=[pl.BlockSpec(memory_space=pltpu.SMEM),
                  pl.BlockSpec((r, c), lambda h: (0, 0))],
        out_specs=pl.BlockSpec((1, r, c), lambda h: (h, 0, 0)),
        out_shape=jax.ShapeDtypeStruct((n_heads, r, c), F32),
        compiler_params=_cparams(1),
        name="bias_build",
    )(table.astype(F32), jnp.asarray(idx_np))


def _ffn_kernel(x_ref, g_ref, w1_ref, w3_ref, w2_ref, fg_ref, o_ref, *, final):
    x = x_ref[...]
    xn = _rms(x, g_ref[...]).astype(BF16)
    h1 = _dot(xn, w1_ref[...])
    h3 = _dot(xn, w3_ref[...])
    act = (h1 * (1.0 / (1.0 + jnp.exp(-h1)))) * h3
    y = _dot(act.astype(BF16), w2_ref[...])
    out = x + 0.5 * y
    if final:
        out = _rms(out, fg_ref[...])
    o_ref[...] = out


def _ffn(h, g, w1, w3, w2, final_g, final):
    t, d = h.shape
    dff = w1.shape[1]
    return pl.pallas_call(
        functools.partial(_ffn_kernel, final=final),
        grid=(t // TM,),
        in_specs=[pl.BlockSpec((TM, d), lambda i: (i, 0)),
                  _const_spec((1, d)),
                  _const_spec((d, dff)), _const_spec((d, dff)), _const_spec((dff, d)),
                  _const_spec((1, d))],
        out_specs=pl.BlockSpec((TM, d), lambda i: (i, 0)),
        out_shape=jax.ShapeDtypeStruct((t, d), F32),
        compiler_params=_cparams(1),
        name="ffn",
    )(h, g.reshape(1, d), w1.astype(BF16), w3.astype(BF16), w2.astype(BF16), final_g.reshape(1, d))


def _fold_store(scr, blk, outs, n_slab):
    o1, o4, o16 = outs
    o1[...] = blk.astype(BF16)
    for c in range(n_slab):
        scr[c] = blk[:, c * LANES:(c + 1) * LANES]
    for o, dil in ((o4, 4), (o16, 16)):
        rows = TM // dil
        for r in range(dil):
            for c in range(n_slab):
                o[0, r, :, c * LANES:(c + 1) * LANES] = scr[c, pl.ds(r, rows, stride=dil), :].astype(BF16)


def _proj_even_kernel(h_ref, g_ref, w_ref, cqg_ref, wqb_ref, wqi_ref,
                      q1, k1, v1, q4, k4, v4, q16, k16, v16, qb, kb, vb, qi, ki, wi, scr,
                      *, aw, bw, rank):
    u = _rms(h_ref[...], g_ref[...]).astype(BF16)
    p = _dot(u, w_ref[...])
    n_slab = aw // LANES
    scale = HEAD_DIM ** -0.5
    _fold_store(scr, p[:, 0:aw] * scale, (q1, q4, q16), n_slab)
    _fold_store(scr, p[:, aw:2 * aw], (k1, k4, k16), n_slab)
    _fold_store(scr, p[:, 2 * aw:3 * aw], (v1, v4, v16), n_slab)
    c0 = 3 * aw
    kb[...] = p[:, c0:c0 + bw].astype(BF16)
    vb[...] = p[:, c0 + bw:c0 + 2 * bw].astype(BF16)
    c1 = c0 + 2 * bw
    cq = _rms(p[:, c1:c1 + rank], cqg_ref[...]).astype(BF16)
    qb[...] = (_dot(cq, wqb_ref[...]) * scale).astype(BF16)
    qi[...] = _dot(cq, wqi_ref[...]).astype(BF16)
    c2 = c1 + rank
    nidx = IDX_HEADS * IDX_DIM
    ki[...] = p[:, c2:c2 + nidx].astype(BF16)
    idx_scale = (IDX_DIM ** -0.5) * (IDX_HEADS ** -0.5)
    wi[...] = p[:, c2 + nidx:c2 + nidx + LANES] * idx_scale


def _proj_even(h, g, w_in, cq_g, wq_b, wq_idx, batch, seq):
    t, d = h.shape
    bw = wq_b.shape[1]
    rank = wq_b.shape[0]
    aw = (w_in.shape[1] - rank - 2 * bw - IDX_DIM - IDX_HEADS) // 3
    nidx = IDX_HEADS * IDX_DIM
    offs = np.cumsum([0, aw, aw, aw, rank, bw, bw, IDX_DIM, IDX_HEADS])
    col = lambda i: w_in[:, offs[i]:offs[i + 1]]
    w_kidx = jnp.tile(col(6), (1, IDX_HEADS))
    w_widx = jnp.pad(col(7), ((0, 0), (0, LANES - IDX_HEADS)))
    w_all = jnp.concatenate([col(0), col(1), col(2), col(4), col(5), col(3), w_kidx, w_widx], axis=1).astype(BF16)
    ncol = w_all.shape[1]
    spt = seq // TM
    tok = lambda width: pl.BlockSpec((TM, width), lambda i: (i, 0))
    fold = lambda dil: pl.BlockSpec((1, dil, TM // dil, aw), lambda i: (i // spt, 0, i % spt, 0))
    sh = jax.ShapeDtypeStruct
    nat = sh((t, aw), BF16)
    f4 = sh((batch, 4, seq // 4, aw), BF16)
    f16 = sh((batch, 16, seq // 16, aw), BF16)
    return pl.pallas_call(
        functools.partial(_proj_even_kernel, aw=aw, bw=bw, rank=rank),
        grid=(t // TM,),
        in_specs=[tok(d), _const_spec((1, d)), _const_spec((d, ncol)), _const_spec((1, rank)),
                  _const_spec((rank, bw)), _const_spec((rank, nidx))],
        out_specs=[tok(aw), tok(aw), tok(aw), fold(4), fold(4), fold(4), fold(16), fold(16), fold(16),
                   tok(bw), tok(bw), tok(bw), tok(nidx), tok(nidx), tok(LANES)],
        out_shape=[nat, nat, nat, f4, f4, f4, f16, f16, f16,
                   sh((t, bw), BF16), sh((t, bw), BF16), sh((t, bw), BF16),
                   sh((t, nidx), BF16), sh((t, nidx), BF16), sh((t, LANES), F32)],
        scratch_shapes=[pltpu.VMEM((aw // LANES, TM, LANES), F32)],
        compiler_params=_cparams(1),
        name="proj_even",
    )(h, g.reshape(1, d), w_all, cq_g.reshape(1, rank), wq_b.astype(BF16), wq_idx.astype(BF16))


def _stack_heads(q):
    lane_b = _lane_id_bf16(q.shape)
    z = jnp.zeros_like(q)
    return jnp.concatenate([jnp.where(lane_b < HEAD_DIM, q, z), jnp.where(lane_b >= HEAD_DIM, q, z)], axis=0)


def _band_block(q2, kw, vw, bias2):
    s = _dot_nt(q2, kw) + bias2
    m = jnp.max(s, axis=-1, keepdims=True)
    p = jnp.exp(s - m)
    den = jnp.sum(p, axis=-1, keepdims=True)
    return _dot(p.astype(BF16), vw), m, den


def _attn_a_kernel(q1, k1, v1, q4, k4, v4, q16, k16, v16, bias_ref, o_ref, acc_s, m_s, den_s, *, seq):
    lo = lax.broadcasted_iota(I32, (BLOCK, LANES), 1) < HEAD_DIM
    shape = (BLOCK, LANES)
    refs = ((q1, k1, v1), (q4, k4, v4), (q16, k16, v16))
    for br, (_, dil) in enumerate(A_BRANCHES):
        qr, kr, vr = refs[br]
        for r in range(dil):
            at = (lambda ref, s, n: ref[0, pl.ds(s, n), :]) if dil == 1 else \
                 (lambda ref, s, n, r=r: ref[0, r, pl.ds(s, n), :])
            for n in range(seq // dil // BLOCK):
                q2 = _stack_heads(at(qr, n * BLOCK, BLOCK))
                if n == 0:
                    acc, m, den = _band_block(q2, at(kr, 0, BLOCK), at(vr, 0, BLOCK), bias_ref[br, 0, :, BLOCK:])
                else:
                    k0 = (n - 1) * BLOCK
                    acc, m, den = _band_block(q2, at(kr, k0, 2 * BLOCK), at(vr, k0, 2 * BLOCK), bias_ref[br, 0])
                rows = pl.ds(n * BLOCK, BLOCK) if dil == 1 else pl.ds(n * BLOCK * dil + r, BLOCK, stride=dil)
                acc_s[br, rows, :] = jnp.where(lo, acc[:BLOCK], acc[BLOCK:])
                m_s[br, rows, :] = jnp.where(lo, jnp.broadcast_to(m[:BLOCK], shape), jnp.broadcast_to(m[BLOCK:], shape))
                den_s[br, rows, :] = jnp.where(lo, jnp.broadcast_to(den[:BLOCK], shape),
                                               jnp.broadcast_to(den[BLOCK:], shape))

    chunk = 2 * BLOCK

    def merge(c, carry):
        rows = pl.ds(pl.multiple_of(c * chunk, chunk), chunk)
        ms = [m_s[br, rows, :] for br in range(3)]
        mx = jnp.maximum(jnp.maximum(ms[0], ms[1]), ms[2])
        num = jnp.zeros((chunk, LANES), F32)
        den = jnp.zeros((chunk, LANES), F32)
        for br in range(3):
            e = jnp.exp(ms[br] - mx)
            num = num + e * acc_s[br, rows, :]
            den = den + e * den_s[br, rows, :]
        o_ref[0, rows, :] = (num / den).astype(BF16)
        return carry

    lax.fori_loop(0, seq // chunk, merge, 0)


def _attn_a(qkv, bias, batch, seq):
    q1, k1, v1, q4, k4, v4, q16, k16, v16 = qkv
    aw = q1.shape[-1]
    npair = aw // LANES
    nat = pl.BlockSpec((1, seq, LANES), lambda b, p: (b, 0, p))
    fold = lambda dil: pl.BlockSpec((1, dil, seq // dil, LANES), lambda b, p: (b, 0, 0, p))
    r3 = lambda a: a.reshape(batch, seq, aw)
    return pl.pallas_call(
        functools.partial(_attn_a_kernel, seq=seq),
        grid=(batch, npair),
        in_specs=[nat, nat, nat, fold(4), fold(4), fold(4), fold(16), fold(16), fold(16),
                  pl.BlockSpec((3, 1, 2 * BLOCK, 2 * BLOCK), lambda b, p: (0, p, 0, 0))],
        out_specs=nat,
        out_shape=jax.ShapeDtypeStruct((batch, seq, aw), BF16),
        scratch_shapes=[pltpu.VMEM((3, seq, LANES), F32)] * 3,
        compiler_params=_cparams(2),
        name="attn_dilated",
    )(r3(q1), r3(k1), r3(v1), q4, k4, v4, q16, k16, v16, bias)


def _attn_c_kernel(sink_ref, q_ref, k_ref, v_ref, bias_ref, o_ref, *, seq, group):
    g = pl.program_id(1)
    npair = group // 2
    lo = lax.broadcasted_iota(I32, (BLOCK, LANES), 1) < HEAD_DIM

    def block(q0, k0, width, bias2):
        qn = q_ref[0, pl.ds(q0, BLOCK), :]
        q2 = jnp.concatenate([_stack_heads(qn[:, p * LANES:(p + 1) * LANES]) for p in range(npair)], axis=0)
        acc, m, den = _band_block(q2, k_ref[0, pl.ds(k0, width), :], v_ref[0, pl.ds(k0, width), :], bias2)
        for p in range(npair):
            outs = []
            for hh in range(2):
                h = 2 * p + hh
                r = slice(h * BLOCK, (h + 1) * BLOCK)
                outs.append(acc[r] / (den[r] + jnp.exp(sink_ref[g * group + h] - m[r])))
            o_ref[0, pl.ds(q0, BLOCK), p * LANES:(p + 1) * LANES] = jnp.where(lo, outs[0], outs[1]).astype(BF16)

    block(0, 0, BLOCK, bias_ref[0, :, BLOCK:])

    def body(n, carry):
        block(pl.multiple_of(n * BLOCK, BLOCK), pl.multiple_of((n - 1) * BLOCK, BLOCK), 2 * BLOCK, bias_ref[0])
        return carry

    lax.fori_loop(1, seq // BLOCK, body, 0)


def _attn_c(q, k2, v2, bias, sinks, batch, seq, group):
    qw = q.shape[-1]
    n_kv = k2.shape[-1] // LANES
    gw = qw // n_kv
    return pl.pallas_call(
        functools.partial(_attn_c_kernel, seq=seq, group=group),
        grid=(batch, n_kv),
        in_specs=[pl.BlockSpec(memory_space=pltpu.SMEM),
                  pl.BlockSpec((1, seq, gw), lambda b, g: (b, 0, g)),
                  pl.BlockSpec((1, seq, LANES), lambda b, g: (b, 0, g)),
                  pl.BlockSpec((1, seq, LANES), lambda b, g: (b, 0, g)),
                  pl.BlockSpec((1, group * BLOCK, 2 * BLOCK), lambda b, g: (g, 0, 0))],
        out_specs=pl.BlockSpec((1, seq, gw), lambda b, g: (b, 0, g)),
        out_shape=jax.ShapeDtypeStruct((batch, seq, qw), BF16),
        compiler_params=_cparams(2),
        name="attn_swa",
    )(sinks, q, k2, v2, bias)


def _dsa_kernel(qi_ref, wi_ref, ki_ref, qb_ref, kb_ref, vb_ref, bias_ref, o_ref,
                qs_ref, ws_ref, sk_ref, mask_ref, q2_ref, acc_ref, m_ref, den_ref, *, topk, n_heads):
    i = pl.program_id(1)
    tpc = CW // BLOCK
    nch = i // tpc + 1
    npair = n_heads // 2
    row = lax.broadcasted_iota(I32, (BLOCK, CW), 0) + i * BLOCK
    col = lax.broadcasted_iota(I32, (BLOCK, CW), 1)

    qi = qi_ref[...]
    lane_i = _lane_id_bf16(qi.shape)
    w = wi_ref[...]
    for h in range(IDX_HEADS):
        in_head = (lane_i >= h * IDX_DIM) & (lane_i < (h + 1) * IDX_DIM)
        qs_ref[h * BLOCK:(h + 1) * BLOCK, :] = jnp.where(in_head, qi, jnp.zeros_like(qi))
        ws_ref[h * BLOCK:(h + 1) * BLOCK, :] = w[:, h:h + 1]

    def scores(c, carry):
        c0 = pl.multiple_of(c * CW, CW)
        sc = jnp.maximum(_dot_nt(qs_ref[...], ki_ref[0, pl.ds(c0, CW), :]), 0.0) * ws_ref[...]
        score = sc[0:BLOCK]
        for h in range(1, IDX_HEADS):
            score = score + sc[h * BLOCK:(h + 1) * BLOCK]
        score = jnp.where(score == 0.0, 0.0, score)
        score = jnp.where(col + c0 <= row, score, -jnp.inf)
        bits = lax.bitcast_convert_type(score, I32)
        sk_ref[c] = bits ^ (lax.shift_right_arithmetic(bits, 31) & 0x7FFFFFFF)
        return carry

    lax.fori_loop(0, nch, scores, 0)

    def count(pred):
        def chunk(c, tot):
            x = sk_ref[c]
            for t in range(tpc):
                tot = jnp.where(pred(x[:, t * BLOCK:(t + 1) * BLOCK]), tot + 1.0, tot)
            return tot
        tot = lax.fori_loop(0, nch, chunk, jnp.zeros((BLOCK, BLOCK), F32))
        return jnp.sum(tot, axis=-1, keepdims=True)

    kf = float(topk)

    def search(it, ans):
        cand = ans | lax.shift_left(jnp.int32(1), 31 - it)
        cand_s = cand ^ INT_MIN
        return jnp.where(count(lambda x: x >= cand_s) >= kf, cand, ans)

    ans = lax.fori_loop(0, 32, search, jnp.zeros((BLOCK, 1), I32))
    thr = ans ^ INT_MIN
    n_gt = count(lambda x: x > thr)
    n_eq = count(lambda x: x == thr)
    need = kf - n_gt

    def make_mask(c, carry):
        causal = col + c * CW <= row
        mask_ref[c] = jnp.where((sk_ref[c] >= thr) & causal, 0.0, NEG)
        return carry

    lax.fori_loop(0, nch, make_mask, 0)

    excess = jnp.where((n_eq > need) & (thr > KEY_NEG_INF), 1.0, 0.0)

    @pl.when(jnp.max(excess) > 0.0)
    def _():
        r_i = lax.broadcasted_iota(I32, (BLOCK, BLOCK), 0)
        c_i = lax.broadcasted_iota(I32, (BLOCK, BLOCK), 1)
        upper = jnp.where(r_i <= c_i, 1.0, 0.0).astype(BF16)

        def tie_chunk(c, carry):
            x = sk_ref[c]
            for t in range(tpc):
                cols = slice(t * BLOCK, (t + 1) * BLOCK)
                xs = x[:, cols]
                eq = xs == thr
                eqf = jnp.where(eq, 1.0, 0.0)
                rank = _dot(eqf.astype(BF16), upper) + carry
                keep = (xs > thr) | (eq & (rank <= need))
                causal = c_i + (c * CW + t * BLOCK) <= r_i + i * BLOCK
                mask_ref[c, :, cols] = jnp.where(keep & causal, 0.0, NEG)
                carry = carry + jnp.sum(eqf, axis=-1, keepdims=True)
            return carry

        lax.fori_loop(0, nch, tie_chunk, jnp.zeros((BLOCK, 1), F32))

    for p in range(npair):
        q2_ref[p] = _stack_heads(qb_ref[:, p * LANES:(p + 1) * LANES])
    m_ref[...] = jnp.full(m_ref.shape, M_INIT, F32)
    den_ref[...] = jnp.zeros(den_ref.shape, F32)
    acc_ref[...] = jnp.zeros(acc_ref.shape, F32)

    def attend(c, carry):
        c0 = pl.multiple_of(c * CW, CW)
        msk = mask_ref[c]
        msk2 = jnp.concatenate([msk, msk], axis=0)
        for p in range(npair):
            lanes = slice(p * LANES, (p + 1) * LANES)
            bias2 = jnp.concatenate(
                [jnp.concatenate([bias_ref[2 * p + hh, jnp.maximum(i - (c * tpc + t), 0)] for t in range(tpc)], axis=1)
                 for hh in range(2)], axis=0)
            s = _dot_nt(q2_ref[p], kb_ref[0, pl.ds(c0, CW), lanes]) + bias2 + msk2
            m_old = m_ref[p]
            m_new = jnp.maximum(m_old, jnp.max(s, axis=-1, keepdims=True))
            alpha = jnp.exp(m_old - m_new)
            pe = jnp.exp(s - m_new)
            den_ref[p] = alpha * den_ref[p] + jnp.sum(pe, axis=-1, keepdims=True)
            acc_ref[p] = alpha * acc_ref[p] + _dot(pe.astype(BF16), vb_ref[0, pl.ds(c0, CW), lanes])
            m_ref[p] = m_new
        return carry

    lax.fori_loop(0, nch, attend, 0)

    lo = lax.broadcasted_iota(I32, (BLOCK, LANES), 1) < HEAD_DIM
    for p in range(npair):
        o = acc_ref[p] / den_ref[p]
        o_ref[:, p * LANES:(p + 1) * LANES] = jnp.where(lo, o[:BLOCK], o[BLOCK:]).astype(BF16)


def _dsa(qi, wi, ki, qb, kb, vb, bias, batch, seq, topk):
    bw = qb.shape[-1]
    nidx = qi.shape[-1]
    nb = seq // BLOCK
    n_heads = bw // HEAD_DIM
    npair = n_heads // 2
    qrow = lambda width: pl.BlockSpec((BLOCK, width), lambda b, i: (b * nb + i, 0))
    full = lambda width: pl.BlockSpec((1, seq, width), lambda b, i: (b, 0, 0))
    vm = pltpu.VMEM
    return pl.pallas_call(
        functools.partial(_dsa_kernel, topk=topk, n_heads=n_heads),
        grid=(batch, nb),
        in_specs=[qrow(nidx), qrow(LANES), full(nidx), qrow(bw), full(bw), full(bw),
                  _const_spec((n_heads, nb, BLOCK, BLOCK))],
        out_specs=qrow(bw),
        out_shape=jax.ShapeDtypeStruct((batch * seq, bw), BF16),
        scratch_shapes=[vm((IDX_HEADS * BLOCK, nidx), BF16), vm((IDX_HEADS * BLOCK, 1), F32),
                        vm((seq // CW, BLOCK, CW), I32), vm((seq // CW, BLOCK, CW), F32),
                        vm((npair, 2 * BLOCK, LANES), BF16), vm((npair, 2 * BLOCK, LANES), F32),
                        vm((npair, 2 * BLOCK, 1), F32), vm((npair, 2 * BLOCK, 1), F32)],
        compiler_params=_cparams(2),
        name="dsa",
    )(qi, wi, ki.reshape(batch, seq, nidx), qb, kb.reshape(batch, seq, bw), vb.reshape(batch, seq, bw), bias)


def _outproj_kernel(*refs, n_in):
    h_ref, o_ref = refs[0], refs[-1]
    acc = h_ref[...]
    for a_ref, w_ref in zip(refs[1:1 + n_in], refs[1 + n_in:1 + 2 * n_in]):
        acc = acc + _dot(a_ref[...], w_ref[...])
    o_ref[...] = acc


def _outproj(h, parts, w_out):
    t, d = h.shape
    ws, r0 = [], 0
    for a in parts:
        ws.append(w_out[r0:r0 + a.shape[1]].astype(BF16))
        r0 += a.shape[1]
    tok = lambda width: pl.BlockSpec((TM, width), lambda i: (i, 0))
    return pl.pallas_call(
        functools.partial(_outproj_kernel, n_in=len(parts)),
        grid=(t // TM,),
        in_specs=[tok(d)] + [tok(a.shape[1]) for a in parts] + [_const_spec(w.shape) for w in ws],
        out_specs=tok(d),
        out_shape=jax.ShapeDtypeStruct((t, d), F32),
        compiler_params=_cparams(1),
        name="outproj",
    )(h, *parts, *ws)


def _proj_odd_kernel(h_ref, g_ref, w_ref, q_ref, k_ref, v_ref, *, qw, kw):
    u = _rms(h_ref[...], g_ref[...]).astype(BF16)
    p = _dot(u, w_ref[...])
    q_ref[...] = (p[:, 0:qw] * (HEAD_DIM ** -0.5)).astype(BF16)
    k_ref[...] = p[:, qw:qw + kw].astype(BF16)
    v_ref[...] = p[:, qw + kw:qw + 2 * kw].astype(BF16)


def _proj_odd(h, g, w_in, qw, kvw):
    t, d = h.shape
    dup = lambda w: jnp.repeat(w.reshape(d, kvw // HEAD_DIM, 1, HEAD_DIM), 2, axis=2).reshape(d, 2 * kvw)
    w_all = jnp.concatenate([w_in[:, :qw], dup(w_in[:, qw:qw + kvw]), dup(w_in[:, qw + kvw:])], axis=1).astype(BF16)
    tok = lambda width: pl.BlockSpec((TM, width), lambda i: (i, 0))
    sh = jax.ShapeDtypeStruct
    return pl.pallas_call(
        functools.partial(_proj_odd_kernel, qw=qw, kw=2 * kvw),
        grid=(t // TM,),
        in_specs=[tok(d), _const_spec((1, d)), _const_spec(w_all.shape)],
        out_specs=[tok(qw), tok(2 * kvw), tok(2 * kvw)],
        out_shape=[sh((t, qw), BF16), sh((t, 2 * kvw), BF16), sh((t, 2 * kvw), BF16)],
        compiler_params=_cparams(1),
        name="proj_odd",
    )(h, g.reshape(1, d), w_all)


def kernel(x, norm_g, final_g, rel_bias_table, ffn_w1, ffn_w3, ffn_w2, hyb_w_in, hyb_cq_g, hyb_wq_b,
           hyb_wq_idx, hyb_w_out, swa_w_in, swa_sinks, swa_w_out):
    batch, seq, d = x.shape
    depth = norm_g.shape[0]
    n_heads = d // HEAD_DIM
    a_heads = n_heads // 2
    topk = min(TOPK_MAX, seq // 4)
    assert seq % (BLOCK * 16) == 0 and seq % CW == 0 and (batch * seq) % TM == 0 and seq % TM == 0

    bias_a = _build_bias(
        rel_bias_table,
        np.concatenate([_band_bucket_idx(dil, win // dil) for win, dil in A_BRANCHES], axis=0),
        0, a_heads).reshape(a_heads // 2, 2, 3, BLOCK, 2 * BLOCK).transpose(2, 0, 1, 3, 4)
    bias_a = bias_a.reshape(3, a_heads // 2, 2 * BLOCK, 2 * BLOCK)
    bias_b = _build_bias(rel_bias_table, _dense_bucket_idx(seq), a_heads, n_heads - a_heads)
    bias_b = bias_b.reshape(n_heads - a_heads, seq // BLOCK, BLOCK, BLOCK)
    bias_c = _build_bias(rel_bias_table, _band_bucket_idx(1, C_WINDOW - 1), 0, n_heads)

    h = x.reshape(batch * seq, d)
    for layer in range(depth):
        g = norm_g[layer]
        i = layer // 2
        h = _ffn(h, g[0], ffn_w1[layer, 0], ffn_w3[layer, 0], ffn_w2[layer, 0], final_g, False)
        if layer % 2 == 0:
            outs = _proj_even(h, g[1], hyb_w_in[i], hyb_cq_g[i], hyb_wq_b[i], hyb_wq_idx[i], batch, seq)
            o_a = _attn_a(outs[:9], bias_a, batch, seq)
            qb, kb, vb, qi, ki, wi = outs[9:]
            o_b = _dsa(qi, wi, ki, qb, kb, vb, bias_b, batch, seq, topk)
            h = _outproj(h, [o_a.reshape(batch * seq, -1), o_b], hyb_w_out[i])
        else:
            kvw = (swa_w_in.shape[-1] - d) // 2
            n_kv = kvw // HEAD_DIM
            group = n_heads // n_kv
            q, k2, v2 = _proj_odd(h, g[1], swa_w_in[i], d, kvw)
            r3 = lambda a: a.reshape(batch, seq, a.shape[-1])
            o_c = _attn_c(r3(q), r3(k2), r3(v2), bias_c.reshape(n_kv, group * BLOCK, 2 * BLOCK),
                          swa_sinks[i].astype(F32), batch, seq, group)
            h = _outproj(h, [o_c.reshape(batch * seq, -1)], swa_w_out[i])
        last = layer == depth - 1
        h = _ffn(h, g[2], ffn_w1[layer, 1], ffn_w3[layer, 1], ffn_w2[layer, 1], final_g, last)
    return h.reshape(batch, seq, d)
```

```python
import functools
import math

import numpy as np
import jax
import jax.numpy as jnp
from jax import lax
from jax.experimental import pallas as pl
from jax.experimental.pallas import tpu as pltpu

F32 = jnp.float32
BF16 = jnp.bfloat16
I32 = jnp.int32

EPS = 1e-6
HEAD_DIM = 64
BLOCK = 128
LANES = 128
NUM_BUCKETS = 32
MAX_DISTANCE = 2048
A_BRANCHES = ((128, 1), (512, 4), (2048, 16))
C_WINDOW = 128
TOPK_MAX = 256
IDX_HEADS = 8
IDX_DIM = 32
NEG = -1e30
INT_MIN = -(2 ** 31)
KEY_NEG_INF = (0xFF800000 - (1 << 32)) ^ 0x7FFFFFFF
VMEM_LIMIT = 56 * 1024 * 1024

TM = 512
CW = 4 * BLOCK


def _cparams(n_axes):
    return pltpu.CompilerParams(
        dimension_semantics=("arbitrary",) * n_axes, vmem_limit_bytes=VMEM_LIMIT)


def _rms(x, g):
    return x * lax.rsqrt(jnp.mean(x * x, axis=-1, keepdims=True) + EPS) * g


def _dot(a, b):
    return jnp.dot(a, b, preferred_element_type=F32)


def _dot_nt(a, b):
    return lax.dot_general(a, b, (((1,), (1,)), ((), ())), preferred_element_type=F32)


def _lane_id_bf16(shape):
    return lax.broadcasted_iota(I32, shape, 1).astype(F32).astype(BF16)


def _rows(start, size):
    if isinstance(start, int):
        return pl.ds(start, size)
    return pl.ds(pl.multiple_of(start, size), size)


def _const_spec(shape):
    nd = len(shape)
    return pl.BlockSpec(shape, lambda *_: (0,) * nd, pipeline_mode=pl.Buffered(1))


def _t5_bucket_np(dist):
    dist = np.maximum(dist, 0)
    max_exact = NUM_BUCKETS // 2
    d_f = np.maximum(dist, 1).astype(np.float32)
    large = max_exact + (np.log(d_f / max_exact) / math.log(MAX_DISTANCE / max_exact)
                         * (NUM_BUCKETS - max_exact)).astype(np.int32)
    large = np.minimum(large, NUM_BUCKETS - 1)
    return np.where(dist < max_exact, dist, large).astype(np.int32)


def _band_bucket_idx(dil, max_dist):
    loc = np.arange(BLOCK)[:, None] + BLOCK - np.arange(2 * BLOCK)[None, :]
    valid = (loc >= 0) & (loc <= max_dist)
    return np.where(valid, _t5_bucket_np(dil * loc), -1).astype(np.int32)


def _dense_bucket_idx(seq):
    nb = seq // BLOCK
    d = np.arange(nb)[:, None, None] * BLOCK - np.arange(BLOCK)[None, :, None] + np.arange(BLOCK)[None, None, :]
    return np.where(d >= 0, _t5_bucket_np(d), -1).astype(np.int32).reshape(seq, BLOCK)


def _bias_kernel(tab_ref, idx_ref, o_ref, *, head0):
    h = pl.program_id(0) + head0
    idx = idx_ref[...]
    out = jnp.full(idx.shape, NEG, F32)
    for b in range(NUM_BUCKETS):
        out = jnp.where(idx == b, tab_ref[b, h], out)
    o_ref[0] = out


def _build_bias(table, idx_np, head0, n_heads):
    r, c = idx_np.shape
    return pl.pallas_call(
        functools.partial(_bias_kernel, head0=head0),
        grid=(n_heads,),
        in_specs=[pl.BlockSpec(memory_space=pltpu.SMEM),
                  pl.BlockSpec((r, c), lambda h: (0, 0))],
        out_specs=pl.BlockSpec((1, r, c), lambda h: (h, 0, 0)),
        out_shape=jax.ShapeDtypeStruct((n_heads, r, c), F32),
        compiler_params=_cparams(1),
        name="bias_build",
    )(table.astype(F32), jnp.asarray(idx_np))


def _ffn_kernel(x_ref, g_ref, w1_ref, w3_ref, w2_ref, fg_ref, o_ref, *, final):
    x = x_ref[...]
    xn = _rms(x, g_ref[...]).astype(BF16)
    h1 = _dot(xn, w1_ref[...])
    h3 = _dot(xn, w3_ref[...])
    act = (h1 * (1.0 / (1.0 + jnp.exp(-h1)))) * h3
    y = _dot(act.astype(BF16), w2_ref[...])
    out = x + 0.5 * y
    if final:
        out = _rms(out, fg_ref[...])
    o_ref[...] = out


def _ffn(h, g, w1, w3, w2, final_g, final):
    t, d = h.shape
    dff = w1.shape[1]
    return pl.pallas_call(
        functools.partial(_ffn_kernel, final=final),
        grid=(t // TM,),
        in_specs=[pl.BlockSpec((TM, d), lambda i: (i, 0)),
                  _const_spec((1, d)),
                  _const_spec((d, dff)), _const_spec((d, dff)), _const_spec((dff, d)),
                  _const_spec((1, d))],
        out_specs=pl.BlockSpec((TM, d), lambda i: (i, 0)),
        out_shape=jax.ShapeDtypeStruct((t, d), F32),
        compiler_params=_cparams(1),
        name="ffn",
    )(h, g.reshape(1, d), w1.astype(BF16), w3.astype(BF16), w2.astype(BF16), final_g.reshape(1, d))


def _fold_store(scr, blk, outs, n_slab):
    o1, o4, o16 = outs
    o1[...] = blk.astype(BF16)
    for c in range(n_slab):
        scr[c] = blk[:, c * LANES:(c + 1) * LANES]
    for o, dil in ((o4, 4), (o16, 16)):
        rows = TM // dil
        for r in range(dil):
            for c in range(n_slab):
                o[0, r, :, c * LANES:(c + 1) * LANES] = scr[c, pl.ds(r, rows, stride=dil), :].astype(BF16)


def _proj_even_kernel(h_ref, g_ref, w_ref, cqg_ref, wqb_ref, wqi_ref,
                      q1, k1, v1, q4, k4, v4, q16, k16, v16, qb, kb, vb, qi, ki, wi, scr,
                      *, aw, bw, rank):
    u = _rms(h_ref[...], g_ref[...]).astype(BF16)
    p = _dot(u, w_ref[...])
    n_slab = aw // LANES
    scale = HEAD_DIM ** -0.5
    _fold_store(scr, p[:, 0:aw] * scale, (q1, q4, q16), n_slab)
    _fold_store(scr, p[:, aw:2 * aw], (k1, k4, k16), n_slab)
    _fold_store(scr, p[:, 2 * aw:3 * aw], (v1, v4, v16), n_slab)
    c0 = 3 * aw
    kb[...] = p[:, c0:c0 + bw].astype(BF16)
    vb[0] = p[:, c0 + bw:c0 + 2 * bw].T.astype(BF16)
    c1 = c0 + 2 * bw
    cq = _rms(p[:, c1:c1 + rank], cqg_ref[...]).astype(BF16)
    qb[...] = (_dot(cq, wqb_ref[...]) * scale).astype(BF16)
    qi[...] = _dot(cq, wqi_ref[...]).astype(BF16)
    c2 = c1 + rank
    nidx = IDX_HEADS * IDX_DIM
    ki[...] = p[:, c2:c2 + nidx].astype(BF16)
    idx_scale = (IDX_DIM ** -0.5) * (IDX_HEADS ** -0.5)
    wi[...] = p[:, c2 + nidx:c2 + nidx + LANES] * idx_scale


def _proj_even(h, g, w_in, cq_g, wq_b, wq_idx, batch, seq):
    t, d = h.shape
    bw = wq_b.shape[1]
    rank = wq_b.shape[0]
    aw = (w_in.shape[1] - rank - 2 * bw - IDX_DIM - IDX_HEADS) // 3
    nidx = IDX_HEADS * IDX_DIM
    offs = np.cumsum([0, aw, aw, aw, rank, bw, bw, IDX_DIM, IDX_HEADS])
    col = lambda i: w_in[:, offs[i]:offs[i + 1]]
    w_kidx = jnp.tile(col(6), (1, IDX_HEADS))
    w_widx = jnp.pad(col(7), ((0, 0), (0, LANES - IDX_HEADS)))
    w_all = jnp.concatenate([col(0), col(1), col(2), col(4), col(5), col(3), w_kidx, w_widx], axis=1).astype(BF16)
    ncol = w_all.shape[1]
    spt = seq // TM
    tok = lambda width: pl.BlockSpec((TM, width), lambda i: (i, 0))
    fold = lambda dil: pl.BlockSpec((1, dil, TM // dil, aw), lambda i: (i // spt, 0, i % spt, 0))
    sh = jax.ShapeDtypeStruct
    nat = sh((t, aw), BF16)
    f4 = sh((batch, 4, seq // 4, aw), BF16)
    f16 = sh((batch, 16, seq // 16, aw), BF16)
    return pl.pallas_call(
        functools.partial(_proj_even_kernel, aw=aw, bw=bw, rank=rank),
        grid=(t // TM,),
        in_specs=[tok(d), _const_spec((1, d)), _const_spec((d, ncol)), _const_spec((1, rank)),
                  _const_spec((rank, bw)), _const_spec((rank, nidx))],
        out_specs=[tok(aw), tok(aw), tok(aw), fold(4), fold(4), fold(4), fold(16), fold(16), fold(16),
                   tok(bw), tok(bw), pl.BlockSpec((1, bw, TM), lambda i: (i // spt, 0, i % spt)),
                   tok(nidx), tok(nidx), tok(LANES)],
        out_shape=[nat, nat, nat, f4, f4, f4, f16, f16, f16,
                   sh((t, bw), BF16), sh((t, bw), BF16), sh((batch, bw, seq), BF16),
                   sh((t, nidx), BF16), sh((t, nidx), BF16), sh((t, LANES), F32)],
        scratch_shapes=[pltpu.VMEM((aw // LANES, TM, LANES), F32)],
        compiler_params=_cparams(1),
        name="proj_even",
    )(h, g.reshape(1, d), w_all, cq_g.reshape(1, rank), wq_b.astype(BF16), wq_idx.astype(BF16))


def _stack_heads(q):
    lane_b = _lane_id_bf16(q.shape)
    z = jnp.zeros_like(q)
    return jnp.concatenate([jnp.where(lane_b < HEAD_DIM, q, z), jnp.where(lane_b >= HEAD_DIM, q, z)], axis=0)


def _band_block(q2, kw, vw, bias2):
    s = _dot_nt(q2, kw) + bias2
    m = jnp.max(s, axis=-1, keepdims=True)
    p = jnp.exp(s - m)
    den = jnp.sum(p, axis=-1, keepdims=True)
    return _dot(p.astype(BF16), vw), m, den


def _attn_a_kernel(q1, k1, v1, q4, k4, v4, q16, k16, v16, bias_ref, o_ref, acc_s, m_s, den_s, *, seq):
    lo = lax.broadcasted_iota(I32, (BLOCK, LANES), 1) < HEAD_DIM
    shape = (BLOCK, LANES)
    refs = ((q1, k1, v1), (q4, k4, v4), (q16, k16, v16))
    for br, (_, dil) in enumerate(A_BRANCHES):
        qr, kr, vr = refs[br]
        for r in range(dil):
            at = (lambda ref, s, n: ref[0, pl.ds(s, n), :]) if dil == 1 else \
                 (lambda ref, s, n, r=r: ref[0, r, pl.ds(s, n), :])
            for n in range(seq // dil // BLOCK):
                q2 = _stack_heads(at(qr, n * BLOCK, BLOCK))
                if n == 0:
                    acc, m, den = _band_block(q2, at(kr, 0, BLOCK), at(vr, 0, BLOCK), bias_ref[br, 0, :, BLOCK:])
                else:
                    k0 = (n - 1) * BLOCK
                    acc, m, den = _band_block(q2, at(kr, k0, 2 * BLOCK), at(vr, k0, 2 * BLOCK), bias_ref[br, 0])
                rows = pl.ds(n * BLOCK, BLOCK) if dil == 1 else pl.ds(n * BLOCK * dil + r, BLOCK, stride=dil)
                acc_s[br, rows, :] = jnp.where(lo, acc[:BLOCK], acc[BLOCK:])
                m_s[br, rows, :] = jnp.where(lo, jnp.broadcast_to(m[:BLOCK], shape), jnp.broadcast_to(m[BLOCK:], shape))
                den_s[br, rows, :] = jnp.where(lo, jnp.broadcast_to(den[:BLOCK], shape),
                                               jnp.broadcast_to(den[BLOCK:], shape))

    chunk = 2 * BLOCK

    def merge(c, carry):
        rows = pl.ds(pl.multiple_of(c * chunk, chunk), chunk)
        ms = [m_s[br, rows, :] for br in range(3)]
        mx = jnp.maximum(jnp.maximum(ms[0], ms[1]), ms[2])
        num = jnp.zeros((chunk, LANES), F32)
        den = jnp.zeros((chunk, LANES), F32)
        for br in range(3):
            e = jnp.exp(ms[br] - mx)
            num = num + e * acc_s[br, rows, :]
            den = den + e * den_s[br, rows, :]
        o_ref[0, rows, :] = (num / den).astype(BF16)
        return carry

    lax.fori_loop(0, seq // chunk, merge, 0)


def _attn_a(qkv, bias, batch, seq):
    q1, k1, v1, q4, k4, v4, q16, k16, v16 = qkv
    aw = q1.shape[-1]
    npair = aw // LANES
    nat = pl.BlockSpec((1, seq, LANES), lambda b, p: (b, 0, p))
    fold = lambda dil: pl.BlockSpec((1, dil, seq // dil, LANES), lambda b, p: (b, 0, 0, p))
    r3 = lambda a: a.reshape(batch, seq, aw)
    return pl.pallas_call(
        functools.partial(_attn_a_kernel, seq=seq),
        grid=(batch, npair),
        in_specs=[nat, nat, nat, fold(4), fold(4), fold(4), fold(16), fold(16), fold(16),
                  pl.BlockSpec((3, 1, 2 * BLOCK, 2 * BLOCK), lambda b, p: (0, p, 0, 0))],
        out_specs=nat,
        out_shape=jax.ShapeDtypeStruct((batch, seq, aw), BF16),
        scratch_shapes=[pltpu.VMEM((3, seq, LANES), F32)] * 3,
        compiler_params=_cparams(2),
        name="attn_dilated",
    )(r3(q1), r3(k1), r3(v1), q4, k4, v4, q16, k16, v16, bias)


def _attn_c_kernel(sink_ref, q_ref, k_ref, v_ref, bias_ref, o_ref, *, seq, group):
    g = pl.program_id(1)
    npair = group // 2
    lo = lax.broadcasted_iota(I32, (BLOCK, LANES), 1) < HEAD_DIM

    def block(q0, k0, width, bias2):
        qn = q_ref[0, pl.ds(q0, BLOCK), :]
        q2 = jnp.concatenate([_stack_heads(qn[:, p * LANES:(p + 1) * LANES]) for p in range(npair)], axis=0)
        acc, m, den = _band_block(q2, k_ref[0, pl.ds(k0, width), :], v_ref[0, pl.ds(k0, width), :], bias2)
        for p in range(npair):
            outs = []
            for hh in range(2):
                h = 2 * p + hh
                r = slice(h * BLOCK, (h + 1) * BLOCK)
                outs.append(acc[r] / (den[r] + jnp.exp(sink_ref[g * group + h] - m[r])))
            o_ref[0, pl.ds(q0, BLOCK), p * LANES:(p + 1) * LANES] = jnp.where(lo, outs[0], outs[1]).astype(BF16)

    block(0, 0, BLOCK, bias_ref[0, :, BLOCK:])

    def body(n, carry):
        block(pl.multiple_of(n * BLOCK, BLOCK), pl.multiple_of((n - 1) * BLOCK, BLOCK), 2 * BLOCK, bias_ref[0])
        return carry

    lax.fori_loop(1, seq // BLOCK, body, 0)


def _attn_c(q, k2, v2, bias, sinks, batch, seq, group):
    qw = q.shape[-1]
    n_kv = k2.shape[-1] // LANES
    gw = qw // n_kv
    return pl.pallas_call(
        functools.partial(_attn_c_kernel, seq=seq, group=group),
        grid=(batch, n_kv),
        in_specs=[pl.BlockSpec(memory_space=pltpu.SMEM),
                  pl.BlockSpec((1, seq, gw), lambda b, g: (b, 0, g)),
                  pl.BlockSpec((1, seq, LANES), lambda b, g: (b, 0, g)),
                  pl.BlockSpec((1, seq, LANES), lambda b, g: (b, 0, g)),
                  pl.BlockSpec((1, group * BLOCK, 2 * BLOCK), lambda b, g: (g, 0, 0))],
        out_specs=pl.BlockSpec((1, seq, gw), lambda b, g: (b, 0, g)),
        out_shape=jax.ShapeDtypeStruct((batch, seq, qw), BF16),
        compiler_params=_cparams(2),
        name="attn_swa",
    )(sinks, q, k2, v2, bias)


def _dsa_body(nchs, i, ki_ref, kb_ref, vt_ref, bias_ref, o_ref, qs_ref, wt_ref, q2_ref, sk_ref, mask_ref,
              *, topk, npair):
    tpc = CW // BLOCK
    last = nchs - 1
    key_pos = lax.broadcasted_iota(I32, (CW, BLOCK), 0) + last * CW
    qry_pos = lax.broadcasted_iota(I32, (CW, BLOCK), 1) + i * BLOCK
    causal_last = key_pos <= qry_pos

    for c in range(nchs):
        st = _dot_nt(ki_ref[0, c * CW:(c + 1) * CW, :], qs_ref[...])
        score = None
        for h in range(IDX_HEADS):
            term = jnp.maximum(st[:, h * BLOCK:(h + 1) * BLOCK], 0.0) * wt_ref[h:h + 1, :]
            score = term if score is None else score + term
        score = jnp.where(score == 0.0, 0.0, score)
        if c == last:
            score = jnp.where(causal_last, score, -jnp.inf)
        bits = lax.bitcast_convert_type(score, I32)
        sk_ref[c] = bits ^ (lax.shift_right_arithmetic(bits, 31) & 0x7FFFFFFF)

    grp = 8 * 8

    def count(pred):
        acc = jnp.zeros((grp, BLOCK), F32)
        for c in range(nchs):
            for g in range(CW // grp):
                acc = jnp.where(pred(sk_ref[c, g * grp:(g + 1) * grp, :]), acc + 1.0, acc)
        return jnp.sum(acc, axis=0, keepdims=True)

    kf = float(topk)

    def search(it, ans):
        cand = ans | lax.shift_left(jnp.int32(1), 31 - it)
        cand_s = cand ^ INT_MIN
        return jnp.where(count(lambda x: x >= cand_s) >= kf, cand, ans)

    ans = lax.fori_loop(0, 32, search, jnp.zeros((1, BLOCK), I32))
    thr = ans ^ INT_MIN
    n_gt = count(lambda x: x > thr)
    n_eq = count(lambda x: x == thr)
    need = kf - n_gt

    for c in range(nchs):
        sel = sk_ref[c] >= thr
        if c == last:
            sel = sel & causal_last
        mask_ref[c] = jnp.where(sel, 0.0, NEG)

    excess = jnp.where((n_eq > need) & (thr > KEY_NEG_INF), 1.0, 0.0)

    @pl.when(jnp.max(excess) > 0.0)
    def _():
        r_i = lax.broadcasted_iota(I32, (BLOCK, BLOCK), 0)
        c_i = lax.broadcasted_iota(I32, (BLOCK, BLOCK), 1)
        lower = jnp.where(c_i <= r_i, 1.0, 0.0).astype(BF16)
        carry = jnp.zeros((1, BLOCK), F32)
        for c in range(nchs):
            for t in range(tpc):
                rows = slice(t * BLOCK, (t + 1) * BLOCK)
                xs = sk_ref[c, rows, :]
                eq = xs == thr
                eqf = jnp.where(eq, 1.0, 0.0)
                rank = _dot(lower, eqf.astype(BF16)) + carry
                keep = (xs > thr) | (eq & (rank <= need))
                causal = r_i + (c * CW + t * BLOCK) <= c_i + i * BLOCK
                mask_ref[c, rows, :] = jnp.where(keep & causal, 0.0, NEG)
                carry = carry + jnp.sum(eqf, axis=0, keepdims=True)

    first_half = lax.broadcasted_iota(I32, (LANES, BLOCK), 0) < HEAD_DIM
    for p in range(npair):
        lanes = slice(p * LANES, (p + 1) * LANES)
        logits = []
        for c in range(nchs):
            s = _dot_nt(kb_ref[0, c * CW:(c + 1) * CW, lanes], q2_ref[p])
            msk = mask_ref[c]
            per_head = []
            for hh in range(2):
                bias = jnp.concatenate(
                    [bias_ref[2 * p + hh, jnp.maximum(i - (c * tpc + t), 0)] for t in range(tpc)], axis=0)
                per_head.append(s[:, hh * BLOCK:(hh + 1) * BLOCK] + bias + msk)
            logits.append(per_head)
        out_t = jnp.zeros((LANES, 2 * BLOCK), F32)
        dens = []
        pes = [[None, None] for _ in range(nchs)]
        for hh in range(2):
            m = None
            for c in range(nchs):
                mc = jnp.max(logits[c][hh], axis=0, keepdims=True)
                m = mc if m is None else jnp.maximum(m, mc)
            den = jnp.zeros((1, BLOCK), F32)
            for c in range(nchs):
                pe = jnp.exp(logits[c][hh] - m)
                den = den + jnp.sum(pe, axis=0, keepdims=True)
                pes[c][hh] = pe.astype(BF16)
            dens.append(den)
        for c in range(nchs):
            out_t = out_t + _dot(vt_ref[0, lanes, c * CW:(c + 1) * CW], jnp.concatenate(pes[c], axis=1))
        o_t = jnp.where(first_half, out_t[:, :BLOCK] / dens[0], out_t[:, BLOCK:] / dens[1])
        o_ref[:, lanes] = o_t.T.astype(BF16)


def _dsa_kernel(qi_ref, wi_ref, ki_ref, qb_ref, kb_ref, vt_ref, bias_ref, o_ref,
                qs_ref, wt_ref, q2_ref, sk_ref, mask_ref, *, topk, n_heads, n_chunks):
    i = pl.program_id(1)
    npair = n_heads // 2

    qi = qi_ref[...]
    lane_i = _lane_id_bf16(qi.shape)
    for h in range(IDX_HEADS):
        in_head = (lane_i >= h * IDX_DIM) & (lane_i < (h + 1) * IDX_DIM)
        qs_ref[h * BLOCK:(h + 1) * BLOCK, :] = jnp.where(in_head, qi, jnp.zeros_like(qi))
    wt_ref[...] = wi_ref[...].T
    for p in range(npair):
        q2_ref[p] = _stack_heads(qb_ref[:, p * LANES:(p + 1) * LANES])

    for nchs in range(1, n_chunks + 1):
        @pl.when(i // (CW // BLOCK) + 1 == nchs)
        def _(nchs=nchs):
            _dsa_body(nchs, i, ki_ref, kb_ref, vt_ref, bias_ref, o_ref, qs_ref, wt_ref, q2_ref, sk_ref,
                      mask_ref, topk=topk, npair=npair)


def _dsa(qi, wi, ki, qb, kb, vt, bias, batch, seq, topk):
    bw = qb.shape[-1]
    nidx = qi.shape[-1]
    nb = seq // BLOCK
    n_heads = bw // HEAD_DIM
    npair = n_heads // 2
    qrow = lambda width: pl.BlockSpec((BLOCK, width), lambda b, i: (b * nb + i, 0))
    full = lambda width: pl.BlockSpec((1, seq, width), lambda b, i: (b, 0, 0))
    vm = pltpu.VMEM
    return pl.pallas_call(
        functools.partial(_dsa_kernel, topk=topk, n_heads=n_heads, n_chunks=seq // CW),
        grid=(batch, nb),
        in_specs=[qrow(nidx), qrow(LANES), full(nidx), qrow(bw), full(bw),
                  pl.BlockSpec((1, bw, seq), lambda b, i: (b, 0, 0)),
                  _const_spec((n_heads, nb, BLOCK, BLOCK))],
        out_specs=qrow(bw),
        out_shape=jax.ShapeDtypeStruct((batch * seq, bw), BF16),
        scratch_shapes=[vm((IDX_HEADS * BLOCK, nidx), BF16), vm((LANES, BLOCK), F32),
                        vm((npair, 2 * BLOCK, LANES), BF16),
                        vm((seq // CW, CW, BLOCK), I32), vm((seq // CW, CW, BLOCK), F32)],
        compiler_params=_cparams(2),
        name="dsa",
    )(qi, wi, ki.reshape(batch, seq, nidx), qb, kb.reshape(batch, seq, bw), vt, bias)


def _outproj_kernel(*refs, n_in):
    h_ref, o_ref = refs[0], refs[-1]
    acc = h_ref[...]
    for a_ref, w_ref in zip(refs[1:1 + n_in], refs[1 + n_in:1 + 2 * n_in]):
        acc = acc + _dot(a_ref[...], w_ref[...])
    o_ref[...] = acc


def _outproj(h, parts, w_out):
    t, d = h.shape
    ws, r0 = [], 0
    for a in parts:
        ws.append(w_out[r0:r0 + a.shape[1]].astype(BF16))
        r0 += a.shape[1]
    tok = lambda width: pl.BlockSpec((TM, width), lambda i: (i, 0))
    return pl.pallas_call(
        functools.partial(_outproj_kernel, n_in=len(parts)),
        grid=(t // TM,),
        in_specs=[tok(d)] + [tok(a.shape[1]) for a in parts] + [_const_spec(w.shape) for w in ws],
        out_specs=tok(d),
        out_shape=jax.ShapeDtypeStruct((t, d), F32),
        compiler_params=_cparams(1),
        name="outproj",
    )(h, *parts, *ws)


def _proj_odd_kernel(h_ref, g_ref, w_ref, q_ref, k_ref, v_ref, *, qw, kw):
    u = _rms(h_ref[...], g_ref[...]).astype(BF16)
    p = _dot(u, w_ref[...])
    q_ref[...] = (p[:, 0:qw] * (HEAD_DIM ** -0.5)).astype(BF16)
    k_ref[...] = p[:, qw:qw + kw].astype(BF16)
    v_ref[...] = p[:, qw + kw:qw + 2 * kw].astype(BF16)


def _proj_odd(h, g, w_in, qw, kvw):
    t, d = h.shape
    dup = lambda w: jnp.repeat(w.reshape(d, kvw // HEAD_DIM, 1, HEAD_DIM), 2, axis=2).reshape(d, 2 * kvw)
    w_all = jnp.concatenate([w_in[:, :qw], dup(w_in[:, qw:qw + kvw]), dup(w_in[:, qw + kvw:])], axis=1).astype(BF16)
    tok = lambda width: pl.BlockSpec((TM, width), lambda i: (i, 0))
    sh = jax.ShapeDtypeStruct
    return pl.pallas_call(
        functools.partial(_proj_odd_kernel, qw=qw, kw=2 * kvw),
        grid=(t // TM,),
        in_specs=[tok(d), _const_spec((1, d)), _const_spec(w_all.shape)],
        out_specs=[tok(qw), tok(2 * kvw), tok(2 * kvw)],
        out_shape=[sh((t, qw), BF16), sh((t, 2 * kvw), BF16), sh((t, 2 * kvw), BF16)],
        compiler_params=_cparams(1),
        name="proj_odd",
    )(h, g.reshape(1, d), w_all)


def kernel(x, norm_g, final_g, rel_bias_table, ffn_w1, ffn_w3, ffn_w2, hyb_w_in, hyb_cq_g, hyb_wq_b,
           hyb_wq_idx, hyb_w_out, swa_w_in, swa_sinks, swa_w_out):
    batch, seq, d = x.shape
    depth = norm_g.shape[0]
    n_heads = d // HEAD_DIM
    a_heads = n_heads // 2
    topk = min(TOPK_MAX, seq // 4)
    assert seq % (BLOCK * 16) == 0 and seq % CW == 0 and (batch * seq) % TM == 0 and seq % TM == 0

    bias_a = _build_bias(
        rel_bias_table,
        np.concatenate([_band_bucket_idx(dil, win // dil) for win, dil in A_BRANCHES], axis=0),
        0, a_heads).reshape(a_heads // 2, 2, 3, BLOCK, 2 * BLOCK).transpose(2, 0, 1, 3, 4)
    bias_a = bias_a.reshape(3, a_heads // 2, 2 * BLOCK, 2 * BLOCK)
    bias_b = _build_bias(rel_bias_table, _dense_bucket_idx(seq), a_heads, n_heads - a_heads)
    bias_b = bias_b.reshape(n_heads - a_heads, seq // BLOCK, BLOCK, BLOCK)
    bias_c = _build_bias(rel_bias_table, _band_bucket_idx(1, C_WINDOW - 1), 0, n_heads)

    h = x.reshape(batch * seq, d)
    for layer in range(depth):
        g = norm_g[layer]
        i = layer // 2
        h = _ffn(h, g[0], ffn_w1[layer, 0], ffn_w3[layer, 0], ffn_w2[layer, 0], final_g, False)
        if layer % 2 == 0:
            outs = _proj_even(h, g[1], hyb_w_in[i], hyb_cq_g[i], hyb_wq_b[i], hyb_wq_idx[i], batch, seq)
            o_a = _attn_a(outs[:9], bias_a, batch, seq)
            qb, kb, vb, qi, ki, wi = outs[9:]
            o_b = _dsa(qi, wi, ki, qb, kb, vb, bias_b, batch, seq, topk)
            h = _outproj(h, [o_a.reshape(batch * seq, -1), o_b], hyb_w_out[i])
        else:
            kvw = (swa_w_in.shape[-1] - d) // 2
            n_kv = kvw // HEAD_DIM
            group = n_heads // n_kv
            q, k2, v2 = _proj_odd(h, g[1], swa_w_in[i], d, kvw)
            r3 = lambda a: a.reshape(batch, seq, a.shape[-1])
            o_c = _attn_c(r3(q), r3(k2), r3(v2), bias_c.reshape(n_kv, group * BLOCK, 2 * BLOCK),
                          swa_sinks[i].astype(F32), batch, seq, group)
            h = _outproj(h, [o_c.reshape(batch * seq, -1)], swa_w_out[i])
        last = layer == depth - 1
        h = _ffn(h, g[2], ffn_w1[layer, 1], ffn_w3[layer, 1], ffn_w2[layer, 1], final_g, last)
    return h.reshape(batch, seq, d)
```

```python
import functools
import math

import numpy as np
import jax
import jax.numpy as jnp
from jax import lax
from jax.experimental import pallas as pl
from jax.experimental.pallas import tpu as pltpu

F32 = jnp.float32
BF16 = jnp.bfloat16
I32 = jnp.int32

EPS = 1e-6
HEAD_DIM = 64
BLOCK = 128
LANES = 128
NUM_BUCKETS = 32
MAX_DISTANCE = 2048
A_BRANCHES = ((128, 1), (512, 4), (2048, 16))
C_WINDOW = 128
TOPK_MAX = 256
IDX_HEADS = 8
IDX_DIM = 32
NEG = -1e30
LOG2E = math.log2(math.e)
Q_SCALE = HEAD_DIM ** -0.5 * LOG2E
INT_MIN = -(2 ** 31)
KEY_NEG_INF = (0xFF800000 - (1 << 32)) ^ 0x7FFFFFFF
VMEM_LIMIT = 56 * 1024 * 1024

TM = 512
CW = 4 * BLOCK
C_UNROLL = 3


def _cparams(n_axes):
    return pltpu.CompilerParams(
        dimension_semantics=("arbitrary",) * n_axes, vmem_limit_bytes=VMEM_LIMIT)


def _rms(x, g):
    return x * lax.rsqrt(jnp.mean(x * x, axis=-1, keepdims=True) + EPS) * g


def _dot(a, b):
    return jnp.dot(a, b, preferred_element_type=F32)


def _dot_nt(a, b):
    return lax.dot_general(a, b, (((1,), (1,)), ((), ())), preferred_element_type=F32)


def _lane_id_bf16(shape):
    return lax.broadcasted_iota(I32, shape, 1).astype(F32).astype(BF16)


def _rows(start, size):
    if isinstance(start, int):
        return pl.ds(start, size)
    return pl.ds(pl.multiple_of(start, size), size)


def _const_spec(shape):
    nd = len(shape)
    return pl.BlockSpec(shape, lambda *_: (0,) * nd, pipeline_mode=pl.Buffered(1))


def _t5_bucket_np(dist):
    dist = np.maximum(dist, 0)
    max_exact = NUM_BUCKETS // 2
    d_f = np.maximum(dist, 1).astype(np.float32)
    large = max_exact + (np.log(d_f / max_exact) / math.log(MAX_DISTANCE / max_exact)
                         * (NUM_BUCKETS - max_exact)).astype(np.int32)
    large = np.minimum(large, NUM_BUCKETS - 1)
    return np.where(dist < max_exact, dist, large).astype(np.int32)


def _band_bucket_idx(dil, max_dist):
    loc = np.arange(BLOCK)[:, None] + BLOCK - np.arange(2 * BLOCK)[None, :]
    valid = (loc >= 0) & (loc <= max_dist)
    return np.where(valid, _t5_bucket_np(dil * loc), -1).astype(np.int32)


def _dense_bucket_idx(seq):
    nb = seq // BLOCK
    d = np.arange(nb)[:, None, None] * BLOCK - np.arange(BLOCK)[None, :, None] + np.arange(BLOCK)[None, None, :]
    return np.where(d >= 0, _t5_bucket_np(d), -1).astype(np.int32).reshape(seq, BLOCK)


def _bias_kernel(tab_ref, idx_ref, o_ref, *, head0):
    h = pl.program_id(0) + head0
    idx = idx_ref[...]
    out = jnp.full(idx.shape, NEG, F32)
    for b in range(NUM_BUCKETS):
        out = jnp.where(idx == b, tab_ref[b, h] * LOG2E, out)
    o_ref[0] = out


def _build_bias(table, idx_np, head0, n_heads):
    r, c = idx_np.shape
    return pl.pallas_call(
        functools.partial(_bias_kernel, head0=head0),
        grid=(n_heads,),
        in_specs=[pl.BlockSpec(memory_space=pltpu.SMEM),
                  pl.BlockSpec((r, c), lambda h: (0, 0))],
        out_specs=pl.BlockSpec((1, r, c), lambda h: (h, 0, 0)),
        out_shape=jax.ShapeDtypeStruct((n_heads, r, c), F32),
        compiler_params=_cparams(1),
        name="bias_build",
    )(table.astype(F32), jnp.asarray(idx_np))


def _ffn_kernel(*refs, n_mix, final, post_splits):
    it = iter(refs)
    x_ref = next(it)
    mix = [next(it) for _ in range(n_mix)]
    mix_w = [next(it) for _ in range(n_mix)]
    g_ref, w1_ref, w3_ref, w2_ref, g2_ref = (next(it) for _ in range(5))
    wp_ref = next(it) if post_splits else None
    o_ref = next(it)
    x = x_ref[...]
    for a_ref, w_ref in zip(mix, mix_w):
        x = x + _dot(a_ref[...], w_ref[...])
    xn = _rms(x, g_ref[...]).astype(BF16)
    h1 = _dot(xn, w1_ref[...])
    h3 = _dot(xn, w3_ref[...])
    act = (h1 * (1.0 / (1.0 + jnp.exp(-h1)))) * h3
    y = _dot(act.astype(BF16), w2_ref[...])
    out = x + 0.5 * y
    if final:
        out = _rms(out, g2_ref[...])
    o_ref[...] = out
    if post_splits:
        p = _dot(_rms(out, g2_ref[...]).astype(BF16), wp_ref[...])
        c0 = 0
        for k, (width, scale) in enumerate(post_splits):
            blk = p[:, c0:c0 + width]
            next(it)[...] = (blk if scale == 1.0 else blk * scale).astype(BF16)
            c0 += width


def _ffn(h, g, w1, w3, w2, g2, *, mix=(), mix_w=(), final=False, post_w=None, post_splits=()):
    t, d = h.shape
    dff = w1.shape[1]
    tok = lambda width: pl.BlockSpec((TM, width), lambda i: (i, 0))
    sh = jax.ShapeDtypeStruct
    in_specs = [tok(d)] + [tok(a.shape[1]) for a in mix] + [_const_spec(w.shape) for w in mix_w]
    in_specs += [_const_spec((1, d)), _const_spec((d, dff)), _const_spec((d, dff)), _const_spec((dff, d)),
                 _const_spec((1, d))]
    args = [h, *mix, *mix_w, g.reshape(1, d), w1.astype(BF16), w3.astype(BF16), w2.astype(BF16), g2.reshape(1, d)]
    out_specs, out_shape = [tok(d)], [sh((t, d), F32)]
    if post_splits:
        in_specs.append(_const_spec(post_w.shape))
        args.append(post_w)
        out_specs += [tok(width) for width, _ in post_splits]
        out_shape += [sh((t, width), BF16) for width, _ in post_splits]
    res = pl.pallas_call(
        functools.partial(_ffn_kernel, n_mix=len(mix), final=final, post_splits=tuple(post_splits)),
        grid=(t // TM,),
        in_specs=in_specs,
        out_specs=out_specs,
        out_shape=out_shape,
        compiler_params=_cparams(1),
        name="ffn",
    )(*args)
    return res if post_splits else res[0]


def _fold_store(scr, blk, outs, n_slab):
    o1, o4, o16 = outs
    o1[...] = blk.astype(BF16)
    for c in range(n_slab):
        scr[c] = blk[:, c * LANES:(c + 1) * LANES]
    for o, dil in ((o4, 4), (o16, 16)):
        rows = TM // dil
        for r in range(dil):
            for c in range(n_slab):
                o[0, r, :, c * LANES:(c + 1) * LANES] = scr[c, pl.ds(r, rows, stride=dil), :].astype(BF16)


def _proj_even_kernel(h_ref, g_ref, w_ref, cqg_ref, wqb_ref, wqi_ref,
                      q1, k1, v1, q4, k4, v4, q16, k16, v16, qb, kb, vb, qi, ki, wi, scr,
                      *, aw, bw, rank):
    u = _rms(h_ref[...], g_ref[...]).astype(BF16)
    p = _dot(u, w_ref[...])
    n_slab = aw // LANES
    scale = Q_SCALE
    _fold_store(scr, p[:, 0:aw] * scale, (q1, q4, q16), n_slab)
    _fold_store(scr, p[:, aw:2 * aw], (k1, k4, k16), n_slab)
    _fold_store(scr, p[:, 2 * aw:3 * aw], (v1, v4, v16), n_slab)
    c0 = 3 * aw
    kb[...] = p[:, c0:c0 + bw].astype(BF16)
    vb[0] = p[:, c0 + bw:c0 + 2 * bw].T.astype(BF16)
    c1 = c0 + 2 * bw
    cq = _rms(p[:, c1:c1 + rank], cqg_ref[...]).astype(BF16)
    qb[...] = (_dot(cq, wqb_ref[...]) * scale).astype(BF16)
    qi[...] = _dot(cq, wqi_ref[...]).astype(BF16)
    c2 = c1 + rank
    nidx = IDX_HEADS * IDX_DIM
    ki[...] = p[:, c2:c2 + nidx].astype(BF16)
    idx_scale = (IDX_DIM ** -0.5) * (IDX_HEADS ** -0.5)
    wi[...] = p[:, c2 + nidx:c2 + nidx + LANES] * idx_scale


def _proj_even(h, g, w_in, cq_g, wq_b, wq_idx, batch, seq):
    t, d = h.shape
    bw = wq_b.shape[1]
    rank = wq_b.shape[0]
    aw = (w_in.shape[1] - rank - 2 * bw - IDX_DIM - IDX_HEADS) // 3
    nidx = IDX_HEADS * IDX_DIM
    offs = np.cumsum([0, aw, aw, aw, rank, bw, bw, IDX_DIM, IDX_HEADS])
    col = lambda i: w_in[:, offs[i]:offs[i + 1]]
    w_kidx = jnp.tile(col(6), (1, IDX_HEADS))
    w_widx = jnp.pad(col(7), ((0, 0), (0, LANES - IDX_HEADS)))
    w_all = jnp.concatenate([col(0), col(1), col(2), col(4), col(5), col(3), w_kidx, w_widx], axis=1).astype(BF16)
    ncol = w_all.shape[1]
    spt = seq // TM
    tok = lambda width: pl.BlockSpec((TM, width), lambda i: (i, 0))
    fold = lambda dil: pl.BlockSpec((1, dil, TM // dil, aw), lambda i: (i // spt, 0, i % spt, 0))
    sh = jax.ShapeDtypeStruct
    nat = sh((t, aw), BF16)
    f4 = sh((batch, 4, seq // 4, aw), BF16)
    f16 = sh((batch, 16, seq // 16, aw), BF16)
    return pl.pallas_call(
        functools.partial(_proj_even_kernel, aw=aw, bw=bw, rank=rank),
        grid=(t // TM,),
        in_specs=[tok(d), _const_spec((1, d)), _const_spec((d, ncol)), _const_spec((1, rank)),
                  _const_spec((rank, bw)), _const_spec((rank, nidx))],
        out_specs=[tok(aw), tok(aw), tok(aw), fold(4), fold(4), fold(4), fold(16), fold(16), fold(16),
                   tok(bw), tok(bw), pl.BlockSpec((1, bw, TM), lambda i: (i // spt, 0, i % spt)),
                   tok(nidx), tok(nidx), tok(LANES)],
        out_shape=[nat, nat, nat, f4, f4, f4, f16, f16, f16,
                   sh((t, bw), BF16), sh((t, bw), BF16), sh((batch, bw, seq), BF16),
                   sh((t, nidx), BF16), sh((t, nidx), BF16), sh((t, LANES), F32)],
        scratch_shapes=[pltpu.VMEM((aw // LANES, TM, LANES), F32)],
        compiler_params=_cparams(1),
        name="proj_even",
    )(h, g.reshape(1, d), w_all, cq_g.reshape(1, rank), wq_b.astype(BF16), wq_idx.astype(BF16))


def _stack_heads(q):
    lane_b = _lane_id_bf16(q.shape)
    z = jnp.zeros_like(q)
    return jnp.concatenate([jnp.where(lane_b < HEAD_DIM, q, z), jnp.where(lane_b >= HEAD_DIM, q, z)], axis=0)


def _band_block(q2, kw, vw, bias2):
    s = _dot_nt(q2, kw) + bias2
    m = jnp.max(s, axis=-1, keepdims=True)
    p = jnp.exp2(s - m)
    den = jnp.sum(p, axis=-1, keepdims=True)
    return _dot(p.astype(BF16), vw), m, den


def _attn_a_kernel(q1, k1, v1, q4, k4, v4, q16, k16, v16, bias_ref, o_ref, acc_s, m_s, den_s, *, seq):
    lo = lax.broadcasted_iota(I32, (BLOCK, LANES), 1) < HEAD_DIM
    shape = (BLOCK, LANES)
    refs = ((q1, k1, v1), (q4, k4, v4), (q16, k16, v16))
    for br, (_, dil) in enumerate(A_BRANCHES):
        qr, kr, vr = refs[br]
        for r in range(dil):
            at = (lambda ref, s, n: ref[0, pl.ds(s, n), :]) if dil == 1 else \
                 (lambda ref, s, n, r=r: ref[0, r, pl.ds(s, n), :])
            for n in range(seq // dil // BLOCK):
                q2 = _stack_heads(at(qr, n * BLOCK, BLOCK))
                if n == 0:
                    acc, m, den = _band_block(q2, at(kr, 0, BLOCK), at(vr, 0, BLOCK), bias_ref[br, 0, :, BLOCK:])
                else:
                    k0 = (n - 1) * BLOCK
                    acc, m, den = _band_block(q2, at(kr, k0, 2 * BLOCK), at(vr, k0, 2 * BLOCK), bias_ref[br, 0])
                rows = pl.ds(n * BLOCK, BLOCK) if dil == 1 else pl.ds(n * BLOCK * dil + r, BLOCK, stride=dil)
                acc_s[br, rows, :] = jnp.where(lo, acc[:BLOCK], acc[BLOCK:])
                m_s[br, rows, :] = jnp.where(lo, jnp.broadcast_to(m[:BLOCK], shape), jnp.broadcast_to(m[BLOCK:], shape))
                den_s[br, rows, :] = jnp.where(lo, jnp.broadcast_to(den[:BLOCK], shape),
                                               jnp.broadcast_to(den[BLOCK:], shape))

    chunk = 2 * BLOCK

    def merge(c, carry):
        rows = pl.ds(pl.multiple_of(c * chunk, chunk), chunk)
        ms = [m_s[br, rows, :] for br in range(3)]
        mx = jnp.maximum(jnp.maximum(ms[0], ms[1]), ms[2])
        num = jnp.zeros((chunk, LANES), F32)
        den = jnp.zeros((chunk, LANES), F32)
        for br in range(3):
            e = jnp.exp2(ms[br] - mx)
            num = num + e * acc_s[br, rows, :]
            den = den + e * den_s[br, rows, :]
        o_ref[0, rows, :] = (num / den).astype(BF16)
        return carry

    lax.fori_loop(0, seq // chunk, merge, 0)


def _attn_a(qkv, bias, batch, seq):
    q1, k1, v1, q4, k4, v4, q16, k16, v16 = qkv
    aw = q1.shape[-1]
    npair = aw // LANES
    nat = pl.BlockSpec((1, seq, LANES), lambda b, p: (b, 0, p))
    fold = lambda dil: pl.BlockSpec((1, dil, seq // dil, LANES), lambda b, p: (b, 0, 0, p))
    r3 = lambda a: a.reshape(batch, seq, aw)
    return pl.pallas_call(
        functools.partial(_attn_a_kernel, seq=seq),
        grid=(batch, npair),
        in_specs=[nat, nat, nat, fold(4), fold(4), fold(4), fold(16), fold(16), fold(16),
                  pl.BlockSpec((3, 1, 2 * BLOCK, 2 * BLOCK), lambda b, p: (0, p, 0, 0))],
        out_specs=nat,
        out_shape=jax.ShapeDtypeStruct((batch, seq, aw), BF16),
        scratch_shapes=[pltpu.VMEM((3, seq, LANES), F32)] * 3,
        compiler_params=_cparams(2),
        name="attn_dilated",
    )(r3(q1), r3(k1), r3(v1), q4, k4, v4, q16, k16, v16, bias)


def _attn_c_kernel(sink_ref, q_ref, k_ref, v_ref, bias_ref, o_ref, *, seq, group):
    g = pl.program_id(1)
    npair = group // 2
    lo = lax.broadcasted_iota(I32, (BLOCK, LANES), 1) < HEAD_DIM

    def block(q0, k0, width, bias2):
        qn = q_ref[0, pl.ds(q0, BLOCK), :]
        q2 = jnp.concatenate([_stack_heads(qn[:, p * LANES:(p + 1) * LANES]) for p in range(npair)], axis=0)
        acc, m, den = _band_block(q2, k_ref[0, pl.ds(k0, width), :], v_ref[0, pl.ds(k0, width), :], bias2)
        for p in range(npair):
            outs = []
            for hh in range(2):
                h = 2 * p + hh
                r = slice(h * BLOCK, (h + 1) * BLOCK)
                outs.append(acc[r] / (den[r] + jnp.exp2(sink_ref[g * group + h] * LOG2E - m[r])))
            o_ref[0, pl.ds(q0, BLOCK), p * LANES:(p + 1) * LANES] = jnp.where(lo, outs[0], outs[1]).astype(BF16)

    block(0, 0, BLOCK, bias_ref[0, :, BLOCK:])

    def body(n, carry):
        block(pl.multiple_of(n * BLOCK, BLOCK), pl.multiple_of((n - 1) * BLOCK, BLOCK), 2 * BLOCK, bias_ref[0])
        return carry

    lax.fori_loop(1, seq // BLOCK, body, 0, unroll=C_UNROLL)


def _attn_c(q, k2, v2, bias, sinks, batch, seq, group):
    qw = q.shape[-1]
    n_kv = k2.shape[-1] // LANES
    gw = qw // n_kv
    return pl.pallas_call(
        functools.partial(_attn_c_kernel, seq=seq, group=group),
        grid=(batch, n_kv),
        in_specs=[pl.BlockSpec(memory_space=pltpu.SMEM),
                  pl.BlockSpec((1, seq, gw), lambda b, g: (b, 0, g)),
                  pl.BlockSpec((1, seq, LANES), lambda b, g: (b, 0, g)),
                  pl.BlockSpec((1, seq, LANES), lambda b, g: (b, 0, g)),
                  pl.BlockSpec((1, group * BLOCK, 2 * BLOCK), lambda b, g: (g, 0, 0))],
        out_specs=pl.BlockSpec((1, seq, gw), lambda b, g: (b, 0, g)),
        out_shape=jax.ShapeDtypeStruct((batch, seq, qw), BF16),
        compiler_params=_cparams(2),
        name="attn_swa",
    )(sinks, q, k2, v2, bias)


def _dsa_body(nchs, i, ki_ref, kb_ref, vt_ref, bias_ref, o_ref, qs_ref, wt_ref, q2_ref, sk_ref, mask_ref,
              *, topk, npair):
    tpc = CW // BLOCK
    last = nchs - 1
    key_pos = lax.broadcasted_iota(I32, (CW, BLOCK), 0) + last * CW
    qry_pos = lax.broadcasted_iota(I32, (CW, BLOCK), 1) + i * BLOCK
    causal_last = key_pos <= qry_pos

    for c in range(nchs):
        st = _dot_nt(ki_ref[0, c * CW:(c + 1) * CW, :], qs_ref[...])
        score = None
        for h in range(IDX_HEADS):
            term = jnp.maximum(st[:, h * BLOCK:(h + 1) * BLOCK], 0.0) * wt_ref[h:h + 1, :]
            score = term if score is None else score + term
        score = jnp.where(score == 0.0, 0.0, score)
        if c == last:
            score = jnp.where(causal_last, score, -jnp.inf)
        bits = lax.bitcast_convert_type(score, I32)
        sk_ref[c] = bits ^ (lax.shift_right_arithmetic(bits, 31) & 0x7FFFFFFF)

    grp = 8 * 8

    def count(pred):
        acc = jnp.zeros((grp, BLOCK), F32)
        for c in range(nchs):
            for g in range(CW // grp):
                acc = jnp.where(pred(sk_ref[c, g * grp:(g + 1) * grp, :]), acc + 1.0, acc)
        return jnp.sum(acc, axis=0, keepdims=True)

    kf = float(topk)

    def search(it, ans):
        cand = ans | lax.shift_left(jnp.int32(1), 31 - it)
        cand_s = cand ^ INT_MIN
        return jnp.where(count(lambda x: x >= cand_s) >= kf, cand, ans)

    ans = lax.fori_loop(0, 32, search, jnp.zeros((1, BLOCK), I32))
    thr = ans ^ INT_MIN
    n_gt = count(lambda x: x > thr)
    n_eq = count(lambda x: x == thr)
    need = kf - n_gt

    for c in range(nchs):
        sel = sk_ref[c] >= thr
        if c == last:
            sel = sel & causal_last
        mask_ref[c] = jnp.where(sel, 0.0, NEG)

    excess = jnp.where((n_eq > need) & (thr > KEY_NEG_INF), 1.0, 0.0)

    @pl.when(jnp.max(excess) > 0.0)
    def _():
        r_i = lax.broadcasted_iota(I32, (BLOCK, BLOCK), 0)
        c_i = lax.broadcasted_iota(I32, (BLOCK, BLOCK), 1)
        lower = jnp.where(c_i <= r_i, 1.0, 0.0).astype(BF16)
        carry = jnp.zeros((1, BLOCK), F32)
        for c in range(nchs):
            for t in range(tpc):
                rows = slice(t * BLOCK, (t + 1) * BLOCK)
                xs = sk_ref[c, rows, :]
                eq = xs == thr
                eqf = jnp.where(eq, 1.0, 0.0)
                rank = _dot(lower, eqf.astype(BF16)) + carry
                keep = (xs > thr) | (eq & (rank <= need))
                causal = r_i + (c * CW + t * BLOCK) <= c_i + i * BLOCK
                mask_ref[c, rows, :] = jnp.where(keep & causal, 0.0, NEG)
                carry = carry + jnp.sum(eqf, axis=0, keepdims=True)

    first_half = lax.broadcasted_iota(I32, (LANES, BLOCK), 0) < HEAD_DIM
    for p in range(npair):
        lanes = slice(p * LANES, (p + 1) * LANES)
        logits = []
        for c in range(nchs):
            s = _dot_nt(kb_ref[0, c * CW:(c + 1) * CW, lanes], q2_ref[p])
            msk = mask_ref[c]
            per_head = []
            for hh in range(2):
                bias = jnp.concatenate(
                    [bias_ref[2 * p + hh, jnp.maximum(i - (c * tpc + t), 0)] for t in range(tpc)], axis=0)
                per_head.append(s[:, hh * BLOCK:(hh + 1) * BLOCK] + bias + msk)
            logits.append(per_head)
        out_t = jnp.zeros((LANES, 2 * BLOCK), F32)
        dens = []
        pes = [[None, None] for _ in range(nchs)]
        for hh in range(2):
            m = None
            for c in range(nchs):
                mc = jnp.max(logits[c][hh], axis=0, keepdims=True)
                m = mc if m is None else jnp.maximum(m, mc)
            den = jnp.zeros((1, BLOCK), F32)
            for c in range(nchs):
                pe = jnp.exp2(logits[c][hh] - m)
                den = den + jnp.sum(pe, axis=0, keepdims=True)
                pes[c][hh] = pe.astype(BF16)
            dens.append(den)
        for c in range(nchs):
            out_t = out_t + _dot(vt_ref[0, lanes, c * CW:(c + 1) * CW], jnp.concatenate(pes[c], axis=1))
        o_t = jnp.where(first_half, out_t[:, :BLOCK] / dens[0], out_t[:, BLOCK:] / dens[1])
        o_ref[:, lanes] = o_t.T.astype(BF16)


def _dsa_kernel(qi_ref, wi_ref, ki_ref, qb_ref, kb_ref, vt_ref, bias_ref, o_ref,
                qs_ref, wt_ref, q2_ref, sk_ref, mask_ref, *, topk, n_heads, n_chunks):
    i = pl.program_id(1)
    npair = n_heads // 2

    qi = qi_ref[...]
    lane_i = _lane_id_bf16(qi.shape)
    for h in range(IDX_HEADS):
        in_head = (lane_i >= h * IDX_DIM) & (lane_i < (h + 1) * IDX_DIM)
        qs_ref[h * BLOCK:(h + 1) * BLOCK, :] = jnp.where(in_head, qi, jnp.zeros_like(qi))
    wt_ref[...] = wi_ref[...].T
    for p in range(npair):
        q2_ref[p] = _stack_heads(qb_ref[:, p * LANES:(p + 1) * LANES])

    for nchs in range(1, n_chunks + 1):
        @pl.when(i // (CW // BLOCK) + 1 == nchs)
        def _(nchs=nchs):
            _dsa_body(nchs, i, ki_ref, kb_ref, vt_ref, bias_ref, o_ref, qs_ref, wt_ref, q2_ref, sk_ref,
                      mask_ref, topk=topk, npair=npair)


def _dsa(qi, wi, ki, qb, kb, vt, bias, batch, seq, topk):
    bw = qb.shape[-1]
    nidx = qi.shape[-1]
    nb = seq // BLOCK
    n_heads = bw // HEAD_DIM
    npair = n_heads // 2
    qrow = lambda width: pl.BlockSpec((BLOCK, width), lambda b, i: (b * nb + i, 0))
    full = lambda width: pl.BlockSpec((1, seq, width), lambda b, i: (b, 0, 0))
    vm = pltpu.VMEM
    return pl.pallas_call(
        functools.partial(_dsa_kernel, topk=topk, n_heads=n_heads, n_chunks=seq // CW),
        grid=(batch, nb),
        in_specs=[qrow(nidx), qrow(LANES), full(nidx), qrow(bw), full(bw),
                  pl.BlockSpec((1, bw, seq), lambda b, i: (b, 0, 0)),
                  _const_spec((n_heads, nb, BLOCK, BLOCK))],
        out_specs=qrow(bw),
        out_shape=jax.ShapeDtypeStruct((batch * seq, bw), BF16),
        scratch_shapes=[vm((IDX_HEADS * BLOCK, nidx), BF16), vm((LANES, BLOCK), F32),
                        vm((npair, 2 * BLOCK, LANES), BF16),
                        vm((seq // CW, CW, BLOCK), I32), vm((seq // CW, CW, BLOCK), F32)],
        compiler_params=_cparams(2),
        name="dsa",
    )(qi, wi, ki.reshape(batch, seq, nidx), qb, kb.reshape(batch, seq, bw), vt, bias)


def _split_rows(w_out, parts):
    ws, r0 = [], 0
    for a in parts:
        ws.append(w_out[r0:r0 + a.shape[1]].astype(BF16))
        r0 += a.shape[1]
    return ws


def _odd_proj_weights(w_in, qw, kvw):
    d = w_in.shape[0]
    dup = lambda w: jnp.repeat(w.reshape(d, kvw // HEAD_DIM, 1, HEAD_DIM), 2, axis=2).reshape(d, 2 * kvw)
    return jnp.concatenate([w_in[:, :qw], dup(w_in[:, qw:qw + kvw]), dup(w_in[:, qw + kvw:])], axis=1).astype(BF16)


def kernel(x, norm_g, final_g, rel_bias_table, ffn_w1, ffn_w3, ffn_w2, hyb_w_in, hyb_cq_g, hyb_wq_b,
           hyb_wq_idx, hyb_w_out, swa_w_in, swa_sinks, swa_w_out):
    batch, seq, d = x.shape
    depth = norm_g.shape[0]
    n_heads = d // HEAD_DIM
    a_heads = n_heads // 2
    topk = min(TOPK_MAX, seq // 4)
    assert seq % (BLOCK * 16) == 0 and seq % CW == 0 and (batch * seq) % TM == 0 and seq % TM == 0

    bias_a = _build_bias(
        rel_bias_table,
        np.concatenate([_band_bucket_idx(dil, win // dil) for win, dil in A_BRANCHES], axis=0),
        0, a_heads).reshape(a_heads // 2, 2, 3, BLOCK, 2 * BLOCK).transpose(2, 0, 1, 3, 4)
    bias_a = bias_a.reshape(3, a_heads // 2, 2 * BLOCK, 2 * BLOCK)
    bias_b = _build_bias(rel_bias_table, _dense_bucket_idx(seq), a_heads, n_heads - a_heads)
    bias_b = bias_b.reshape(n_heads - a_heads, seq // BLOCK, BLOCK, BLOCK)
    bias_c = _build_bias(rel_bias_table, _band_bucket_idx(1, C_WINDOW - 1), 0, n_heads)

    h = x.reshape(batch * seq, d)
    for layer in range(depth):
        g = norm_g[layer]
        i = layer // 2
        ffn_a = (g[0], ffn_w1[layer, 0], ffn_w3[layer, 0], ffn_w2[layer, 0])
        ffn_b = (g[2], ffn_w1[layer, 1], ffn_w3[layer, 1], ffn_w2[layer, 1])
        if layer % 2 == 0:
            h = _ffn(h, *ffn_a, final_g)
            outs = _proj_even(h, g[1], hyb_w_in[i], hyb_cq_g[i], hyb_wq_b[i], hyb_wq_idx[i], batch, seq)
            o_a = _attn_a(outs[:9], bias_a, batch, seq)
            qb, kb, vt, qi, ki, wi = outs[9:]
            o_b = _dsa(qi, wi, ki, qb, kb, vt, bias_b, batch, seq, topk)
            mix, w_out = [o_a.reshape(batch * seq, -1), o_b], hyb_w_out[i]
        else:
            kvw = (swa_w_in.shape[-1] - d) // 2
            n_kv = kvw // HEAD_DIM
            group = n_heads // n_kv
            h, q, k2, v2 = _ffn(h, *ffn_a, g[1], post_w=_odd_proj_weights(swa_w_in[i], d, kvw),
                                post_splits=((d, Q_SCALE), (2 * kvw, 1.0), (2 * kvw, 1.0)))
            r3 = lambda a: a.reshape(batch, seq, a.shape[-1])
            o_c = _attn_c(r3(q), r3(k2), r3(v2), bias_c.reshape(n_kv, group * BLOCK, 2 * BLOCK),
                          swa_sinks[i].astype(F32), batch, seq, group)
            mix, w_out = [o_c.reshape(batch * seq, -1)], swa_w_out[i]
        h = _ffn(h, *ffn_b, final_g, mix=mix, mix_w=_split_rows(w_out, mix), final=layer == depth - 1)
    return h.reshape(batch, seq, d)
```

```python
import functools
import math

import numpy as np
import jax
import jax.numpy as jnp
from jax import lax
from jax.experimental import pallas as pl
from jax.experimental.pallas import tpu as pltpu

F32 = jnp.float32
BF16 = jnp.bfloat16
I32 = jnp.int32
I16 = jnp.int16
HALF = 1 << 15

EPS = 1e-6
HEAD_DIM = 64
BLOCK = 128
LANES = 128
NUM_BUCKETS = 32
MAX_DISTANCE = 2048
A_BRANCHES = ((128, 1), (512, 4), (2048, 16))
C_WINDOW = 128
TOPK_MAX = 256
IDX_HEADS = 8
IDX_DIM = 32
NEG = -1e30
LOG2E = math.log2(math.e)
Q_SCALE = HEAD_DIM ** -0.5 * LOG2E
KEY_NEG_INF = (0xFF800000 - (1 << 32)) ^ 0x7FFFFFFF
VMEM_LIMIT = 56 * 1024 * 1024

TM = 512
CW = 4 * BLOCK
C_UNROLL = 3


def _cparams(n_axes):
    return pltpu.CompilerParams(
        dimension_semantics=("arbitrary",) * n_axes, vmem_limit_bytes=VMEM_LIMIT)


def _rms(x, g):
    return x * lax.rsqrt(jnp.mean(x * x, axis=-1, keepdims=True) + EPS) * g


def _dot(a, b):
    return jnp.dot(a, b, preferred_element_type=F32)


def _dot_nt(a, b):
    return lax.dot_general(a, b, (((1,), (1,)), ((), ())), preferred_element_type=F32)


def _lane_id_bf16(shape):
    return lax.broadcasted_iota(I32, shape, 1).astype(F32).astype(BF16)


def _rows(start, size):
    if isinstance(start, int):
        return pl.ds(start, size)
    return pl.ds(pl.multiple_of(start, size), size)


def _const_spec(shape):
    nd = len(shape)
    return pl.BlockSpec(shape, lambda *_: (0,) * nd, pipeline_mode=pl.Buffered(1))


def _t5_bucket_np(dist):
    dist = np.maximum(dist, 0)
    max_exact = NUM_BUCKETS // 2
    d_f = np.maximum(dist, 1).astype(np.float32)
    large = max_exact + (np.log(d_f / max_exact) / math.log(MAX_DISTANCE / max_exact)
                         * (NUM_BUCKETS - max_exact)).astype(np.int32)
    large = np.minimum(large, NUM_BUCKETS - 1)
    return np.where(dist < max_exact, dist, large).astype(np.int32)


def _band_bucket_idx(dil, max_dist):
    loc = np.arange(BLOCK)[:, None] + BLOCK - np.arange(2 * BLOCK)[None, :]
    valid = (loc >= 0) & (loc <= max_dist)
    return np.where(valid, _t5_bucket_np(dil * loc), -1).astype(np.int32)


def _dense_bucket_idx(seq):
    nb = seq // BLOCK
    d = np.arange(nb)[:, None, None] * BLOCK - np.arange(BLOCK)[None, :, None] + np.arange(BLOCK)[None, None, :]
    return np.where(d >= 0, _t5_bucket_np(d), -1).astype(np.int32).reshape(seq, BLOCK)


def _bias_kernel(tab_ref, idx_ref, o_ref, *, head0):
    h = pl.program_id(0) + head0
    idx = idx_ref[...]
    out = jnp.full(idx.shape, NEG, F32)
    for b in range(NUM_BUCKETS):
        out = jnp.where(idx == b, tab_ref[b, h] * LOG2E, out)
    o_ref[0] = out


def _build_bias(table, idx_np, head0, n_heads):
    r, c = idx_np.shape
    return pl.pallas_call(
        functools.partial(_bias_kernel, head0=head0),
        grid=(n_heads,),
        in_specs=[pl.BlockSpec(memory_space=pltpu.SMEM),
                  pl.BlockSpec((r, c), lambda h: (0, 0))],
        out_specs=pl.BlockSpec((1, r, c), lambda h: (h, 0, 0)),
        out_shape=jax.ShapeDtypeStruct((n_heads, r, c), F32),
        compiler_params=_cparams(1),
        name="bias_build",
    )(table.astype(F32), jnp.asarray(idx_np))


def _ffn_kernel(*refs, n_mix, final, post_splits):
    it = iter(refs)
    x_ref = next(it)
    mix = [next(it) for _ in range(n_mix)]
    mix_w = [next(it) for _ in range(n_mix)]
    g_ref, w1_ref, w3_ref, w2_ref, g2_ref = (next(it) for _ in range(5))
    wp_ref = next(it) if post_splits else None
    o_ref = next(it)
    post_refs = [next(it) for _ in post_splits]
    x = x_ref[...]
    for a_ref, w_ref in zip(mix, mix_w):
        x = x + _dot(a_ref[...], w_ref[...])
    xn = _rms(x, g_ref[...]).astype(BF16)
    h1 = _dot(xn, w1_ref[...])
    h3 = _dot(xn, w3_ref[...])
    act = (h1 * (1.0 / (1.0 + jnp.exp(-h1)))) * h3
    y = _dot(act.astype(BF16), w2_ref[...])
    out = x + 0.5 * y
    if final:
        out = _rms(out, g2_ref[...])
    o_ref[...] = out
    if post_splits:
        p = _dot(_rms(out, g2_ref[...]).astype(BF16), wp_ref[...])
        c0 = 0
        for p_ref, (width, scale) in zip(post_refs, post_splits):
            blk = p[:, c0:c0 + width]
            p_ref[...] = (blk if scale == 1.0 else blk * scale).astype(BF16)
            c0 += width


def _ffn(h, g, w1, w3, w2, g2, *, mix=(), mix_w=(), final=False, post_w=None, post_splits=()):
    t, d = h.shape
    dff = w1.shape[1]
    tok = lambda width: pl.BlockSpec((TM, width), lambda i: (i, 0))
    sh = jax.ShapeDtypeStruct
    in_specs = [tok(d)] + [tok(a.shape[1]) for a in mix] + [_const_spec(w.shape) for w in mix_w]
    in_specs += [_const_spec((1, d)), _const_spec((d, dff)), _const_spec((d, dff)), _const_spec((dff, d)),
                 _const_spec((1, d))]
    args = [h, *mix, *mix_w, g.reshape(1, d), w1.astype(BF16), w3.astype(BF16), w2.astype(BF16), g2.reshape(1, d)]
    out_specs, out_shape = [tok(d)], [sh((t, d), F32)]
    if post_splits:
        in_specs.append(_const_spec(post_w.shape))
        args.append(post_w)
        out_specs += [tok(width) for width, _ in post_splits]
        out_shape += [sh((t, width), BF16) for width, _ in post_splits]
    res = pl.pallas_call(
        functools.partial(_ffn_kernel, n_mix=len(mix), final=final, post_splits=tuple(post_splits)),
        grid=(t // TM,),
        in_specs=in_specs,
        out_specs=out_specs,
        out_shape=out_shape,
        compiler_params=_cparams(1),
        name="ffn",
    )(*args)
    return res if post_splits else res[0]


def _fold_store(scr, blk, outs, n_slab):
    o1, o4, o16 = outs
    o1[...] = blk.astype(BF16)
    for c in range(n_slab):
        scr[c] = blk[:, c * LANES:(c + 1) * LANES]
    for o, dil in ((o4, 4), (o16, 16)):
        rows = TM // dil
        for r in range(dil):
            for c in range(n_slab):
                o[0, r, :, c * LANES:(c + 1) * LANES] = scr[c, pl.ds(r, rows, stride=dil), :].astype(BF16)


def _proj_even_kernel(h_ref, g_ref, w_ref, cqg_ref, wqb_ref, wqi_ref,
                      q1, k1, v1, q4, k4, v4, q16, k16, v16, qb, kb, vb, qi, ki, wi, scr,
                      *, aw, bw, rank):
    u = _rms(h_ref[...], g_ref[...]).astype(BF16)
    p = _dot(u, w_ref[...])
    proj = lambda c0, width: p[:, c0:c0 + width]
    n_slab = aw // LANES
    scale = Q_SCALE
    _fold_store(scr, proj(0, aw) * scale, (q1, q4, q16), n_slab)
    _fold_store(scr, proj(aw, aw), (k1, k4, k16), n_slab)
    _fold_store(scr, proj(2 * aw, aw), (v1, v4, v16), n_slab)
    c0 = 3 * aw
    kb[...] = proj(c0, bw).astype(BF16)
    vb[0] = proj(c0 + bw, bw).T.astype(BF16)
    c1 = c0 + 2 * bw
    cq = _rms(proj(c1, rank), cqg_ref[...]).astype(BF16)
    qb[...] = (_dot(cq, wqb_ref[...]) * scale).astype(BF16)
    qi[...] = _dot(cq, wqi_ref[...]).astype(BF16)
    c2 = c1 + rank
    nidx = IDX_HEADS * IDX_DIM
    ki[...] = proj(c2, nidx).astype(BF16)
    idx_scale = (IDX_DIM ** -0.5) * (IDX_HEADS ** -0.5)
    wi[...] = proj(c2 + nidx, LANES) * idx_scale


def _proj_even(h, g, w_in, cq_g, wq_b, wq_idx, batch, seq):
    t, d = h.shape
    bw = wq_b.shape[1]
    rank = wq_b.shape[0]
    aw = (w_in.shape[1] - rank - 2 * bw - IDX_DIM - IDX_HEADS) // 3
    nidx = IDX_HEADS * IDX_DIM
    offs = np.cumsum([0, aw, aw, aw, rank, bw, bw, IDX_DIM, IDX_HEADS])
    col = lambda i: w_in[:, offs[i]:offs[i + 1]]
    w_kidx = jnp.tile(col(6), (1, IDX_HEADS))
    w_widx = jnp.pad(col(7), ((0, 0), (0, LANES - IDX_HEADS)))
    w_all = jnp.concatenate([col(0), col(1), col(2), col(4), col(5), col(3), w_kidx, w_widx], axis=1).astype(BF16)
    ncol = w_all.shape[1]
    spt = seq // TM
    tok = lambda width: pl.BlockSpec((TM, width), lambda i: (i, 0))
    fold = lambda dil: pl.BlockSpec((1, dil, TM // dil, aw), lambda i: (i // spt, 0, i % spt, 0))
    sh = jax.ShapeDtypeStruct
    nat = sh((t, aw), BF16)
    f4 = sh((batch, 4, seq // 4, aw), BF16)
    f16 = sh((batch, 16, seq // 16, aw), BF16)
    return pl.pallas_call(
        functools.partial(_proj_even_kernel, aw=aw, bw=bw, rank=rank),
        grid=(t // TM,),
        in_specs=[tok(d), _const_spec((1, d)), _const_spec((d, ncol)), _const_spec((1, rank)),
                  _const_spec((rank, bw)), _const_spec((rank, nidx))],
        out_specs=[tok(aw), tok(aw), tok(aw), fold(4), fold(4), fold(4), fold(16), fold(16), fold(16),
                   tok(bw), tok(bw), pl.BlockSpec((1, bw, TM), lambda i: (i // spt, 0, i % spt)),
                   tok(nidx), tok(nidx), tok(LANES)],
        out_shape=[nat, nat, nat, f4, f4, f4, f16, f16, f16,
                   sh((t, bw), BF16), sh((t, bw), BF16), sh((batch, bw, seq), BF16),
                   sh((t, nidx), BF16), sh((t, nidx), BF16), sh((t, LANES), F32)],
        scratch_shapes=[pltpu.VMEM((aw // LANES, TM, LANES), F32)],
        compiler_params=_cparams(1),
        name="proj_even",
    )(h, g.reshape(1, d), w_all, cq_g.reshape(1, rank), wq_b.astype(BF16), wq_idx.astype(BF16))


def _stack_heads(q):
    lane_b = _lane_id_bf16(q.shape)
    z = jnp.zeros_like(q)
    return jnp.concatenate([jnp.where(lane_b < HEAD_DIM, q, z), jnp.where(lane_b >= HEAD_DIM, q, z)], axis=0)


def _band_block(q2, kw, vw, bias2):
    s = _dot_nt(q2, kw) + bias2
    m = jnp.max(s, axis=-1, keepdims=True)
    p = jnp.exp2(s - m)
    den = jnp.sum(p, axis=-1, keepdims=True)
    return _dot(p.astype(BF16), vw), m, den


def _attn_a_kernel(q1, k1, v1, q4, k4, v4, q16, k16, v16, bias_ref, o_ref, acc_s, m_s, den_s, *, seq):
    lo = lax.broadcasted_iota(I32, (BLOCK, LANES), 1) < HEAD_DIM
    shape = (BLOCK, LANES)
    refs = ((q1, k1, v1), (q4, k4, v4), (q16, k16, v16))
    for br, (_, dil) in enumerate(A_BRANCHES):
        qr, kr, vr = refs[br]
        for r in range(dil):
            at = (lambda ref, s, n: ref[0, pl.ds(s, n), :]) if dil == 1 else \
                 (lambda ref, s, n, r=r: ref[0, r, pl.ds(s, n), :])
            for n in range(seq // dil // BLOCK):
                q2 = _stack_heads(at(qr, n * BLOCK, BLOCK))
                if n == 0:
                    acc, m, den = _band_block(q2, at(kr, 0, BLOCK), at(vr, 0, BLOCK), bias_ref[br, 0, :, BLOCK:])
                else:
                    k0 = (n - 1) * BLOCK
                    acc, m, den = _band_block(q2, at(kr, k0, 2 * BLOCK), at(vr, k0, 2 * BLOCK), bias_ref[br, 0])
                rows = pl.ds(n * BLOCK, BLOCK) if dil == 1 else pl.ds(n * BLOCK * dil + r, BLOCK, stride=dil)
                acc_s[br, rows, :] = jnp.where(lo, acc[:BLOCK], acc[BLOCK:])
                m_s[br, rows, :] = jnp.where(lo, jnp.broadcast_to(m[:BLOCK], shape), jnp.broadcast_to(m[BLOCK:], shape))
                den_s[br, rows, :] = jnp.where(lo, jnp.broadcast_to(den[:BLOCK], shape),
                                               jnp.broadcast_to(den[BLOCK:], shape))

    chunk = 2 * BLOCK

    def merge(c, carry):
        rows = pl.ds(pl.multiple_of(c * chunk, chunk), chunk)
        ms = [m_s[br, rows, :] for br in range(3)]
        mx = jnp.maximum(jnp.maximum(ms[0], ms[1]), ms[2])
        num = jnp.zeros((chunk, LANES), F32)
        den = jnp.zeros((chunk, LANES), F32)
        for br in range(3):
            e = jnp.exp2(ms[br] - mx)
            num = num + e * acc_s[br, rows, :]
            den = den + e * den_s[br, rows, :]
        o_ref[0, rows, :] = (num / den).astype(BF16)
        return carry

    lax.fori_loop(0, seq // chunk, merge, 0)


def _attn_a(qkv, bias, batch, seq):
    q1, k1, v1, q4, k4, v4, q16, k16, v16 = qkv
    aw = q1.shape[-1]
    npair = aw // LANES
    nat = pl.BlockSpec((1, seq, LANES), lambda b, p: (b, 0, p))
    fold = lambda dil: pl.BlockSpec((1, dil, seq // dil, LANES), lambda b, p: (b, 0, 0, p))
    r3 = lambda a: a.reshape(batch, seq, aw)
    return pl.pallas_call(
        functools.partial(_attn_a_kernel, seq=seq),
        grid=(batch, npair),
        in_specs=[nat, nat, nat, fold(4), fold(4), fold(4), fold(16), fold(16), fold(16),
                  pl.BlockSpec((3, 1, 2 * BLOCK, 2 * BLOCK), lambda b, p: (0, p, 0, 0))],
        out_specs=nat,
        out_shape=jax.ShapeDtypeStruct((batch, seq, aw), BF16),
        scratch_shapes=[pltpu.VMEM((3, seq, LANES), F32)] * 3,
        compiler_params=_cparams(2),
        name="attn_dilated",
    )(r3(q1), r3(k1), r3(v1), q4, k4, v4, q16, k16, v16, bias)


def _attn_c_kernel(sink_ref, q_ref, k_ref, v_ref, bias_ref, o_ref, *, seq, group):
    g = pl.program_id(1)
    npair = group // 2
    lo = lax.broadcasted_iota(I32, (BLOCK, LANES), 1) < HEAD_DIM

    def block(q0, k0, width, bias2):
        qn = q_ref[0, pl.ds(q0, BLOCK), :]
        q2 = jnp.concatenate([_stack_heads(qn[:, p * LANES:(p + 1) * LANES]) for p in range(npair)], axis=0)
        acc, m, den = _band_block(q2, k_ref[0, pl.ds(k0, width), :], v_ref[0, pl.ds(k0, width), :], bias2)
        for p in range(npair):
            outs = []
            for hh in range(2):
                h = 2 * p + hh
                r = slice(h * BLOCK, (h + 1) * BLOCK)
                outs.append(acc[r] / (den[r] + jnp.exp2(sink_ref[g * group + h] * LOG2E - m[r])))
            o_ref[0, pl.ds(q0, BLOCK), p * LANES:(p + 1) * LANES] = jnp.where(lo, outs[0], outs[1]).astype(BF16)

    block(0, 0, BLOCK, bias_ref[0, :, BLOCK:])

    def body(n, carry):
        block(pl.multiple_of(n * BLOCK, BLOCK), pl.multiple_of((n - 1) * BLOCK, BLOCK), 2 * BLOCK, bias_ref[0])
        return carry

    lax.fori_loop(1, seq // BLOCK, body, 0, unroll=C_UNROLL)


def _attn_c(q, k2, v2, bias, sinks, batch, seq, group):
    qw = q.shape[-1]
    n_kv = k2.shape[-1] // LANES
    gw = qw // n_kv
    return pl.pallas_call(
        functools.partial(_attn_c_kernel, seq=seq, group=group),
        grid=(batch, n_kv),
        in_specs=[pl.BlockSpec(memory_space=pltpu.SMEM),
                  pl.BlockSpec((1, seq, gw), lambda b, g: (b, 0, g)),
                  pl.BlockSpec((1, seq, LANES), lambda b, g: (b, 0, g)),
                  pl.BlockSpec((1, seq, LANES), lambda b, g: (b, 0, g)),
                  pl.BlockSpec((1, group * BLOCK, 2 * BLOCK), lambda b, g: (g, 0, 0))],
        out_specs=pl.BlockSpec((1, seq, gw), lambda b, g: (b, 0, g)),
        out_shape=jax.ShapeDtypeStruct((batch, seq, qw), BF16),
        compiler_params=_cparams(2),
        name="attn_swa",
    )(sinks, q, k2, v2, bias)


def _dsa_body(nchs, i, ki_ref, kb_ref, vt_ref, bias_ref, o_ref, qs_ref, wt_ref, q2_ref, sk_ref, hi_ref, lo_ref,
              mask_ref, *, topk, npair):
    tpc = CW // BLOCK
    last = nchs - 1
    key_pos = lax.broadcasted_iota(I32, (CW, BLOCK), 0) + last * CW
    qry_pos = lax.broadcasted_iota(I32, (CW, BLOCK), 1) + i * BLOCK
    causal_last = key_pos <= qry_pos

    for c in range(nchs):
        st = _dot_nt(ki_ref[0, c * CW:(c + 1) * CW, :], qs_ref[...])
        score = None
        for h in range(IDX_HEADS):
            term = jnp.maximum(st[:, h * BLOCK:(h + 1) * BLOCK], 0.0) * wt_ref[h:h + 1, :]
            score = term if score is None else score + term
        score = jnp.where(score == 0.0, 0.0, score)
        if c == last:
            score = jnp.where(causal_last, score, -jnp.inf)
        bits = lax.bitcast_convert_type(score, I32)
        sk = bits ^ (lax.shift_right_arithmetic(bits, 31) & 0x7FFFFFFF)
        sk_ref[c] = sk
        hi_ref[c] = lax.shift_right_arithmetic(sk, 16).astype(I16)
        lo_ref[c] = ((sk & 0xFFFF) - HALF).astype(I16)

    grp = 8 * 16

    def count(pred):
        acc = jnp.zeros((grp, BLOCK), I16)
        for c in range(nchs):
            for g in range(CW // grp):
                rows = slice(g * grp, (g + 1) * grp)
                acc = jnp.where(pred(hi_ref[c, rows, :], lo_ref[c, rows, :]), acc + 1, acc)
        return jnp.sum(acc.astype(I32), axis=0, keepdims=True)

    def row16(v):
        return jnp.broadcast_to(v - HALF, (grp, BLOCK)).astype(I16)

    def kth_digit(use_hi, k):
        def step(it, ans):
            cand = ans | lax.shift_left(jnp.int32(1), 15 - it)
            c16 = row16(cand)
            n = count((lambda hi, lo: hi >= c16) if use_hi else (lambda hi, lo: lo >= c16))
            return jnp.where(n >= k, cand, ans)
        return lax.fori_loop(0, 16, step, jnp.zeros((1, BLOCK), I32))

    t_hi = kth_digit(True, topk)
    hi16 = row16(t_hi)
    n_gt_hi = count(lambda hi, lo: hi > hi16)
    lowest = jnp.full((CW, BLOCK), -HALF, I16)
    for c in range(nchs):
        lo_ref[c] = jnp.where(hi_ref[c] == row16(t_hi)[:1], lo_ref[c], lowest)
    t_lo = kth_digit(False, topk - n_gt_hi)
    lo16 = row16(t_lo)
    thr = lax.shift_left(t_hi - HALF, 16) | t_lo
    n_gt = n_gt_hi + count(lambda hi, lo: lo > lo16)
    n_eq = count(lambda hi, lo: (hi == hi16) & (lo == lo16))
    need = (topk - n_gt).astype(F32)
    n_eq = n_eq.astype(F32)

    for c in range(nchs):
        sel = sk_ref[c] >= thr
        if c == last:
            sel = sel & causal_last
        mask_ref[c] = jnp.where(sel, 0.0, NEG)

    excess = jnp.where((n_eq > need) & (thr > KEY_NEG_INF), 1.0, 0.0)

    @pl.when(jnp.max(excess) > 0.0)
    def _():
        r_i = lax.broadcasted_iota(I32, (BLOCK, BLOCK), 0)
        c_i = lax.broadcasted_iota(I32, (BLOCK, BLOCK), 1)
        lower = jnp.where(c_i <= r_i, 1.0, 0.0).astype(BF16)
        carry = jnp.zeros((1, BLOCK), F32)
        for c in range(nchs):
            for t in range(tpc):
                rows = slice(t * BLOCK, (t + 1) * BLOCK)
                xs = sk_ref[c, rows, :]
                eq = xs == thr
                eqf = jnp.where(eq, 1.0, 0.0)
                rank = _dot(lower, eqf.astype(BF16)) + carry
                keep = (xs > thr) | (eq & (rank <= need))
                causal = r_i + (c * CW + t * BLOCK) <= c_i + i * BLOCK
                mask_ref[c, rows, :] = jnp.where(keep & causal, 0.0, NEG)
                carry = carry + jnp.sum(eqf, axis=0, keepdims=True)

    first_half = lax.broadcasted_iota(I32, (LANES, BLOCK), 0) < HEAD_DIM
    for p in range(npair):
        lanes = slice(p * LANES, (p + 1) * LANES)
        logits = []
        for c in range(nchs):
            s = _dot_nt(kb_ref[0, c * CW:(c + 1) * CW, lanes], q2_ref[p])
            msk = mask_ref[c]
            per_head = []
            for hh in range(2):
                bias = jnp.concatenate(
                    [bias_ref[2 * p + hh, jnp.maximum(i - (c * tpc + t), 0)] for t in range(tpc)], axis=0)
                per_head.append(s[:, hh * BLOCK:(hh + 1) * BLOCK] + bias + msk)
            logits.append(per_head)
        out_t = jnp.zeros((LANES, 2 * BLOCK), F32)
        dens = []
        pes = [[None, None] for _ in range(nchs)]
        for hh in range(2):
            m = None
            for c in range(nchs):
                mc = jnp.max(logits[c][hh], axis=0, keepdims=True)
                m = mc if m is None else jnp.maximum(m, mc)
            den = jnp.zeros((1, BLOCK), F32)
            for c in range(nchs):
                pe = jnp.exp2(logits[c][hh] - m)
                den = den + jnp.sum(pe, axis=0, keepdims=True)
                pes[c][hh] = pe.astype(BF16)
            dens.append(den)
        for c in range(nchs):
            out_t = out_t + _dot(vt_ref[0, lanes, c * CW:(c + 1) * CW], jnp.concatenate(pes[c], axis=1))
        o_t = jnp.where(first_half, out_t[:, :BLOCK] / dens[0], out_t[:, BLOCK:] / dens[1])
        o_ref[:, lanes] = o_t.T.astype(BF16)


def _dsa_kernel(qi_ref, wi_ref, ki_ref, qb_ref, kb_ref, vt_ref, bias_ref, o_ref,
                qs_ref, wt_ref, q2_ref, sk_ref, hi_ref, lo_ref, mask_ref, *, topk, n_heads, n_chunks):
    i = pl.program_id(1)
    npair = n_heads // 2

    qi = qi_ref[...]
    lane_i = _lane_id_bf16(qi.shape)
    for h in range(IDX_HEADS):
        in_head = (lane_i >= h * IDX_DIM) & (lane_i < (h + 1) * IDX_DIM)
        qs_ref[h * BLOCK:(h + 1) * BLOCK, :] = jnp.where(in_head, qi, jnp.zeros_like(qi))
    wt_ref[...] = wi_ref[...].T
    for p in range(npair):
        q2_ref[p] = _stack_heads(qb_ref[:, p * LANES:(p + 1) * LANES])

    for nchs in range(1, n_chunks + 1):
        @pl.when(i // (CW // BLOCK) + 1 == nchs)
        def _(nchs=nchs):
            _dsa_body(nchs, i, ki_ref, kb_ref, vt_ref, bias_ref, o_ref, qs_ref, wt_ref, q2_ref, sk_ref,
                      hi_ref, lo_ref, mask_ref, topk=topk, npair=npair)


def _dsa(qi, wi, ki, qb, kb, vt, bias, batch, seq, topk):
    bw = qb.shape[-1]
    nidx = qi.shape[-1]
    nb = seq // BLOCK
    n_heads = bw // HEAD_DIM
    npair = n_heads // 2
    qrow = lambda width: pl.BlockSpec((BLOCK, width), lambda b, i: (b * nb + i, 0))
    full = lambda width: pl.BlockSpec((1, seq, width), lambda b, i: (b, 0, 0))
    vm = pltpu.VMEM
    return pl.pallas_call(
        functools.partial(_dsa_kernel, topk=topk, n_heads=n_heads, n_chunks=seq // CW),
        grid=(batch, nb),
        in_specs=[qrow(nidx), qrow(LANES), full(nidx), qrow(bw), full(bw),
                  pl.BlockSpec((1, bw, seq), lambda b, i: (b, 0, 0)),
                  _const_spec((n_heads, nb, BLOCK, BLOCK))],
        out_specs=qrow(bw),
        out_shape=jax.ShapeDtypeStruct((batch * seq, bw), BF16),
        scratch_shapes=[vm((IDX_HEADS * BLOCK, nidx), BF16), vm((LANES, BLOCK), F32),
                        vm((npair, 2 * BLOCK, LANES), BF16),
                        vm((seq // CW, CW, BLOCK), I32), vm((seq // CW, CW, BLOCK), I16),
                        vm((seq // CW, CW, BLOCK), I16), vm((seq // CW, CW, BLOCK), F32)],
        compiler_params=_cparams(2),
        name="dsa",
    )(qi, wi, ki.reshape(batch, seq, nidx), qb, kb.reshape(batch, seq, bw), vt, bias)


def _split_rows(w_out, parts):
    ws, r0 = [], 0
    for a in parts:
        ws.append(w_out[r0:r0 + a.shape[1]].astype(BF16))
        r0 += a.shape[1]
    return ws


def _odd_proj_weights(w_in, qw, kvw):
    d = w_in.shape[0]
    dup = lambda w: jnp.repeat(w.reshape(d, kvw // HEAD_DIM, 1, HEAD_DIM), 2, axis=2).reshape(d, 2 * kvw)
    return jnp.concatenate([w_in[:, :qw], dup(w_in[:, qw:qw + kvw]), dup(w_in[:, qw + kvw:])], axis=1).astype(BF16)


def kernel(x, norm_g, final_g, rel_bias_table, ffn_w1, ffn_w3, ffn_w2, hyb_w_in, hyb_cq_g, hyb_wq_b,
           hyb_wq_idx, hyb_w_out, swa_w_in, swa_sinks, swa_w_out):
    batch, seq, d = x.shape
    depth = norm_g.shape[0]
    n_heads = d // HEAD_DIM
    a_heads = n_heads // 2
    topk = min(TOPK_MAX, seq // 4)
    assert seq % (BLOCK * 16) == 0 and seq % CW == 0 and (batch * seq) % TM == 0 and seq % TM == 0

    bias_a = _build_bias(
        rel_bias_table,
        np.concatenate([_band_bucket_idx(dil, win // dil) for win, dil in A_BRANCHES], axis=0),
        0, a_heads).reshape(a_heads // 2, 2, 3, BLOCK, 2 * BLOCK).transpose(2, 0, 1, 3, 4)
    bias_a = bias_a.reshape(3, a_heads // 2, 2 * BLOCK, 2 * BLOCK)
    bias_b = _build_bias(rel_bias_table, _dense_bucket_idx(seq), a_heads, n_heads - a_heads)
    bias_b = bias_b.reshape(n_heads - a_heads, seq // BLOCK, BLOCK, BLOCK)
    bias_c = _build_bias(rel_bias_table, _band_bucket_idx(1, C_WINDOW - 1), 0, n_heads)

    h = x.reshape(batch * seq, d)
    for layer in range(depth):
        g = norm_g[layer]
        i = layer // 2
        ffn_a = (g[0], ffn_w1[layer, 0], ffn_w3[layer, 0], ffn_w2[layer, 0])
        ffn_b = (g[2], ffn_w1[layer, 1], ffn_w3[layer, 1], ffn_w2[layer, 1])
        if layer % 2 == 0:
            h = _ffn(h, *ffn_a, final_g)
            outs = _proj_even(h, g[1], hyb_w_in[i], hyb_cq_g[i], hyb_wq_b[i], hyb_wq_idx[i], batch, seq)
            o_a = _attn_a(outs[:9], bias_a, batch, seq)
            qb, kb, vt, qi, ki, wi = outs[9:]
            o_b = _dsa(qi, wi, ki, qb, kb, vt, bias_b, batch, seq, topk)
            mix, w_out = [o_a.reshape(batch * seq, -1), o_b], hyb_w_out[i]
        else:
            kvw = (swa_w_in.shape[-1] - d) // 2
            n_kv = kvw // HEAD_DIM
            group = n_heads // n_kv
            h, q, k2, v2 = _ffn(h, *ffn_a, g[1], post_w=_odd_proj_weights(swa_w_in[i], d, kvw),
                                post_splits=((d, Q_SCALE), (2 * kvw, 1.0), (2 * kvw, 1.0)))
            r3 = lambda a: a.reshape(batch, seq, a.shape[-1])
            o_c = _attn_c(r3(q), r3(k2), r3(v2), bias_c.reshape(n_kv, group * BLOCK, 2 * BLOCK),
                          swa_sinks[i].astype(F32), batch, seq, group)
            mix, w_out = [o_c.reshape(batch * seq, -1)], swa_w_out[i]
        h = _ffn(h, *ffn_b, final_g, mix=mix, mix_w=_split_rows(w_out, mix), final=layer == depth - 1)
    return h.reshape(batch, seq, d)
```

```python
import functools
import math

import numpy as np
import jax
import jax.numpy as jnp
from jax import lax
from jax.experimental import pallas as pl
from jax.experimental.pallas import tpu as pltpu

F32 = jnp.float32
BF16 = jnp.bfloat16
I32 = jnp.int32

EPS = 1e-6
HEAD_DIM = 64
BLOCK = 128
LANES = 128
NUM_BUCKETS = 32
MAX_DISTANCE = 2048
A_BRANCHES = ((128, 1), (512, 4), (2048, 16))
C_WINDOW = 128
TOPK_MAX = 256
IDX_HEADS = 8
IDX_DIM = 32
NEG = -1e30
LOG2E = math.log2(math.e)
Q_SCALE = HEAD_DIM ** -0.5 * LOG2E
INT_MIN = -(2 ** 31)
KEY_NEG_INF = (0xFF800000 - (1 << 32)) ^ 0x7FFFFFFF
VMEM_LIMIT = 56 * 1024 * 1024

TM = 512
CW = 4 * BLOCK
C_UNROLL = 3


def _cparams(n_axes):
    return pltpu.CompilerParams(
        dimension_semantics=("arbitrary",) * n_axes, vmem_limit_bytes=VMEM_LIMIT)


def _rms(x, g):
    return x * lax.rsqrt(jnp.mean(x * x, axis=-1, keepdims=True) + EPS) * g


def _dot(a, b):
    return jnp.dot(a, b, preferred_element_type=F32)


def _dot_nt(a, b):
    return lax.dot_general(a, b, (((1,), (1,)), ((), ())), preferred_element_type=F32)


def _lane_id_bf16(shape):
    return lax.broadcasted_iota(I32, shape, 1).astype(F32).astype(BF16)


def _rows(start, size):
    if isinstance(start, int):
        return pl.ds(start, size)
    return pl.ds(pl.multiple_of(start, size), size)


def _const_spec(shape):
    nd = len(shape)
    return pl.BlockSpec(shape, lambda *_: (0,) * nd, pipeline_mode=pl.Buffered(1))


def _t5_bucket_np(dist):
    dist = np.maximum(dist, 0)
    max_exact = NUM_BUCKETS // 2
    d_f = np.maximum(dist, 1).astype(np.float32)
    large = max_exact + (np.log(d_f / max_exact) / math.log(MAX_DISTANCE / max_exact)
                         * (NUM_BUCKETS - max_exact)).astype(np.int32)
    large = np.minimum(large, NUM_BUCKETS - 1)
    return np.where(dist < max_exact, dist, large).astype(np.int32)


def _band_bucket_idx(dil, max_dist):
    loc = np.arange(BLOCK)[:, None] + BLOCK - np.arange(2 * BLOCK)[None, :]
    valid = (loc >= 0) & (loc <= max_dist)
    return np.where(valid, _t5_bucket_np(dil * loc), -1).astype(np.int32)


def _dense_bucket_idx(seq):
    nb = seq // BLOCK
    d = np.arange(nb)[:, None, None] * BLOCK - np.arange(BLOCK)[None, :, None] + np.arange(BLOCK)[None, None, :]
    return np.where(d >= 0, _t5_bucket_np(d), -1).astype(np.int32).reshape(seq, BLOCK)


def _bias_kernel(tab_ref, idx_ref, o_ref, *, head0):
    h = pl.program_id(0) + head0
    idx = idx_ref[...]
    out = jnp.full(idx.shape, NEG, F32)
    for b in range(NUM_BUCKETS):
        out = jnp.where(idx == b, tab_ref[b, h] * LOG2E, out)
    o_ref[0] = out


def _build_bias(table, idx_np, head0, n_heads):
    r, c = idx_np.shape
    return pl.pallas_call(
        functools.partial(_bias_kernel, head0=head0),
        grid=(n_heads,),
        in_specs=[pl.BlockSpec(memory_space=pltpu.SMEM),
                  pl.BlockSpec((r, c), lambda h: (0, 0))],
        out_specs=pl.BlockSpec((1, r, c), lambda h: (h, 0, 0)),
        out_shape=jax.ShapeDtypeStruct((n_heads, r, c), F32),
        compiler_params=_cparams(1),
        name="bias_build",
    )(table.astype(F32), jnp.asarray(idx_np))


def _ffn_kernel(*refs, n_mix, final, post_splits):
    it = iter(refs)
    x_ref = next(it)
    mix = [next(it) for _ in range(n_mix)]
    mix_w = [next(it) for _ in range(n_mix)]
    g_ref, w1_ref, w3_ref, w2_ref, g2_ref = (next(it) for _ in range(5))
    wp_ref = next(it) if post_splits else None
    o_ref = next(it)
    post_refs = [next(it) for _ in post_splits]
    x = x_ref[...]
    for a_ref, w_ref in zip(mix, mix_w):
        x = x + _dot(a_ref[...], w_ref[...])
    xn = _rms(x, g_ref[...]).astype(BF16)
    h1 = _dot(xn, w1_ref[...])
    h3 = _dot(xn, w3_ref[...])
    act = (h1 * (1.0 / (1.0 + jnp.exp(-h1)))) * h3
    y = _dot(act.astype(BF16), w2_ref[...])
    out = x + 0.5 * y
    if final:
        out = _rms(out, g2_ref[...])
    o_ref[...] = out
    if post_splits:
        p = _dot(_rms(out, g2_ref[...]).astype(BF16), wp_ref[...])
        c0 = 0
        for p_ref, (width, scale) in zip(post_refs, post_splits):
            blk = p[:, c0:c0 + width]
            p_ref[...] = (blk if scale == 1.0 else blk * scale).astype(BF16)
            c0 += width


def _ffn(h, g, w1, w3, w2, g2, *, mix=(), mix_w=(), final=False, post_w=None, post_splits=()):
    t, d = h.shape
    dff = w1.shape[1]
    tok = lambda width: pl.BlockSpec((TM, width), lambda i: (i, 0))
    sh = jax.ShapeDtypeStruct
    in_specs = [tok(d)] + [tok(a.shape[1]) for a in mix] + [_const_spec(w.shape) for w in mix_w]
    in_specs += [_const_spec((1, d)), _const_spec((d, dff)), _const_spec((d, dff)), _const_spec((dff, d)),
                 _const_spec((1, d))]
    args = [h, *mix, *mix_w, g.reshape(1, d), w1.astype(BF16), w3.astype(BF16), w2.astype(BF16), g2.reshape(1, d)]
    out_specs, out_shape = [tok(d)], [sh((t, d), F32)]
    if post_splits:
        in_specs.append(_const_spec(post_w.shape))
        args.append(post_w)
        out_specs += [tok(width) for width, _ in post_splits]
        out_shape += [sh((t, width), BF16) for width, _ in post_splits]
    res = pl.pallas_call(
        functools.partial(_ffn_kernel, n_mix=len(mix), final=final, post_splits=tuple(post_splits)),
        grid=(t // TM,),
        in_specs=in_specs,
        out_specs=out_specs,
        out_shape=out_shape,
        compiler_params=_cparams(1),
        name="ffn",
    )(*args)
    return res if post_splits else res[0]


def _fold_store(scr, blk, outs, n_slab):
    o1, o4, o16 = outs
    o1[...] = blk.astype(BF16)
    for c in range(n_slab):
        scr[c] = blk[:, c * LANES:(c + 1) * LANES]
    for o, dil in ((o4, 4), (o16, 16)):
        rows = TM // dil
        for r in range(dil):
            for c in range(n_slab):
                o[0, r, :, c * LANES:(c + 1) * LANES] = scr[c, pl.ds(r, rows, stride=dil), :].astype(BF16)


def _proj_even_kernel(h_ref, g_ref, w_ref, cqg_ref, wqb_ref, wqi_ref,
                      q1, k1, v1, q4, k4, v4, q16, k16, v16, qb, kb, vb, qi, ki, wi, scr,
                      *, aw, bw, rank):
    u = _rms(h_ref[...], g_ref[...]).astype(BF16)
    p = _dot(u, w_ref[...])
    proj = lambda c0, width: p[:, c0:c0 + width]
    n_slab = aw // LANES
    scale = Q_SCALE
    _fold_store(scr, proj(0, aw) * scale, (q1, q4, q16), n_slab)
    _fold_store(scr, proj(aw, aw), (k1, k4, k16), n_slab)
    _fold_store(scr, proj(2 * aw, aw), (v1, v4, v16), n_slab)
    c0 = 3 * aw
    kb[...] = proj(c0, bw).astype(BF16)
    vb[0] = proj(c0 + bw, bw).T.astype(BF16)
    c1 = c0 + 2 * bw
    cq = _rms(proj(c1, rank), cqg_ref[...]).astype(BF16)
    qb[...] = (_dot(cq, wqb_ref[...]) * scale).astype(BF16)
    qi[...] = _dot(cq, wqi_ref[...]).astype(BF16)
    c2 = c1 + rank
    nidx = IDX_HEADS * IDX_DIM
    ki[...] = proj(c2, nidx).astype(BF16)
    idx_scale = (IDX_DIM ** -0.5) * (IDX_HEADS ** -0.5)
    wi[...] = proj(c2 + nidx, LANES) * idx_scale


def _proj_even(h, g, w_in, cq_g, wq_b, wq_idx, batch, seq):
    t, d = h.shape
    bw = wq_b.shape[1]
    rank = wq_b.shape[0]
    aw = (w_in.shape[1] - rank - 2 * bw - IDX_DIM - IDX_HEADS) // 3
    nidx = IDX_HEADS * IDX_DIM
    offs = np.cumsum([0, aw, aw, aw, rank, bw, bw, IDX_DIM, IDX_HEADS])
    col = lambda i: w_in[:, offs[i]:offs[i + 1]]
    w_kidx = jnp.tile(col(6), (1, IDX_HEADS))
    w_widx = jnp.pad(col(7), ((0, 0), (0, LANES - IDX_HEADS)))
    w_all = jnp.concatenate([col(0), col(1), col(2), col(4), col(5), col(3), w_kidx, w_widx], axis=1).astype(BF16)
    ncol = w_all.shape[1]
    spt = seq // TM
    tok = lambda width: pl.BlockSpec((TM, width), lambda i: (i, 0))
    fold = lambda dil: pl.BlockSpec((1, dil, TM // dil, aw), lambda i: (i // spt, 0, i % spt, 0))
    sh = jax.ShapeDtypeStruct
    nat = sh((t, aw), BF16)
    f4 = sh((batch, 4, seq // 4, aw), BF16)
    f16 = sh((batch, 16, seq // 16, aw), BF16)
    return pl.pallas_call(
        functools.partial(_proj_even_kernel, aw=aw, bw=bw, rank=rank),
        grid=(t // TM,),
        in_specs=[tok(d), _const_spec((1, d)), _const_spec((d, ncol)), _const_spec((1, rank)),
                  _const_spec((rank, bw)), _const_spec((rank, nidx))],
        out_specs=[tok(aw), tok(aw), tok(aw), fold(4), fold(4), fold(4), fold(16), fold(16), fold(16),
                   tok(bw), tok(bw), pl.BlockSpec((1, bw, TM), lambda i: (i // spt, 0, i % spt)),
                   tok(nidx), tok(nidx), tok(LANES)],
        out_shape=[nat, nat, nat, f4, f4, f4, f16, f16, f16,
                   sh((t, bw), BF16), sh((t, bw), BF16), sh((batch, bw, seq), BF16),
                   sh((t, nidx), BF16), sh((t, nidx), BF16), sh((t, LANES), F32)],
        scratch_shapes=[pltpu.VMEM((aw // LANES, TM, LANES), F32)],
        compiler_params=_cparams(1),
        name="proj_even",
    )(h, g.reshape(1, d), w_all, cq_g.reshape(1, rank), wq_b.astype(BF16), wq_idx.astype(BF16))


def _stack_heads(q):
    lane_b = _lane_id_bf16(q.shape)
    z = jnp.zeros_like(q)
    return jnp.concatenate([jnp.where(lane_b < HEAD_DIM, q, z), jnp.where(lane_b >= HEAD_DIM, q, z)], axis=0)


def _band_block(q2, kw, vw, bias2):
    s = _dot_nt(q2, kw) + bias2
    m = jnp.max(s, axis=-1, keepdims=True)
    p = jnp.exp2(s - m)
    den = jnp.sum(p, axis=-1, keepdims=True)
    return _dot(p.astype(BF16), vw), m, den


def _attn_a_kernel(q1, k1, v1, q4, k4, v4, q16, k16, v16, bias_ref, o_ref, acc_s, m_s, den_s, *, seq):
    lo = lax.broadcasted_iota(I32, (BLOCK, LANES), 1) < HEAD_DIM
    shape = (BLOCK, LANES)
    refs = ((q1, k1, v1), (q4, k4, v4), (q16, k16, v16))
    for br, (_, dil) in enumerate(A_BRANCHES):
        qr, kr, vr = refs[br]
        for r in range(dil):
            at = (lambda ref, s, n: ref[0, pl.ds(s, n), :]) if dil == 1 else \
                 (lambda ref, s, n, r=r: ref[0, r, pl.ds(s, n), :])
            for n in range(seq // dil // BLOCK):
                q2 = _stack_heads(at(qr, n * BLOCK, BLOCK))
                if n == 0:
                    acc, m, den = _band_block(q2, at(kr, 0, BLOCK), at(vr, 0, BLOCK), bias_ref[br, 0, :, BLOCK:])
                else:
                    k0 = (n - 1) * BLOCK
                    acc, m, den = _band_block(q2, at(kr, k0, 2 * BLOCK), at(vr, k0, 2 * BLOCK), bias_ref[br, 0])
                rows = pl.ds(n * BLOCK, BLOCK) if dil == 1 else pl.ds(n * BLOCK * dil + r, BLOCK, stride=dil)
                acc_s[br, rows, :] = jnp.where(lo, acc[:BLOCK], acc[BLOCK:])
                m_s[br, rows, :] = jnp.where(lo, jnp.broadcast_to(m[:BLOCK], shape), jnp.broadcast_to(m[BLOCK:], shape))
                den_s[br, rows, :] = jnp.where(lo, jnp.broadcast_to(den[:BLOCK], shape),
                                               jnp.broadcast_to(den[BLOCK:], shape))

    chunk = 2 * BLOCK

    def merge(c, carry):
        rows = pl.ds(pl.multiple_of(c * chunk, chunk), chunk)
        ms = [m_s[br, rows, :] for br in range(3)]
        mx = jnp.maximum(jnp.maximum(ms[0], ms[1]), ms[2])
        num = jnp.zeros((chunk, LANES), F32)
        den = jnp.zeros((chunk, LANES), F32)
        for br in range(3):
            e = jnp.exp2(ms[br] - mx)
            num = num + e * acc_s[br, rows, :]
            den = den + e * den_s[br, rows, :]
        o_ref[0, rows, :] = (num / den).astype(BF16)
        return carry

    lax.fori_loop(0, seq // chunk, merge, 0)


def _attn_a(qkv, bias, batch, seq):
    q1, k1, v1, q4, k4, v4, q16, k16, v16 = qkv
    aw = q1.shape[-1]
    npair = aw // LANES
    nat = pl.BlockSpec((1, seq, LANES), lambda b, p: (b, 0, p))
    fold = lambda dil: pl.BlockSpec((1, dil, seq // dil, LANES), lambda b, p: (b, 0, 0, p))
    r3 = lambda a: a.reshape(batch, seq, aw)
    return pl.pallas_call(
        functools.partial(_attn_a_kernel, seq=seq),
        grid=(batch, npair),
        in_specs=[nat, nat, nat, fold(4), fold(4), fold(4), fold(16), fold(16), fold(16),
                  pl.BlockSpec((3, 1, 2 * BLOCK, 2 * BLOCK), lambda b, p: (0, p, 0, 0))],
        out_specs=nat,
        out_shape=jax.ShapeDtypeStruct((batch, seq, aw), BF16),
        scratch_shapes=[pltpu.VMEM((3, seq, LANES), F32)] * 3,
        compiler_params=_cparams(2),
        name="attn_dilated",
    )(r3(q1), r3(k1), r3(v1), q4, k4, v4, q16, k16, v16, bias)


def _attn_c_kernel(sink_ref, q_ref, k_ref, v_ref, bias_ref, o_ref, *, seq, group):
    g = pl.program_id(1)
    npair = group // 2
    lo = lax.broadcasted_iota(I32, (BLOCK, LANES), 1) < HEAD_DIM

    def block(q0, k0, width, bias2):
        qn = q_ref[0, pl.ds(q0, BLOCK), :]
        q2 = jnp.concatenate([_stack_heads(qn[:, p * LANES:(p + 1) * LANES]) for p in range(npair)], axis=0)
        acc, m, den = _band_block(q2, k_ref[0, pl.ds(k0, width), :], v_ref[0, pl.ds(k0, width), :], bias2)
        for p in range(npair):
            outs = []
            for hh in range(2):
                h = 2 * p + hh
                r = slice(h * BLOCK, (h + 1) * BLOCK)
                outs.append(acc[r] / (den[r] + jnp.exp2(sink_ref[g * group + h] * LOG2E - m[r])))
            o_ref[0, pl.ds(q0, BLOCK), p * LANES:(p + 1) * LANES] = jnp.where(lo, outs[0], outs[1]).astype(BF16)

    block(0, 0, BLOCK, bias_ref[0, :, BLOCK:])

    def body(n, carry):
        block(pl.multiple_of(n * BLOCK, BLOCK), pl.multiple_of((n - 1) * BLOCK, BLOCK), 2 * BLOCK, bias_ref[0])
        return carry

    lax.fori_loop(1, seq // BLOCK, body, 0, unroll=C_UNROLL)


def _attn_c(q, k2, v2, bias, sinks, batch, seq, group):
    qw = q.shape[-1]
    n_kv = k2.shape[-1] // LANES
    gw = qw // n_kv
    return pl.pallas_call(
        functools.partial(_attn_c_kernel, seq=seq, group=group),
        grid=(batch, n_kv),
        in_specs=[pl.BlockSpec(memory_space=pltpu.SMEM),
                  pl.BlockSpec((1, seq, gw), lambda b, g: (b, 0, g)),
                  pl.BlockSpec((1, seq, LANES), lambda b, g: (b, 0, g)),
                  pl.BlockSpec((1, seq, LANES), lambda b, g: (b, 0, g)),
                  pl.BlockSpec((1, group * BLOCK, 2 * BLOCK), lambda b, g: (g, 0, 0))],
        out_specs=pl.BlockSpec((1, seq, gw), lambda b, g: (b, 0, g)),
        out_shape=jax.ShapeDtypeStruct((batch, seq, qw), BF16),
        compiler_params=_cparams(2),
        name="attn_swa",
    )(sinks, q, k2, v2, bias)


class _Job:
    def __init__(self, i, nchs, slot, qs_ref, wt_ref, q2_ref, sk_ref, mask_ref):
        self.i, self.nchs = i, nchs
        self.qs, self.wt, self.q2 = qs_ref.at[slot], wt_ref.at[slot], q2_ref.at[slot]
        self.sk, self.mask = sk_ref.at[slot], mask_ref.at[slot]
        key_pos = lax.broadcasted_iota(I32, (CW, BLOCK), 0) + (nchs - 1) * CW
        qry_pos = lax.broadcasted_iota(I32, (CW, BLOCK), 1) + i * BLOCK
        self.causal_last = key_pos <= qry_pos


def _dsa_scores(job, ki_ref):
    for c in range(job.nchs):
        st = _dot_nt(ki_ref[0, c * CW:(c + 1) * CW, :], job.qs[...])
        score = None
        for h in range(IDX_HEADS):
            term = jnp.maximum(st[:, h * BLOCK:(h + 1) * BLOCK], 0.0) * job.wt[h:h + 1, :]
            score = term if score is None else score + term
        score = jnp.where(score == 0.0, 0.0, score)
        if c == job.nchs - 1:
            score = jnp.where(job.causal_last, score, -jnp.inf)
        bits = lax.bitcast_convert_type(score, I32)
        job.sk[c] = bits ^ (lax.shift_right_arithmetic(bits, 31) & 0x7FFFFFFF)


def _dsa_count(job, pred):
    grp = 8 * 8
    acc = jnp.zeros((grp, BLOCK), F32)
    for c in range(job.nchs):
        for g in range(CW // grp):
            acc = jnp.where(pred(job.sk[c, g * grp:(g + 1) * grp, :]), acc + 1.0, acc)
    return jnp.sum(acc, axis=0, keepdims=True)


def _dsa_select(job, thr, topk):
    n_gt = _dsa_count(job, lambda x: x > thr)
    n_eq = _dsa_count(job, lambda x: x == thr)
    need = float(topk) - n_gt
    for c in range(job.nchs):
        sel = job.sk[c] >= thr
        if c == job.nchs - 1:
            sel = sel & job.causal_last
        job.mask[c] = jnp.where(sel, 0.0, NEG)
    excess = jnp.where((n_eq > need) & (thr > KEY_NEG_INF), 1.0, 0.0)
    return need, jnp.max(excess)


def _dsa_break_ties(job, thr, need):
    r_i = lax.broadcasted_iota(I32, (BLOCK, BLOCK), 0)
    c_i = lax.broadcasted_iota(I32, (BLOCK, BLOCK), 1)
    lower = jnp.where(c_i <= r_i, 1.0, 0.0).astype(BF16)
    carry = jnp.zeros((1, BLOCK), F32)
    for c in range(job.nchs):
        for t in range(CW // BLOCK):
            rows = slice(t * BLOCK, (t + 1) * BLOCK)
            xs = job.sk[c, rows, :]
            eq = xs == thr
            eqf = jnp.where(eq, 1.0, 0.0)
            rank = _dot(lower, eqf.astype(BF16)) + carry
            keep = (xs > thr) | (eq & (rank <= need))
            causal = r_i + (c * CW + t * BLOCK) <= c_i + job.i * BLOCK
            job.mask[c, rows, :] = jnp.where(keep & causal, 0.0, NEG)
            carry = carry + jnp.sum(eqf, axis=0, keepdims=True)


def _dsa_attend(job, kb_ref, vt_ref, bias_ref, o_ref, npair):
    i, nchs, mask_ref, q2_ref = job.i, job.nchs, job.mask, job.q2
    tpc = CW // BLOCK
    out_rows = pl.ds(pl.multiple_of(i * BLOCK, BLOCK), BLOCK)
    first_half = lax.broadcasted_iota(I32, (LANES, BLOCK), 0) < HEAD_DIM
    for p in range(npair):
        lanes = slice(p * LANES, (p + 1) * LANES)
        logits = []
        for c in range(nchs):
            s = _dot_nt(kb_ref[0, c * CW:(c + 1) * CW, lanes], q2_ref[p])
            msk = mask_ref[c]
            per_head = []
            for hh in range(2):
                bias = jnp.concatenate(
                    [bias_ref[2 * p + hh, jnp.maximum(i - (c * tpc + t), 0)] for t in range(tpc)], axis=0)
                per_head.append(s[:, hh * BLOCK:(hh + 1) * BLOCK] + bias + msk)
            logits.append(per_head)
        out_t = jnp.zeros((LANES, 2 * BLOCK), F32)
        dens = []
        pes = [[None, None] for _ in range(nchs)]
        for hh in range(2):
            m = None
            for c in range(nchs):
                mc = jnp.max(logits[c][hh], axis=0, keepdims=True)
                m = mc if m is None else jnp.maximum(m, mc)
            den = jnp.zeros((1, BLOCK), F32)
            for c in range(nchs):
                pe = jnp.exp2(logits[c][hh] - m)
                den = den + jnp.sum(pe, axis=0, keepdims=True)
                pes[c][hh] = pe.astype(BF16)
            dens.append(den)
        for c in range(nchs):
            out_t = out_t + _dot(vt_ref[0, lanes, c * CW:(c + 1) * CW], jnp.concatenate(pes[c], axis=1))
        o_t = jnp.where(first_half, out_t[:, :BLOCK] / dens[0], out_t[:, BLOCK:] / dens[1])
        o_ref[0, out_rows, lanes] = o_t.T.astype(BF16)


def _dsa_kernel(qi_ref, wi_ref, ki_ref, qb_ref, kb_ref, vt_ref, bias_ref, o_ref,
                qs_ref, wt_ref, q2_ref, sk_ref, mask_ref, *, topk, n_heads, n_blocks):
    j = pl.program_id(1)
    npair = n_heads // 2
    tpc = CW // BLOCK
    n_chunks = n_blocks // tpc
    blocks = (j, n_blocks - 1 - j)

    for slot, i in enumerate(blocks):
        rows = pl.ds(pl.multiple_of(i * BLOCK, BLOCK), BLOCK)
        qi = qi_ref[0, rows, :]
        lane_i = _lane_id_bf16(qi.shape)
        for h in range(IDX_HEADS):
            in_head = (lane_i >= h * IDX_DIM) & (lane_i < (h + 1) * IDX_DIM)
            qs_ref[slot, h * BLOCK:(h + 1) * BLOCK, :] = jnp.where(in_head, qi, jnp.zeros_like(qi))
        wt_ref[slot] = wi_ref[0, rows, :].T
        for p in range(npair):
            q2_ref[slot, p] = _stack_heads(qb_ref[0, rows, p * LANES:(p + 1) * LANES])

    for v in range(n_chunks // 2):
        @pl.when(j // tpc == v)
        def _(v=v):
            jobs = [_Job(i, nchs, slot, qs_ref, wt_ref, q2_ref, sk_ref, mask_ref)
                    for slot, (i, nchs) in enumerate(zip(blocks, (v + 1, n_chunks - v)))]
            for job in jobs:
                _dsa_scores(job, ki_ref)

            def search(it, answers):
                bit = lax.shift_left(jnp.int32(1), 31 - it)
                out = []
                for job, ans in zip(jobs, answers):
                    cand_s = (ans | bit) ^ INT_MIN
                    n = _dsa_count(job, lambda x, c=cand_s: x >= c)
                    out.append(jnp.where(n >= float(topk), ans | bit, ans))
                return tuple(out)

            answers = lax.fori_loop(0, 32, search, tuple(jnp.zeros((1, BLOCK), I32) for _ in jobs))
            thrs = [ans ^ INT_MIN for ans in answers]
            ties = [_dsa_select(job, thr, topk) for job, thr in zip(jobs, thrs)]

            @pl.when(functools.reduce(jnp.maximum, [excess for _, excess in ties]) > 0.0)
            def _():
                for job, thr, (need, _) in zip(jobs, thrs, ties):
                    _dsa_break_ties(job, thr, need)

            for job in jobs:
                _dsa_attend(job, kb_ref, vt_ref, bias_ref, o_ref, npair)


def _dsa(qi, wi, ki, qb, kb, vt, bias, batch, seq, topk):
    bw = qb.shape[-1]
    nidx = qi.shape[-1]
    nb = seq // BLOCK
    n_heads = bw // HEAD_DIM
    npair = n_heads // 2
    assert nb % (2 * (CW // BLOCK)) == 0
    full = lambda width: pl.BlockSpec((1, seq, width), lambda b, j: (b, 0, 0))
    r3 = lambda a: a.reshape(batch, seq, a.shape[-1])
    vm = pltpu.VMEM
    out = pl.pallas_call(
        functools.partial(_dsa_kernel, topk=topk, n_heads=n_heads, n_blocks=nb),
        grid=(batch, nb // 2),
        in_specs=[full(nidx), full(LANES), full(nidx), full(bw), full(bw),
                  pl.BlockSpec((1, bw, seq), lambda b, j: (b, 0, 0)),
                  _const_spec((n_heads, nb, BLOCK, BLOCK))],
        out_specs=full(bw),
        out_shape=jax.ShapeDtypeStruct((batch, seq, bw), BF16),
        scratch_shapes=[vm((2, IDX_HEADS * BLOCK, nidx), BF16), vm((2, LANES, BLOCK), F32),
                        vm((2, npair, 2 * BLOCK, LANES), BF16),
                        vm((2, seq // CW, CW, BLOCK), I32), vm((2, seq // CW, CW, BLOCK), F32)],
        compiler_params=_cparams(2),
        name="dsa",
    )(r3(qi), r3(wi), r3(ki), r3(qb), r3(kb), vt, bias)
    return out.reshape(batch * seq, bw)


def _split_rows(w_out, parts):
    ws, r0 = [], 0
    for a in parts:
        ws.append(w_out[r0:r0 + a.shape[1]].astype(BF16))
        r0 += a.shape[1]
    return ws


def _odd_proj_weights(w_in, qw, kvw):
    d = w_in.shape[0]
    dup = lambda w: jnp.repeat(w.reshape(d, kvw // HEAD_DIM, 1, HEAD_DIM), 2, axis=2).reshape(d, 2 * kvw)
    return jnp.concatenate([w_in[:, :qw], dup(w_in[:, qw:qw + kvw]), dup(w_in[:, qw + kvw:])], axis=1).astype(BF16)


def kernel(x, norm_g, final_g, rel_bias_table, ffn_w1, ffn_w3, ffn_w2, hyb_w_in, hyb_cq_g, hyb_wq_b,
           hyb_wq_idx, hyb_w_out, swa_w_in, swa_sinks, swa_w_out):
    batch, seq, d = x.shape
    depth = norm_g.shape[0]
    n_heads = d // HEAD_DIM
    a_heads = n_heads // 2
    topk = min(TOPK_MAX, seq // 4)
    assert seq % (BLOCK * 16) == 0 and seq % CW == 0 and (batch * seq) % TM == 0 and seq % TM == 0

    bias_a = _build_bias(
        rel_bias_table,
        np.concatenate([_band_bucket_idx(dil, win // dil) for win, dil in A_BRANCHES], axis=0),
        0, a_heads).reshape(a_heads // 2, 2, 3, BLOCK, 2 * BLOCK).transpose(2, 0, 1, 3, 4)
    bias_a = bias_a.reshape(3, a_heads // 2, 2 * BLOCK, 2 * BLOCK)
    bias_b = _build_bias(rel_bias_table, _dense_bucket_idx(seq), a_heads, n_heads - a_heads)
    bias_b = bias_b.reshape(n_heads - a_heads, seq // BLOCK, BLOCK, BLOCK)
    bias_c = _build_bias(rel_bias_table, _band_bucket_idx(1, C_WINDOW - 1), 0, n_heads)

    h = x.reshape(batch * seq, d)
    for layer in range(depth):
        g = norm_g[layer]
        i = layer // 2
        ffn_a = (g[0], ffn_w1[layer, 0], ffn_w3[layer, 0], ffn_w2[layer, 0])
        ffn_b = (g[2], ffn_w1[layer, 1], ffn_w3[layer, 1], ffn_w2[layer, 1])
        if layer % 2 == 0:
            h = _ffn(h, *ffn_a, final_g)
            outs = _proj_even(h, g[1], hyb_w_in[i], hyb_cq_g[i], hyb_wq_b[i], hyb_wq_idx[i], batch, seq)
            o_a = _attn_a(outs[:9], bias_a, batch, seq)
            qb, kb, vt, qi, ki, wi = outs[9:]
            o_b = _dsa(qi, wi, ki, qb, kb, vt, bias_b, batch, seq, topk)
            mix, w_out = [o_a.reshape(batch * seq, -1), o_b], hyb_w_out[i]
        else:
            kvw = (swa_w_in.shape[-1] - d) // 2
            n_kv = kvw // HEAD_DIM
            group = n_heads // n_kv
            h, q, k2, v2 = _ffn(h, *ffn_a, g[1], post_w=_odd_proj_weights(swa_w_in[i], d, kvw),
                                post_splits=((d, Q_SCALE), (2 * kvw, 1.0), (2 * kvw, 1.0)))
            r3 = lambda a: a.reshape(batch, seq, a.shape[-1])
            o_c = _attn_c(r3(q), r3(k2), r3(v2), bias_c.reshape(n_kv, group * BLOCK, 2 * BLOCK),
                          swa_sinks[i].astype(F32), batch, seq, group)
            mix, w_out = [o_c.reshape(batch * seq, -1)], swa_w_out[i]
        h = _ffn(h, *ffn_b, final_g, mix=mix, mix_w=_split_rows(w_out, mix), final=layer == depth - 1)
    return h.reshape(batch, seq, d)
```

```python
import functools
import math

import numpy as np
import jax
import jax.numpy as jnp
from jax import lax
from jax.experimental import pallas as pl
from jax.experimental.pallas import tpu as pltpu

F32 = jnp.float32
BF16 = jnp.bfloat16
I32 = jnp.int32

EPS = 1e-6
HEAD_DIM = 64
BLOCK = 128
LANES = 128
NUM_BUCKETS = 32
MAX_DISTANCE = 2048
A_BRANCHES = ((128, 1), (512, 4), (2048, 16))
C_WINDOW = 128
TOPK_MAX = 256
IDX_HEADS = 8
IDX_DIM = 32
NEG = -1e30
LOG2E = math.log2(math.e)
Q_SCALE = HEAD_DIM ** -0.5 * LOG2E
INT_MIN = -(2 ** 31)
KEY_NEG_INF = (0xFF800000 - (1 << 32)) ^ 0x7FFFFFFF
VMEM_LIMIT = 56 * 1024 * 1024

TM = 512
CW = 2 * BLOCK
C_UNROLL = 5


def _cparams(n_axes):
    return pltpu.CompilerParams(
        dimension_semantics=("arbitrary",) * n_axes, vmem_limit_bytes=VMEM_LIMIT)


def _rms(x, g):
    return x * lax.rsqrt(jnp.mean(x * x, axis=-1, keepdims=True) + EPS) * g


def _dot(a, b):
    return jnp.dot(a, b, preferred_element_type=F32)


def _dot_nt(a, b):
    return lax.dot_general(a, b, (((1,), (1,)), ((), ())), preferred_element_type=F32)


def _lane_id_bf16(shape):
    return lax.broadcasted_iota(I32, shape, 1).astype(F32).astype(BF16)


def _rows(start, size):
    if isinstance(start, int):
        return pl.ds(start, size)
    return pl.ds(pl.multiple_of(start, size), size)


def _const_spec(shape):
    nd = len(shape)
    return pl.BlockSpec(shape, lambda *_: (0,) * nd, pipeline_mode=pl.Buffered(1))


def _t5_bucket_np(dist):
    dist = np.maximum(dist, 0)
    max_exact = NUM_BUCKETS // 2
    d_f = np.maximum(dist, 1).astype(np.float32)
    large = max_exact + (np.log(d_f / max_exact) / math.log(MAX_DISTANCE / max_exact)
                         * (NUM_BUCKETS - max_exact)).astype(np.int32)
    large = np.minimum(large, NUM_BUCKETS - 1)
    return np.where(dist < max_exact, dist, large).astype(np.int32)


def _band_bucket_idx(dil, max_dist):
    loc = np.arange(BLOCK)[:, None] + BLOCK - np.arange(2 * BLOCK)[None, :]
    valid = (loc >= 0) & (loc <= max_dist)
    return np.where(valid, _t5_bucket_np(dil * loc), -1).astype(np.int32)


def _dense_bucket_idx(seq):
    nb = seq // BLOCK
    d = np.arange(nb)[:, None, None] * BLOCK - np.arange(BLOCK)[None, :, None] + np.arange(BLOCK)[None, None, :]
    return np.where(d >= 0, _t5_bucket_np(d), -1).astype(np.int32).reshape(seq, BLOCK)


def _bias_kernel(tab_ref, idx_ref, o_ref, *, head0):
    h = pl.program_id(0) + head0
    idx = idx_ref[...]
    out = jnp.full(idx.shape, NEG, F32)
    for b in range(NUM_BUCKETS):
        out = jnp.where(idx == b, tab_ref[b, h] * LOG2E, out)
    o_ref[0] = out


def _build_bias(table, idx_np, head0, n_heads):
    r, c = idx_np.shape
    return pl.pallas_call(
        functools.partial(_bias_kernel, head0=head0),
        grid=(n_heads,),
        in_specs=[pl.BlockSpec(memory_space=pltpu.SMEM),
                  pl.BlockSpec((r, c), lambda h: (0, 0))],
        out_specs=pl.BlockSpec((1, r, c), lambda h: (h, 0, 0)),
        out_shape=jax.ShapeDtypeStruct((n_heads, r, c), F32),
        compiler_params=_cparams(1),
        name="bias_build",
    )(table.astype(F32), jnp.asarray(idx_np))


def _ffn_kernel(*refs, n_mix, final, post_splits):
    it = iter(refs)
    x_ref = next(it)
    mix = [next(it) for _ in range(n_mix)]
    mix_w = [next(it) for _ in range(n_mix)]
    g_ref, w1_ref, w3_ref, w2_ref, g2_ref = (next(it) for _ in range(5))
    wp_ref = next(it) if post_splits else None
    o_ref = next(it)
    post_refs = [next(it) for _ in post_splits]
    x = x_ref[...]
    for a_ref, w_ref in zip(mix, mix_w):
        x = x + _dot(a_ref[...], w_ref[...])
    xn = _rms(x, g_ref[...]).astype(BF16)
    h1 = _dot(xn, w1_ref[...])
    h3 = _dot(xn, w3_ref[...])
    act = (h1 * (1.0 / (1.0 + jnp.exp(-h1)))) * h3
    y = _dot(act.astype(BF16), w2_ref[...])
    out = x + 0.5 * y
    if final:
        out = _rms(out, g2_ref[...])
    o_ref[...] = out
    if post_splits:
        p = _dot(_rms(out, g2_ref[...]).astype(BF16), wp_ref[...])
        c0 = 0
        for p_ref, (width, scale) in zip(post_refs, post_splits):
            blk = p[:, c0:c0 + width]
            p_ref[...] = (blk if scale == 1.0 else blk * scale).astype(BF16)
            c0 += width


def _ffn(h, g, ffn_w, which, g2, *, mix=(), mix_w=(), final=False, post_w=None, post_splits=()):
    t, d = h.shape
    dff = ffn_w[0].shape[-1]
    tok = lambda width: pl.BlockSpec((TM, width), lambda i: (i, 0))
    pick = lambda rows, cols: pl.BlockSpec((None, None, rows, cols), lambda i: (*which, 0, 0),
                                           pipeline_mode=pl.Buffered(1))
    sh = jax.ShapeDtypeStruct
    in_specs = [tok(d)] + [tok(a.shape[1]) for a in mix] + [_const_spec(w.shape) for w in mix_w]
    in_specs += [_const_spec((1, d)), pick(d, dff), pick(d, dff), pick(dff, d), _const_spec((1, d))]
    args = [h, *mix, *mix_w, g.reshape(1, d), *ffn_w, g2.reshape(1, d)]
    out_specs, out_shape = [tok(d)], [sh((t, d), F32)]
    if post_splits:
        in_specs.append(_const_spec(post_w.shape))
        args.append(post_w)
        out_specs += [tok(width) for width, _ in post_splits]
        out_shape += [sh((t, width), BF16) for width, _ in post_splits]
    res = pl.pallas_call(
        functools.partial(_ffn_kernel, n_mix=len(mix), final=final, post_splits=tuple(post_splits)),
        grid=(t // TM,),
        in_specs=in_specs,
        out_specs=out_specs,
        out_shape=out_shape,
        compiler_params=_cparams(1),
        name="ffn",
    )(*args)
    return res if post_splits else res[0]


def _fold_store(scr, scr4, blk, outs, n_slab):
    o1, o4, o16 = outs
    o1[...] = blk.astype(BF16)
    for c in range(n_slab):
        scr[c] = blk[:, c * LANES:(c + 1) * LANES]
    rows4, rows16 = TM // 4, TM // 16
    for r4 in range(4):
        for c in range(n_slab):
            cols = slice(c * LANES, (c + 1) * LANES)
            f4 = scr[c, pl.ds(r4, rows4, stride=4), :]
            o4[0, r4, :, cols] = f4.astype(BF16)
            scr4[c] = f4
            for q in range(4):
                o16[0, 4 * q + r4, :, cols] = scr4[c, pl.ds(q, rows16, stride=4), :].astype(BF16)


def _proj_even_kernel(h_ref, g_ref, w_ref, cqg_ref, wqb_ref, wqi_ref,
                      q1, k1, v1, q4, k4, v4, q16, k16, v16, qb, kb, vb, qi, ki, wi, scr, scr4,
                      *, aw, bw, rank):
    u = _rms(h_ref[...], g_ref[...]).astype(BF16)
    p = _dot(u, w_ref[...])
    proj = lambda c0, width: p[:, c0:c0 + width]
    n_slab = aw // LANES
    scale = Q_SCALE
    _fold_store(scr, scr4, proj(0, aw) * scale, (q1, q4, q16), n_slab)
    _fold_store(scr, scr4, proj(aw, aw), (k1, k4, k16), n_slab)
    _fold_store(scr, scr4, proj(2 * aw, aw), (v1, v4, v16), n_slab)
    c0 = 3 * aw
    kb[...] = proj(c0, bw).astype(BF16)
    vb[0] = proj(c0 + bw, bw).T.astype(BF16)
    c1 = c0 + 2 * bw
    cq = _rms(proj(c1, rank), cqg_ref[...]).astype(BF16)
    qb[...] = (_dot(cq, wqb_ref[...]) * scale).astype(BF16)
    qi[...] = _dot(cq, wqi_ref[...]).astype(BF16)
    c2 = c1 + rank
    nidx = IDX_HEADS * IDX_DIM
    ki[...] = proj(c2, nidx).astype(BF16)
    idx_scale = (IDX_DIM ** -0.5) * (IDX_HEADS ** -0.5)
    wi[...] = proj(c2 + nidx, LANES) * idx_scale


def _proj_even(h, g, w_in, cq_g, wq_b, wq_idx, batch, seq):
    t, d = h.shape
    bw = wq_b.shape[1]
    rank = wq_b.shape[0]
    aw = (w_in.shape[1] - rank - 2 * bw - IDX_DIM - IDX_HEADS) // 3
    nidx = IDX_HEADS * IDX_DIM
    offs = np.cumsum([0, aw, aw, aw, rank, bw, bw, IDX_DIM, IDX_HEADS])
    col = lambda i: w_in[:, offs[i]:offs[i + 1]]
    w_kidx = jnp.tile(col(6), (1, IDX_HEADS))
    w_widx = jnp.pad(col(7), ((0, 0), (0, LANES - IDX_HEADS)))
    w_all = jnp.concatenate([col(0), col(1), col(2), col(4), col(5), col(3), w_kidx, w_widx], axis=1).astype(BF16)
    ncol = w_all.shape[1]
    spt = seq // TM
    tok = lambda width: pl.BlockSpec((TM, width), lambda i: (i, 0))
    fold = lambda dil: pl.BlockSpec((1, dil, TM // dil, aw), lambda i: (i // spt, 0, i % spt, 0))
    sh = jax.ShapeDtypeStruct
    nat = sh((t, aw), BF16)
    f4 = sh((batch, 4, seq // 4, aw), BF16)
    f16 = sh((batch, 16, seq // 16, aw), BF16)
    return pl.pallas_call(
        functools.partial(_proj_even_kernel, aw=aw, bw=bw, rank=rank),
        grid=(t // TM,),
        in_specs=[tok(d), _const_spec((1, d)), _const_spec((d, ncol)), _const_spec((1, rank)),
                  _const_spec((rank, bw)), _const_spec((rank, nidx))],
        out_specs=[tok(aw), tok(aw), tok(aw), fold(4), fold(4), fold(4), fold(16), fold(16), fold(16),
                   tok(bw), tok(bw), pl.BlockSpec((1, bw, TM), lambda i: (i // spt, 0, i % spt)),
                   tok(nidx), tok(nidx), tok(LANES)],
        out_shape=[nat, nat, nat, f4, f4, f4, f16, f16, f16,
                   sh((t, bw), BF16), sh((t, bw), BF16), sh((batch, bw, seq), BF16),
                   sh((t, nidx), BF16), sh((t, nidx), BF16), sh((t, LANES), F32)],
        scratch_shapes=[pltpu.VMEM((aw // LANES, TM, LANES), F32), pltpu.VMEM((aw // LANES, TM // 4, LANES), F32)],
        compiler_params=_cparams(1),
        name="proj_even",
    )(h, g.reshape(1, d), w_all, cq_g.reshape(1, rank), wq_b.astype(BF16), wq_idx.astype(BF16))


def _stack_heads(q):
    lane_b = _lane_id_bf16(q.shape)
    z = jnp.zeros_like(q)
    return jnp.concatenate([jnp.where(lane_b < HEAD_DIM, q, z), jnp.where(lane_b >= HEAD_DIM, q, z)], axis=0)


def _band_block(q2, kw, vw, bias2):
    s = _dot_nt(q2, kw) + bias2
    m = jnp.max(s, axis=-1, keepdims=True)
    p = jnp.exp2(s - m)
    den = jnp.sum(p, axis=-1, keepdims=True)
    return _dot(p.astype(BF16), vw), m, den


def _attn_a_kernel(q1, k1, v1, q4, k4, v4, q16, k16, v16, bias_ref, o_ref, acc_s, m_s, den_s, *, seq):
    lo = lax.broadcasted_iota(I32, (BLOCK, LANES), 1) < HEAD_DIM
    shape = (BLOCK, LANES)
    refs = ((q1, k1, v1), (q4, k4, v4), (q16, k16, v16))
    for br, (_, dil) in enumerate(A_BRANCHES):
        qr, kr, vr = refs[br]
        for r in range(dil):
            at = (lambda ref, s, n: ref[0, pl.ds(s, n), :]) if dil == 1 else \
                 (lambda ref, s, n, r=r: ref[0, r, pl.ds(s, n), :])
            for n in range(seq // dil // BLOCK):
                q2 = _stack_heads(at(qr, n * BLOCK, BLOCK))
                if n == 0:
                    acc, m, den = _band_block(q2, at(kr, 0, BLOCK), at(vr, 0, BLOCK), bias_ref[br, 0, :, BLOCK:])
                else:
                    k0 = (n - 1) * BLOCK
                    acc, m, den = _band_block(q2, at(kr, k0, 2 * BLOCK), at(vr, k0, 2 * BLOCK), bias_ref[br, 0])
                rows = pl.ds(n * BLOCK, BLOCK) if dil == 1 else pl.ds(n * BLOCK * dil + r, BLOCK, stride=dil)
                acc_s[br, rows, :] = jnp.where(lo, acc[:BLOCK], acc[BLOCK:])
                m_s[br, rows, :] = jnp.where(lo, jnp.broadcast_to(m[:BLOCK], shape), jnp.broadcast_to(m[BLOCK:], shape))
                den_s[br, rows, :] = jnp.where(lo, jnp.broadcast_to(den[:BLOCK], shape),
                                               jnp.broadcast_to(den[BLOCK:], shape))

    chunk = 2 * BLOCK

    def merge(c, carry):
        rows = pl.ds(pl.multiple_of(c * chunk, chunk), chunk)
        ms = [m_s[br, rows, :] for br in range(3)]
        mx = jnp.maximum(jnp.maximum(ms[0], ms[1]), ms[2])
        num = jnp.zeros((chunk, LANES), F32)
        den = jnp.zeros((chunk, LANES), F32)
        for br in range(3):
            e = jnp.exp2(ms[br] - mx)
            num = num + e * acc_s[br, rows, :]
            den = den + e * den_s[br, rows, :]
        o_ref[0, rows, :] = (num / den).astype(BF16)
        return carry

    lax.fori_loop(0, seq // chunk, merge, 0)


def _attn_a(qkv, bias, batch, seq):
    q1, k1, v1, q4, k4, v4, q16, k16, v16 = qkv
    aw = q1.shape[-1]
    npair = aw // LANES
    nat = pl.BlockSpec((1, seq, LANES), lambda b, p: (b, 0, p))
    fold = lambda dil: pl.BlockSpec((1, dil, seq // dil, LANES), lambda b, p: (b, 0, 0, p))
    r3 = lambda a: a.reshape(batch, seq, aw)
    return pl.pallas_call(
        functools.partial(_attn_a_kernel, seq=seq),
        grid=(batch, npair),
        in_specs=[nat, nat, nat, fold(4), fold(4), fold(4), fold(16), fold(16), fold(16),
                  pl.BlockSpec((3, 1, 2 * BLOCK, 2 * BLOCK), lambda b, p: (0, p, 0, 0))],
        out_specs=nat,
        out_shape=jax.ShapeDtypeStruct((batch, seq, aw), BF16),
        scratch_shapes=[pltpu.VMEM((3, seq, LANES), F32)] * 3,
        compiler_params=_cparams(2),
        name="attn_dilated",
    )(r3(q1), r3(k1), r3(v1), q4, k4, v4, q16, k16, v16, bias)


def _attn_c_kernel(sink_ref, q_ref, k_ref, v_ref, bias_ref, o_ref, *, seq, group):
    g = pl.program_id(1)
    npair = group // 2
    lo = lax.broadcasted_iota(I32, (BLOCK, LANES), 1) < HEAD_DIM

    def block(q0, k0, width, bias2):
        qn = q_ref[0, pl.ds(q0, BLOCK), :]
        q2 = jnp.concatenate([_stack_heads(qn[:, p * LANES:(p + 1) * LANES]) for p in range(npair)], axis=0)
        acc, m, den = _band_block(q2, k_ref[0, pl.ds(k0, width), :], v_ref[0, pl.ds(k0, width), :], bias2)
        for p in range(npair):
            outs = []
            for hh in range(2):
                h = 2 * p + hh
                r = slice(h * BLOCK, (h + 1) * BLOCK)
                outs.append(acc[r] / (den[r] + jnp.exp2(sink_ref[g * group + h] * LOG2E - m[r])))
            o_ref[0, pl.ds(q0, BLOCK), p * LANES:(p + 1) * LANES] = jnp.where(lo, outs[0], outs[1]).astype(BF16)

    block(0, 0, BLOCK, bias_ref[0, :, BLOCK:])

    def body(n, carry):
        block(pl.multiple_of(n * BLOCK, BLOCK), pl.multiple_of((n - 1) * BLOCK, BLOCK), 2 * BLOCK, bias_ref[0])
        return carry

    lax.fori_loop(1, seq // BLOCK, body, 0, unroll=C_UNROLL)


def _attn_c(q, k2, v2, bias, sinks, batch, seq, group):
    qw = q.shape[-1]
    n_kv = k2.shape[-1] // LANES
    gw = qw // n_kv
    return pl.pallas_call(
        functools.partial(_attn_c_kernel, seq=seq, group=group),
        grid=(batch, n_kv),
        in_specs=[pl.BlockSpec(memory_space=pltpu.SMEM),
                  pl.BlockSpec((1, seq, gw), lambda b, g: (b, 0, g)),
                  pl.BlockSpec((1, seq, LANES), lambda b, g: (b, 0, g)),
                  pl.BlockSpec((1, seq, LANES), lambda b, g: (b, 0, g)),
                  pl.BlockSpec((1, group * BLOCK, 2 * BLOCK), lambda b, g: (g, 0, 0))],
        out_specs=pl.BlockSpec((1, seq, gw), lambda b, g: (b, 0, g)),
        out_shape=jax.ShapeDtypeStruct((batch, seq, qw), BF16),
        compiler_params=_cparams(2),
        name="attn_swa",
    )(sinks, q, k2, v2, bias)


class _Job:
    def __init__(self, i, nchs, slot, qs_ref, wt_ref, q2_ref, sk_ref, mask_ref):
        self.i, self.nchs = i, nchs
        self.qs, self.wt, self.q2 = qs_ref.at[slot], wt_ref.at[slot], q2_ref.at[slot]
        self.sk, self.mask = sk_ref.at[slot], mask_ref.at[slot]
        key_pos = lax.broadcasted_iota(I32, (CW, BLOCK), 0) + (nchs - 1) * CW
        qry_pos = lax.broadcasted_iota(I32, (CW, BLOCK), 1) + i * BLOCK
        self.causal_last = key_pos <= qry_pos


def _dsa_scores(job, ki_ref):
    for c in range(job.nchs):
        st = _dot_nt(ki_ref[0, c * CW:(c + 1) * CW, :], job.qs[...])
        score = None
        for h in range(IDX_HEADS):
            term = jnp.maximum(st[:, h * BLOCK:(h + 1) * BLOCK], 0.0) * job.wt[h:h + 1, :]
            score = term if score is None else score + term
        score = jnp.where(score == 0.0, 0.0, score)
        if c == job.nchs - 1:
            score = jnp.where(job.causal_last, score, -jnp.inf)
        bits = lax.bitcast_convert_type(score, I32)
        job.sk[c] = bits ^ (lax.shift_right_arithmetic(bits, 31) & 0x7FFFFFFF)


def _dsa_count(job, pred):
    grp = 8 * 8
    acc = jnp.zeros((grp, BLOCK), F32)
    for c in range(job.nchs):
        for g in range(CW // grp):
            acc = jnp.where(pred(job.sk[c, g * grp:(g + 1) * grp, :]), acc + 1.0, acc)
    return jnp.sum(acc, axis=0, keepdims=True)


def _dsa_select(job, thr, topk):
    n_gt = _dsa_count(job, lambda x: x > thr)
    n_eq = _dsa_count(job, lambda x: x == thr)
    need = float(topk) - n_gt
    for c in range(job.nchs):
        sel = job.sk[c] >= thr
        if c == job.nchs - 1:
            sel = sel & job.causal_last
        job.mask[c] = jnp.where(sel, 0.0, NEG)
    excess = jnp.where((n_eq > need) & (thr > KEY_NEG_INF), 1.0, 0.0)
    return need, jnp.max(excess)


def _dsa_break_ties(job, thr, need):
    r_i = lax.broadcasted_iota(I32, (BLOCK, BLOCK), 0)
    c_i = lax.broadcasted_iota(I32, (BLOCK, BLOCK), 1)
    lower = jnp.where(c_i <= r_i, 1.0, 0.0).astype(BF16)
    carry = jnp.zeros((1, BLOCK), F32)
    for c in range(job.nchs):
        for t in range(CW // BLOCK):
            rows = slice(t * BLOCK, (t + 1) * BLOCK)
            xs = job.sk[c, rows, :]
            eq = xs == thr
            eqf = jnp.where(eq, 1.0, 0.0)
            rank = _dot(lower, eqf.astype(BF16)) + carry
            keep = (xs > thr) | (eq & (rank <= need))
            causal = r_i + (c * CW + t * BLOCK) <= c_i + job.i * BLOCK
            job.mask[c, rows, :] = jnp.where(keep & causal, 0.0, NEG)
            carry = carry + jnp.sum(eqf, axis=0, keepdims=True)


def _dsa_attend(job, kb_ref, vt_ref, bias_ref, o_ref, npair):
    i, nchs, mask_ref, q2_ref = job.i, job.nchs, job.mask, job.q2
    tpc = CW // BLOCK
    out_rows = pl.ds(pl.multiple_of(i * BLOCK, BLOCK), BLOCK)
    first_half = lax.broadcasted_iota(I32, (LANES, BLOCK), 0) < HEAD_DIM
    for p in range(npair):
        lanes = slice(p * LANES, (p + 1) * LANES)
        logits = []
        for c in range(nchs):
            s = _dot_nt(kb_ref[0, c * CW:(c + 1) * CW, lanes], q2_ref[p])
            msk = mask_ref[c]
            per_head = []
            for hh in range(2):
                bias = jnp.concatenate(
                    [bias_ref[2 * p + hh, jnp.maximum(i - (c * tpc + t), 0)] for t in range(tpc)], axis=0)
                per_head.append(s[:, hh * BLOCK:(hh + 1) * BLOCK] + bias + msk)
            logits.append(per_head)
        out_t = jnp.zeros((LANES, 2 * BLOCK), F32)
        dens = []
        pes = [[None, None] for _ in range(nchs)]
        for hh in range(2):
            m = None
            for c in range(nchs):
                mc = jnp.max(logits[c][hh], axis=0, keepdims=True)
                m = mc if m is None else jnp.maximum(m, mc)
            den = jnp.zeros((1, BLOCK), F32)
            for c in range(nchs):
                pe = jnp.exp2(logits[c][hh] - m)
                den = den + jnp.sum(pe, axis=0, keepdims=True)
                pes[c][hh] = pe.astype(BF16)
            dens.append(den)
        for c in range(nchs):
            out_t = out_t + _dot(vt_ref[0, lanes, c * CW:(c + 1) * CW], jnp.concatenate(pes[c], axis=1))
        o_t = jnp.where(first_half, out_t[:, :BLOCK] / dens[0], out_t[:, BLOCK:] / dens[1])
        o_ref[0, out_rows, lanes] = o_t.T.astype(BF16)


def _dsa_kernel(qi_ref, wi_ref, ki_ref, qb_ref, kb_ref, vt_ref, bias_ref, o_ref,
                qs_ref, wt_ref, q2_ref, sk_ref, mask_ref, *, topk, n_heads, n_blocks):
    j = pl.program_id(1)
    npair = n_heads // 2
    tpc = CW // BLOCK
    n_chunks = n_blocks // tpc
    blocks = (j, n_blocks - 1 - j)

    for slot, i in enumerate(blocks):
        rows = pl.ds(pl.multiple_of(i * BLOCK, BLOCK), BLOCK)
        qi = qi_ref[0, rows, :]
        lane_i = _lane_id_bf16(qi.shape)
        for h in range(IDX_HEADS):
            in_head = (lane_i >= h * IDX_DIM) & (lane_i < (h + 1) * IDX_DIM)
            qs_ref[slot, h * BLOCK:(h + 1) * BLOCK, :] = jnp.where(in_head, qi, jnp.zeros_like(qi))
        wt_ref[slot] = wi_ref[0, rows, :].T
        for p in range(npair):
            q2_ref[slot, p] = _stack_heads(qb_ref[0, rows, p * LANES:(p + 1) * LANES])

    for v in range(n_chunks // 2):
        @pl.when(j // tpc == v)
        def _(v=v):
            jobs = [_Job(i, nchs, slot, qs_ref, wt_ref, q2_ref, sk_ref, mask_ref)
                    for slot, (i, nchs) in enumerate(zip(blocks, (v + 1, n_chunks - v)))]
            for job in jobs:
                _dsa_scores(job, ki_ref)

            def search(it, answers):
                bit = lax.shift_left(jnp.int32(1), 31 - it)
                out = []
                for job, ans in zip(jobs, answers):
                    cand_s = (ans | bit) ^ INT_MIN
                    n = _dsa_count(job, lambda x, c=cand_s: x >= c)
                    out.append(jnp.where(n >= float(topk), ans | bit, ans))
                return tuple(out)

            answers = lax.fori_loop(0, 32, search, tuple(jnp.zeros((1, BLOCK), I32) for _ in jobs))
            thrs = [ans ^ INT_MIN for ans in answers]
            ties = [_dsa_select(job, thr, topk) for job, thr in zip(jobs, thrs)]

            @pl.when(functools.reduce(jnp.maximum, [excess for _, excess in ties]) > 0.0)
            def _():
                for job, thr, (need, _) in zip(jobs, thrs, ties):
                    _dsa_break_ties(job, thr, need)

            for job in jobs:
                _dsa_attend(job, kb_ref, vt_ref, bias_ref, o_ref, npair)


def _dsa(qi, wi, ki, qb, kb, vt, bias, batch, seq, topk):
    bw = qb.shape[-1]
    nidx = qi.shape[-1]
    nb = seq // BLOCK
    n_heads = bw // HEAD_DIM
    npair = n_heads // 2
    assert nb % (2 * (CW // BLOCK)) == 0
    full = lambda width: pl.BlockSpec((1, seq, width), lambda b, j: (b, 0, 0))
    r3 = lambda a: a.reshape(batch, seq, a.shape[-1])
    vm = pltpu.VMEM
    out = pl.pallas_call(
        functools.partial(_dsa_kernel, topk=topk, n_heads=n_heads, n_blocks=nb),
        grid=(batch, nb // 2),
        in_specs=[full(nidx), full(LANES), full(nidx), full(bw), full(bw),
                  pl.BlockSpec((1, bw, seq), lambda b, j: (b, 0, 0)),
                  _const_spec((n_heads, nb, BLOCK, BLOCK))],
        out_specs=full(bw),
        out_shape=jax.ShapeDtypeStruct((batch, seq, bw), BF16),
        scratch_shapes=[vm((2, IDX_HEADS * BLOCK, nidx), BF16), vm((2, LANES, BLOCK), F32),
                        vm((2, npair, 2 * BLOCK, LANES), BF16),
                        vm((2, seq // CW, CW, BLOCK), I32), vm((2, seq // CW, CW, BLOCK), F32)],
        compiler_params=_cparams(2),
        name="dsa",
    )(r3(qi), r3(wi), r3(ki), r3(qb), r3(kb), vt, bias)
    return out.reshape(batch * seq, bw)


def _split_rows(w_out, parts):
    ws, r0 = [], 0
    for a in parts:
        ws.append(w_out[r0:r0 + a.shape[1]].astype(BF16))
        r0 += a.shape[1]
    return ws


def _odd_proj_weights(w_in, qw, kvw):
    d = w_in.shape[0]
    dup = lambda w: jnp.repeat(w.reshape(d, kvw // HEAD_DIM, 1, HEAD_DIM), 2, axis=2).reshape(d, 2 * kvw)
    return jnp.concatenate([w_in[:, :qw], dup(w_in[:, qw:qw + kvw]), dup(w_in[:, qw + kvw:])], axis=1).astype(BF16)


def kernel(x, norm_g, final_g, rel_bias_table, ffn_w1, ffn_w3, ffn_w2, hyb_w_in, hyb_cq_g, hyb_wq_b,
           hyb_wq_idx, hyb_w_out, swa_w_in, swa_sinks, swa_w_out):
    batch, seq, d = x.shape
    depth = norm_g.shape[0]
    n_heads = d // HEAD_DIM
    a_heads = n_heads // 2
    topk = min(TOPK_MAX, seq // 4)
    assert seq % (BLOCK * 16) == 0 and seq % CW == 0 and (batch * seq) % TM == 0 and seq % TM == 0

    bias_a = _build_bias(
        rel_bias_table,
        np.concatenate([_band_bucket_idx(dil, win // dil) for win, dil in A_BRANCHES], axis=0),
        0, a_heads).reshape(a_heads // 2, 2, 3, BLOCK, 2 * BLOCK).transpose(2, 0, 1, 3, 4)
    bias_a = bias_a.reshape(3, a_heads // 2, 2 * BLOCK, 2 * BLOCK)
    bias_b = _build_bias(rel_bias_table, _dense_bucket_idx(seq), a_heads, n_heads - a_heads)
    bias_b = bias_b.reshape(n_heads - a_heads, seq // BLOCK, BLOCK, BLOCK)
    bias_c = _build_bias(rel_bias_table, _band_bucket_idx(1, C_WINDOW - 1), 0, n_heads)

    ffn_w = (ffn_w1.astype(BF16), ffn_w3.astype(BF16), ffn_w2.astype(BF16))
    h = x.reshape(batch * seq, d)
    for layer in range(depth):
        g = norm_g[layer]
        i = layer // 2
        ffn_a = (g[0], ffn_w, (layer, 0))
        ffn_b = (g[2], ffn_w, (layer, 1))
        if layer % 2 == 0:
            h = _ffn(h, *ffn_a, final_g)
            outs = _proj_even(h, g[1], hyb_w_in[i], hyb_cq_g[i], hyb_wq_b[i], hyb_wq_idx[i], batch, seq)
            o_a = _attn_a(outs[:9], bias_a, batch, seq)
            qb, kb, vt, qi, ki, wi = outs[9:]
            o_b = _dsa(qi, wi, ki, qb, kb, vt, bias_b, batch, seq, topk)
            mix, w_out = [o_a.reshape(batch * seq, -1), o_b], hyb_w_out[i]
        else:
            kvw = (swa_w_in.shape[-1] - d) // 2
            n_kv = kvw // HEAD_DIM
            group = n_heads // n_kv
            h, q, k2, v2 = _ffn(h, *ffn_a, g[1], post_w=_odd_proj_weights(swa_w_in[i], d, kvw),
                                post_splits=((d, Q_SCALE), (2 * kvw, 1.0), (2 * kvw, 1.0)))
            r3 = lambda a: a.reshape(batch, seq, a.shape[-1])
            o_c = _attn_c(r3(q), r3(k2), r3(v2), bias_c.reshape(n_kv, group * BLOCK, 2 * BLOCK),
                          swa_sinks[i].astype(F32), batch, seq, group)
            mix, w_out = [o_c.reshape(batch * seq, -1)], swa_w_out[i]
        h = _ffn(h, *ffn_b, final_g, mix=mix, mix_w=_split_rows(w_out, mix), final=layer == depth - 1)
    return h.reshape(batch, seq, d)
```

```python
import functools
import math

import numpy as np
import jax
import jax.numpy as jnp
from jax import lax
from jax.experimental import pallas as pl
from jax.experimental.pallas import tpu as pltpu

F32 = jnp.float32
BF16 = jnp.bfloat16
I32 = jnp.int32

EPS = 1e-6
HEAD_DIM = 64
BLOCK = 128
LANES = 128
NUM_BUCKETS = 32
MAX_DISTANCE = 2048
A_BRANCHES = ((128, 1), (512, 4), (2048, 16))
C_WINDOW = 128
TOPK_MAX = 256
IDX_HEADS = 8
IDX_DIM = 32
NEG = -1e30
LOG2E = math.log2(math.e)
Q_SCALE = HEAD_DIM ** -0.5 * LOG2E
INT_MIN = -(2 ** 31)
KEY_NEG_INF = (0xFF800000 - (1 << 32)) ^ 0x7FFFFFFF
VMEM_LIMIT = 56 * 1024 * 1024

TM = 1024
TM_FFN = 1024
CW = 2 * BLOCK
C_UNROLL = 5


def _cparams(n_axes):
    return pltpu.CompilerParams(
        dimension_semantics=("arbitrary",) * n_axes, vmem_limit_bytes=VMEM_LIMIT)


def _rms(x, g):
    return x * lax.rsqrt(jnp.mean(x * x, axis=-1, keepdims=True) + EPS) * g


def _dot(a, b):
    return jnp.dot(a, b, preferred_element_type=F32)


def _dot_nt(a, b):
    return lax.dot_general(a, b, (((1,), (1,)), ((), ())), preferred_element_type=F32)


def _lane_id_bf16(shape):
    return lax.broadcasted_iota(I32, shape, 1).astype(F32).astype(BF16)


def _rows(start, size):
    if isinstance(start, int):
        return pl.ds(start, size)
    return pl.ds(pl.multiple_of(start, size), size)


def _const_spec(shape):
    nd = len(shape)
    return pl.BlockSpec(shape, lambda *_: (0,) * nd, pipeline_mode=pl.Buffered(1))


def _t5_bucket_np(dist):
    dist = np.maximum(dist, 0)
    max_exact = NUM_BUCKETS // 2
    d_f = np.maximum(dist, 1).astype(np.float32)
    large = max_exact + (np.log(d_f / max_exact) / math.log(MAX_DISTANCE / max_exact)
                         * (NUM_BUCKETS - max_exact)).astype(np.int32)
    large = np.minimum(large, NUM_BUCKETS - 1)
    return np.where(dist < max_exact, dist, large).astype(np.int32)


def _band_bucket_idx(dil, max_dist):
    loc = np.arange(BLOCK)[:, None] + BLOCK - np.arange(2 * BLOCK)[None, :]
    valid = (loc >= 0) & (loc <= max_dist)
    return np.where(valid, _t5_bucket_np(dil * loc), -1).astype(np.int32)


def _dense_bucket_idx(seq):
    nb = seq // BLOCK
    d = np.arange(nb)[:, None, None] * BLOCK - np.arange(BLOCK)[None, :, None] + np.arange(BLOCK)[None, None, :]
    return np.where(d >= 0, _t5_bucket_np(d), -1).astype(np.int32).reshape(seq, BLOCK)


def _bias_kernel(tab_ref, idx_ref, o_ref, *, head0):
    h = pl.program_id(0) + head0
    idx = idx_ref[...]
    out = jnp.full(idx.shape, NEG, F32)
    for b in range(NUM_BUCKETS):
        out = jnp.where(idx == b, tab_ref[b, h] * LOG2E, out)
    o_ref[0] = out


def _build_bias(table, idx_np, head0, n_heads):
    r, c = idx_np.shape
    return pl.pallas_call(
        functools.partial(_bias_kernel, head0=head0),
        grid=(n_heads,),
        in_specs=[pl.BlockSpec(memory_space=pltpu.SMEM),
                  pl.BlockSpec((r, c), lambda h: (0, 0))],
        out_specs=pl.BlockSpec((1, r, c), lambda h: (h, 0, 0)),
        out_shape=jax.ShapeDtypeStruct((n_heads, r, c), F32),
        compiler_params=_cparams(1),
        name="bias_build",
    )(table.astype(F32), jnp.asarray(idx_np))


def _ffn_kernel(*refs, n_mix, final, post_splits):
    it = iter(refs)
    x_ref = next(it)
    mix = [next(it) for _ in range(n_mix)]
    mix_w = [next(it) for _ in range(n_mix)]
    g_ref, w1_ref, w3_ref, w2_ref, g2_ref = (next(it) for _ in range(5))
    wp_ref = next(it) if post_splits else None
    o_ref = next(it)
    post_refs = [next(it) for _ in post_splits]
    x = x_ref[...]
    for a_ref, w_ref in zip(mix, mix_w):
        x = x + _dot(a_ref[...], w_ref[...])
    xn = _rms(x, g_ref[...]).astype(BF16)
    h1 = _dot(xn, w1_ref[...])
    h3 = _dot(xn, w3_ref[...])
    act = (h1 * (1.0 / (1.0 + jnp.exp(-h1)))) * h3
    y = _dot(act.astype(BF16), w2_ref[...])
    out = x + 0.5 * y
    if final:
        out = _rms(out, g2_ref[...])
    o_ref[...] = out
    if post_splits:
        p = _dot(_rms(out, g2_ref[...]).astype(BF16), wp_ref[...])
        c0 = 0
        for p_ref, (width, scale) in zip(post_refs, post_splits):
            blk = p[:, c0:c0 + width]
            p_ref[...] = (blk if scale == 1.0 else blk * scale).astype(BF16)
            c0 += width


def _ffn(h, g, ffn_w, which, g2, *, mix=(), mix_w=(), final=False, post_w=None, post_splits=()):
    t, d = h.shape
    dff = ffn_w[0].shape[-1]
    tok = lambda width: pl.BlockSpec((TM_FFN, width), lambda i: (i, 0))
    pick = lambda rows, cols: pl.BlockSpec((None, None, rows, cols), lambda i: (*which, 0, 0),
                                           pipeline_mode=pl.Buffered(1))
    sh = jax.ShapeDtypeStruct
    in_specs = [tok(d)] + [tok(a.shape[1]) for a in mix] + [_const_spec(w.shape) for w in mix_w]
    in_specs += [_const_spec((1, d)), pick(d, dff), pick(d, dff), pick(dff, d), _const_spec((1, d))]
    args = [h, *mix, *mix_w, g.reshape(1, d), *ffn_w, g2.reshape(1, d)]
    out_specs, out_shape = [tok(d)], [sh((t, d), F32)]
    if post_splits:
        in_specs.append(_const_spec(post_w.shape))
        args.append(post_w)
        out_specs += [tok(width) for width, _ in post_splits]
        out_shape += [sh((t, width), BF16) for width, _ in post_splits]
    res = pl.pallas_call(
        functools.partial(_ffn_kernel, n_mix=len(mix), final=final, post_splits=tuple(post_splits)),
        grid=(t // TM_FFN,),
        in_specs=in_specs,
        out_specs=out_specs,
        out_shape=out_shape,
        compiler_params=_cparams(1),
        name="ffn",
    )(*args)
    return res if post_splits else res[0]


def _fold_store(scr, scr4, blk, outs, n_slab):
    o1, o4, o16 = outs
    o1[...] = blk.astype(BF16)
    for c in range(n_slab):
        scr[c] = blk[:, c * LANES:(c + 1) * LANES]
    rows4, rows16 = TM // 4, TM // 16
    for r4 in range(4):
        for c in range(n_slab):
            cols = slice(c * LANES, (c + 1) * LANES)
            f4 = scr[c, pl.ds(r4, rows4, stride=4), :]
            o4[0, r4, :, cols] = f4.astype(BF16)
            scr4[c] = f4
            for q in range(4):
                o16[0, 4 * q + r4, :, cols] = scr4[c, pl.ds(q, rows16, stride=4), :].astype(BF16)


def _proj_even_kernel(h_ref, g_ref, w_ref, cqg_ref, wqb_ref, wqi_ref,
                      q1, k1, v1, q4, k4, v4, q16, k16, v16, qb, kb, vb, qi, ki, wi, scr, scr4,
                      *, aw, bw, rank):
    u = _rms(h_ref[...], g_ref[...]).astype(BF16)
    p = _dot(u, w_ref[...])
    proj = lambda c0, width: p[:, c0:c0 + width]
    n_slab = aw // LANES
    scale = Q_SCALE
    _fold_store(scr, scr4, proj(0, aw) * scale, (q1, q4, q16), n_slab)
    _fold_store(scr, scr4, proj(aw, aw), (k1, k4, k16), n_slab)
    _fold_store(scr, scr4, proj(2 * aw, aw), (v1, v4, v16), n_slab)
    c0 = 3 * aw
    kb[...] = proj(c0, bw).astype(BF16)
    vb[0] = proj(c0 + bw, bw).T.astype(BF16)
    c1 = c0 + 2 * bw
    cq = _rms(proj(c1, rank), cqg_ref[...]).astype(BF16)
    qb[...] = (_dot(cq, wqb_ref[...]) * scale).astype(BF16)
    qi[...] = _dot(cq, wqi_ref[...]).astype(BF16)
    c2 = c1 + rank
    nidx = IDX_HEADS * IDX_DIM
    ki[...] = proj(c2, nidx).astype(BF16)
    idx_scale = (IDX_DIM ** -0.5) * (IDX_HEADS ** -0.5)
    wi[...] = proj(c2 + nidx, LANES) * idx_scale


def _proj_even(h, g, w_in, cq_g, wq_b, wq_idx, batch, seq):
    t, d = h.shape
    bw = wq_b.shape[1]
    rank = wq_b.shape[0]
    aw = (w_in.shape[1] - rank - 2 * bw - IDX_DIM - IDX_HEADS) // 3
    nidx = IDX_HEADS * IDX_DIM
    offs = np.cumsum([0, aw, aw, aw, rank, bw, bw, IDX_DIM, IDX_HEADS])
    col = lambda i: w_in[:, offs[i]:offs[i + 1]]
    w_kidx = jnp.tile(col(6), (1, IDX_HEADS))
    w_widx = jnp.pad(col(7), ((0, 0), (0, LANES - IDX_HEADS)))
    w_all = jnp.concatenate([col(0), col(1), col(2), col(4), col(5), col(3), w_kidx, w_widx], axis=1).astype(BF16)
    ncol = w_all.shape[1]
    spt = seq // TM
    tok = lambda width: pl.BlockSpec((TM, width), lambda i: (i, 0))
    fold = lambda dil: pl.BlockSpec((1, dil, TM // dil, aw), lambda i: (i // spt, 0, i % spt, 0))
    sh = jax.ShapeDtypeStruct
    nat = sh((t, aw), BF16)
    f4 = sh((batch, 4, seq // 4, aw), BF16)
    f16 = sh((batch, 16, seq // 16, aw), BF16)
    return pl.pallas_call(
        functools.partial(_proj_even_kernel, aw=aw, bw=bw, rank=rank),
        grid=(t // TM,),
        in_specs=[tok(d), _const_spec((1, d)), _const_spec((d, ncol)), _const_spec((1, rank)),
                  _const_spec((rank, bw)), _const_spec((rank, nidx))],
        out_specs=[tok(aw), tok(aw), tok(aw), fold(4), fold(4), fold(4), fold(16), fold(16), fold(16),
                   tok(bw), tok(bw), pl.BlockSpec((1, bw, TM), lambda i: (i // spt, 0, i % spt)),
                   tok(nidx), tok(nidx), tok(LANES)],
        out_shape=[nat, nat, nat, f4, f4, f4, f16, f16, f16,
                   sh((t, bw), BF16), sh((t, bw), BF16), sh((batch, bw, seq), BF16),
                   sh((t, nidx), BF16), sh((t, nidx), BF16), sh((t, LANES), F32)],
        scratch_shapes=[pltpu.VMEM((aw // LANES, TM, LANES), F32), pltpu.VMEM((aw // LANES, TM // 4, LANES), F32)],
        compiler_params=_cparams(1),
        name="proj_even",
    )(h, g.reshape(1, d), w_all, cq_g.reshape(1, rank), wq_b.astype(BF16), wq_idx.astype(BF16))


def _stack_heads(q):
    lane_b = _lane_id_bf16(q.shape)
    z = jnp.zeros_like(q)
    return jnp.concatenate([jnp.where(lane_b < HEAD_DIM, q, z), jnp.where(lane_b >= HEAD_DIM, q, z)], axis=0)


def _band_block(q2, kw, vw, bias2):
    s = _dot_nt(q2, kw) + bias2
    m = jnp.max(s, axis=-1, keepdims=True)
    p = jnp.exp2(s - m)
    den = jnp.sum(p, axis=-1, keepdims=True)
    return _dot(p.astype(BF16), vw), m, den


def _attn_a_kernel(q1, k1, v1, q4, k4, v4, q16, k16, v16, bias_ref, o_ref, acc_s, m_s, den_s, *, seq):
    lo = lax.broadcasted_iota(I32, (BLOCK, LANES), 1) < HEAD_DIM
    shape = (BLOCK, LANES)
    refs = ((q1, k1, v1), (q4, k4, v4), (q16, k16, v16))
    for br, (_, dil) in enumerate(A_BRANCHES):
        qr, kr, vr = refs[br]
        for r in range(dil):
            at = (lambda ref, s, n: ref[0, pl.ds(s, n), :]) if dil == 1 else \
                 (lambda ref, s, n, r=r: ref[0, r, pl.ds(s, n), :])
            for n in range(seq // dil // BLOCK):
                q2 = _stack_heads(at(qr, n * BLOCK, BLOCK))
                if n == 0:
                    acc, m, den = _band_block(q2, at(kr, 0, BLOCK), at(vr, 0, BLOCK), bias_ref[br, 0, :, BLOCK:])
                else:
                    k0 = (n - 1) * BLOCK
                    acc, m, den = _band_block(q2, at(kr, k0, 2 * BLOCK), at(vr, k0, 2 * BLOCK), bias_ref[br, 0])
                rows = pl.ds(n * BLOCK, BLOCK) if dil == 1 else pl.ds(n * BLOCK * dil + r, BLOCK, stride=dil)
                acc_s[br, rows, :] = jnp.where(lo, acc[:BLOCK], acc[BLOCK:])
                m_s[br, rows, :] = jnp.where(lo, jnp.broadcast_to(m[:BLOCK], shape), jnp.broadcast_to(m[BLOCK:], shape))
                den_s[br, rows, :] = jnp.where(lo, jnp.broadcast_to(den[:BLOCK], shape),
                                               jnp.broadcast_to(den[BLOCK:], shape))

    chunk = 2 * BLOCK

    def merge(c, carry):
        rows = pl.ds(pl.multiple_of(c * chunk, chunk), chunk)
        ms = [m_s[br, rows, :] for br in range(3)]
        mx = jnp.maximum(jnp.maximum(ms[0], ms[1]), ms[2])
        num = jnp.zeros((chunk, LANES), F32)
        den = jnp.zeros((chunk, LANES), F32)
        for br in range(3):
            e = jnp.exp2(ms[br] - mx)
            num = num + e * acc_s[br, rows, :]
            den = den + e * den_s[br, rows, :]
        o_ref[0, rows, :] = (num / den).astype(BF16)
        return carry

    lax.fori_loop(0, seq // chunk, merge, 0)


def _attn_a(qkv, bias, batch, seq):
    q1, k1, v1, q4, k4, v4, q16, k16, v16 = qkv
    aw = q1.shape[-1]
    npair = aw // LANES
    nat = pl.BlockSpec((1, seq, LANES), lambda b, p: (b, 0, p))
    fold = lambda dil: pl.BlockSpec((1, dil, seq // dil, LANES), lambda b, p: (b, 0, 0, p))
    r3 = lambda a: a.reshape(batch, seq, aw)
    return pl.pallas_call(
        functools.partial(_attn_a_kernel, seq=seq),
        grid=(batch, npair),
        in_specs=[nat, nat, nat, fold(4), fold(4), fold(4), fold(16), fold(16), fold(16),
                  pl.BlockSpec((3, 1, 2 * BLOCK, 2 * BLOCK), lambda b, p: (0, p, 0, 0))],
        out_specs=nat,
        out_shape=jax.ShapeDtypeStruct((batch, seq, aw), BF16),
        scratch_shapes=[pltpu.VMEM((3, seq, LANES), F32)] * 3,
        compiler_params=_cparams(2),
        name="attn_dilated",
    )(r3(q1), r3(k1), r3(v1), q4, k4, v4, q16, k16, v16, bias)


def _attn_c_kernel(sink_ref, q_ref, k_ref, v_ref, bias_ref, o_ref, *, seq, group):
    g = pl.program_id(1)
    npair = group // 2
    lo = lax.broadcasted_iota(I32, (BLOCK, LANES), 1) < HEAD_DIM

    def block(q0, k0, width, bias2):
        qn = q_ref[0, pl.ds(q0, BLOCK), :]
        q2 = jnp.concatenate([_stack_heads(qn[:, p * LANES:(p + 1) * LANES]) for p in range(npair)], axis=0)
        acc, m, den = _band_block(q2, k_ref[0, pl.ds(k0, width), :], v_ref[0, pl.ds(k0, width), :], bias2)
        for p in range(npair):
            outs = []
            for hh in range(2):
                h = 2 * p + hh
                r = slice(h * BLOCK, (h + 1) * BLOCK)
                outs.append(acc[r] / (den[r] + jnp.exp2(sink_ref[g * group + h] * LOG2E - m[r])))
            o_ref[0, pl.ds(q0, BLOCK), p * LANES:(p + 1) * LANES] = jnp.where(lo, outs[0], outs[1]).astype(BF16)

    block(0, 0, BLOCK, bias_ref[0, :, BLOCK:])

    def body(n, carry):
        block(pl.multiple_of(n * BLOCK, BLOCK), pl.multiple_of((n - 1) * BLOCK, BLOCK), 2 * BLOCK, bias_ref[0])
        return carry

    lax.fori_loop(1, seq // BLOCK, body, 0, unroll=C_UNROLL)


def _attn_c(q, k2, v2, bias, sinks, batch, seq, group):
    qw = q.shape[-1]
    n_kv = k2.shape[-1] // LANES
    gw = qw // n_kv
    return pl.pallas_call(
        functools.partial(_attn_c_kernel, seq=seq, group=group),
        grid=(batch, n_kv),
        in_specs=[pl.BlockSpec(memory_space=pltpu.SMEM),
                  pl.BlockSpec((1, seq, gw), lambda b, g: (b, 0, g)),
                  pl.BlockSpec((1, seq, LANES), lambda b, g: (b, 0, g)),
                  pl.BlockSpec((1, seq, LANES), lambda b, g: (b, 0, g)),
                  pl.BlockSpec((1, group * BLOCK, 2 * BLOCK), lambda b, g: (g, 0, 0))],
        out_specs=pl.BlockSpec((1, seq, gw), lambda b, g: (b, 0, g)),
        out_shape=jax.ShapeDtypeStruct((batch, seq, qw), BF16),
        compiler_params=_cparams(2),
        name="attn_swa",
    )(sinks, q, k2, v2, bias)


class _Job:
    def __init__(self, i, nchs, slot, qs_ref, wt_ref, q2_ref, sk_ref, mask_ref):
        self.i, self.nchs = i, nchs
        self.qs, self.wt, self.q2 = qs_ref.at[slot], wt_ref.at[slot], q2_ref.at[slot]
        self.sk, self.mask = sk_ref.at[slot], mask_ref.at[slot]
        key_pos = lax.broadcasted_iota(I32, (CW, BLOCK), 0) + (nchs - 1) * CW
        qry_pos = lax.broadcasted_iota(I32, (CW, BLOCK), 1) + i * BLOCK
        self.causal_last = key_pos <= qry_pos


def _dsa_scores(job, ki_ref):
    for c in range(job.nchs):
        st = _dot_nt(ki_ref[0, c * CW:(c + 1) * CW, :], job.qs[...])
        score = None
        for h in range(IDX_HEADS):
            term = jnp.maximum(st[:, h * BLOCK:(h + 1) * BLOCK], 0.0) * job.wt[h:h + 1, :]
            score = term if score is None else score + term
        score = jnp.where(score == 0.0, 0.0, score)
        if c == job.nchs - 1:
            score = jnp.where(job.causal_last, score, -jnp.inf)
        bits = lax.bitcast_convert_type(score, I32)
        job.sk[c] = bits ^ (lax.shift_right_arithmetic(bits, 31) & 0x7FFFFFFF)


def _dsa_count(job, pred):
    grp = 8 * 8
    acc = jnp.zeros((grp, BLOCK), F32)
    for c in range(job.nchs):
        for g in range(CW // grp):
            acc = jnp.where(pred(job.sk[c, g * grp:(g + 1) * grp, :]), acc + 1.0, acc)
    return jnp.sum(acc, axis=0, keepdims=True)


def _dsa_select(job, thr, topk):
    for c in range(job.nchs):
        sel = job.sk[c] >= thr
        if c == job.nchs - 1:
            sel = sel & job.causal_last
        job.mask[c] = jnp.where(sel, 0.0, NEG)
    n_ge = _dsa_count(job, lambda x: x >= thr)
    return jnp.max(jnp.where((n_ge > float(topk)) & (thr > KEY_NEG_INF), 1.0, 0.0))


def _dsa_break_ties(job, thr, topk):
    need = float(topk) - _dsa_count(job, lambda x: x > thr)
    r_i = lax.broadcasted_iota(I32, (BLOCK, BLOCK), 0)
    c_i = lax.broadcasted_iota(I32, (BLOCK, BLOCK), 1)
    lower = jnp.where(c_i <= r_i, 1.0, 0.0).astype(BF16)
    carry = jnp.zeros((1, BLOCK), F32)
    for c in range(job.nchs):
        for t in range(CW // BLOCK):
            rows = slice(t * BLOCK, (t + 1) * BLOCK)
            xs = job.sk[c, rows, :]
            eq = xs == thr
            eqf = jnp.where(eq, 1.0, 0.0)
            rank = _dot(lower, eqf.astype(BF16)) + carry
            keep = (xs > thr) | (eq & (rank <= need))
            causal = r_i + (c * CW + t * BLOCK) <= c_i + job.i * BLOCK
            job.mask[c, rows, :] = jnp.where(keep & causal, 0.0, NEG)
            carry = carry + jnp.sum(eqf, axis=0, keepdims=True)


def _dsa_attend(job, kb_ref, vt_ref, bias_ref, o_ref, npair):
    i, nchs, mask_ref, q2_ref = job.i, job.nchs, job.mask, job.q2
    tpc = CW // BLOCK
    out_rows = pl.ds(pl.multiple_of(i * BLOCK, BLOCK), BLOCK)
    first_half = lax.broadcasted_iota(I32, (LANES, BLOCK), 0) < HEAD_DIM
    for p in range(npair):
        lanes = slice(p * LANES, (p + 1) * LANES)
        logits = []
        for c in range(nchs):
            s = _dot_nt(kb_ref[0, c * CW:(c + 1) * CW, lanes], q2_ref[p])
            msk = mask_ref[c]
            per_head = []
            for hh in range(2):
                bias = jnp.concatenate(
                    [bias_ref[2 * p + hh, jnp.maximum(i - (c * tpc + t), 0)] for t in range(tpc)], axis=0)
                per_head.append(s[:, hh * BLOCK:(hh + 1) * BLOCK] + bias + msk)
            logits.append(per_head)
        out_t = jnp.zeros((LANES, 2 * BLOCK), F32)
        dens = []
        pes = [[None, None] for _ in range(nchs)]
        for hh in range(2):
            m = None
            for c in range(nchs):
                mc = jnp.max(logits[c][hh], axis=0, keepdims=True)
                m = mc if m is None else jnp.maximum(m, mc)
            den = jnp.zeros((1, BLOCK), F32)
            for c in range(nchs):
                pe = jnp.exp2(logits[c][hh] - m)
                den = den + jnp.sum(pe, axis=0, keepdims=True)
                pes[c][hh] = pe.astype(BF16)
            dens.append(den)
        for c in range(nchs):
            out_t = out_t + _dot(vt_ref[0, lanes, c * CW:(c + 1) * CW], jnp.concatenate(pes[c], axis=1))
        o_t = jnp.where(first_half, out_t[:, :BLOCK] / dens[0], out_t[:, BLOCK:] / dens[1])
        o_ref[0, out_rows, lanes] = o_t.T.astype(BF16)


def _dsa_kernel(qi_ref, wi_ref, ki_ref, qb_ref, kb_ref, vt_ref, bias_ref, o_ref,
                qs_ref, wt_ref, q2_ref, sk_ref, mask_ref, *, topk, n_heads, n_blocks):
    j = pl.program_id(1)
    npair = n_heads // 2
    tpc = CW // BLOCK
    n_chunks = n_blocks // tpc
    blocks = (j, n_blocks - 1 - j)

    for slot, i in enumerate(blocks):
        rows = pl.ds(pl.multiple_of(i * BLOCK, BLOCK), BLOCK)
        qi = qi_ref[0, rows, :]
        lane_i = _lane_id_bf16(qi.shape)
        for h in range(IDX_HEADS):
            in_head = (lane_i >= h * IDX_DIM) & (lane_i < (h + 1) * IDX_DIM)
            qs_ref[slot, h * BLOCK:(h + 1) * BLOCK, :] = jnp.where(in_head, qi, jnp.zeros_like(qi))
        wt_ref[slot] = wi_ref[0, rows, :].T
        for p in range(npair):
            q2_ref[slot, p] = _stack_heads(qb_ref[0, rows, p * LANES:(p + 1) * LANES])

    for v in range(n_chunks // 2):
        @pl.when(j // tpc == v)
        def _(v=v):
            jobs = [_Job(i, nchs, slot, qs_ref, wt_ref, q2_ref, sk_ref, mask_ref)
                    for slot, (i, nchs) in enumerate(zip(blocks, (v + 1, n_chunks - v)))]
            for job in jobs:
                _dsa_scores(job, ki_ref)

            def search(it, answers):
                bit = lax.shift_left(jnp.int32(1), 31 - it)
                out = []
                for job, ans in zip(jobs, answers):
                    cand_s = (ans | bit) ^ INT_MIN
                    n = _dsa_count(job, lambda x, c=cand_s: x >= c)
                    out.append(jnp.where(n >= float(topk), ans | bit, ans))
                return tuple(out)

            answers = lax.fori_loop(0, 32, search, tuple(jnp.zeros((1, BLOCK), I32) for _ in jobs))
            thrs = [ans ^ INT_MIN for ans in answers]
            surplus = [_dsa_select(job, thr, topk) for job, thr in zip(jobs, thrs)]

            @pl.when(functools.reduce(jnp.maximum, surplus) > 0.0)
            def _():
                for job, thr in zip(jobs, thrs):
                    _dsa_break_ties(job, thr, topk)

            for job in jobs:
                _dsa_attend(job, kb_ref, vt_ref, bias_ref, o_ref, npair)


def _dsa(qi, wi, ki, qb, kb, vt, bias, batch, seq, topk):
    bw = qb.shape[-1]
    nidx = qi.shape[-1]
    nb = seq // BLOCK
    n_heads = bw // HEAD_DIM
    npair = n_heads // 2
    assert nb % (2 * (CW // BLOCK)) == 0
    full = lambda width: pl.BlockSpec((1, seq, width), lambda b, j: (b, 0, 0))
    r3 = lambda a: a.reshape(batch, seq, a.shape[-1])
    vm = pltpu.VMEM
    out = pl.pallas_call(
        functools.partial(_dsa_kernel, topk=topk, n_heads=n_heads, n_blocks=nb),
        grid=(batch, nb // 2),
        in_specs=[full(nidx), full(LANES), full(nidx), full(bw), full(bw),
                  pl.BlockSpec((1, bw, seq), lambda b, j: (b, 0, 0)),
                  _const_spec((n_heads, nb, BLOCK, BLOCK))],
        out_specs=full(bw),
        out_shape=jax.ShapeDtypeStruct((batch, seq, bw), BF16),
        scratch_shapes=[vm((2, IDX_HEADS * BLOCK, nidx), BF16), vm((2, LANES, BLOCK), F32),
                        vm((2, npair, 2 * BLOCK, LANES), BF16),
                        vm((2, seq // CW, CW, BLOCK), I32), vm((2, seq // CW, CW, BLOCK), F32)],
        compiler_params=_cparams(2),
        name="dsa",
    )(r3(qi), r3(wi), r3(ki), r3(qb), r3(kb), vt, bias)
    return out.reshape(batch * seq, bw)


def _split_rows(w_out, parts):
    ws, r0 = [], 0
    for a in parts:
        ws.append(w_out[r0:r0 + a.shape[1]].astype(BF16))
        r0 += a.shape[1]
    return ws


def _odd_proj_weights(w_in, qw, kvw):
    d = w_in.shape[0]
    dup = lambda w: jnp.repeat(w.reshape(d, kvw // HEAD_DIM, 1, HEAD_DIM), 2, axis=2).reshape(d, 2 * kvw)
    return jnp.concatenate([w_in[:, :qw], dup(w_in[:, qw:qw + kvw]), dup(w_in[:, qw + kvw:])], axis=1).astype(BF16)


def kernel(x, norm_g, final_g, rel_bias_table, ffn_w1, ffn_w3, ffn_w2, hyb_w_in, hyb_cq_g, hyb_wq_b,
           hyb_wq_idx, hyb_w_out, swa_w_in, swa_sinks, swa_w_out):
    batch, seq, d = x.shape
    depth = norm_g.shape[0]
    n_heads = d // HEAD_DIM
    a_heads = n_heads // 2
    topk = min(TOPK_MAX, seq // 4)
    assert seq % (BLOCK * 16) == 0 and seq % CW == 0 and (batch * seq) % TM_FFN == 0 and seq % TM == 0

    bias_a = _build_bias(
        rel_bias_table,
        np.concatenate([_band_bucket_idx(dil, win // dil) for win, dil in A_BRANCHES], axis=0),
        0, a_heads).reshape(a_heads // 2, 2, 3, BLOCK, 2 * BLOCK).transpose(2, 0, 1, 3, 4)
    bias_a = bias_a.reshape(3, a_heads // 2, 2 * BLOCK, 2 * BLOCK)
    bias_b = _build_bias(rel_bias_table, _dense_bucket_idx(seq), a_heads, n_heads - a_heads)
    bias_b = bias_b.reshape(n_heads - a_heads, seq // BLOCK, BLOCK, BLOCK)
    bias_c = _build_bias(rel_bias_table, _band_bucket_idx(1, C_WINDOW - 1), 0, n_heads)

    ffn_w = (ffn_w1.astype(BF16), ffn_w3.astype(BF16), ffn_w2.astype(BF16))
    h = x.reshape(batch * seq, d)
    for layer in range(depth):
        g = norm_g[layer]
        i = layer // 2
        ffn_a = (g[0], ffn_w, (layer, 0))
        ffn_b = (g[2], ffn_w, (layer, 1))
        if layer % 2 == 0:
            h = _ffn(h, *ffn_a, final_g)
            outs = _proj_even(h, g[1], hyb_w_in[i], hyb_cq_g[i], hyb_wq_b[i], hyb_wq_idx[i], batch, seq)
            o_a = _attn_a(outs[:9], bias_a, batch, seq)
            qb, kb, vt, qi, ki, wi = outs[9:]
            o_b = _dsa(qi, wi, ki, qb, kb, vt, bias_b, batch, seq, topk)
            mix, w_out = [o_a.reshape(batch * seq, -1), o_b], hyb_w_out[i]
        else:
            kvw = (swa_w_in.shape[-1] - d) // 2
            n_kv = kvw // HEAD_DIM
            group = n_heads // n_kv
            h, q, k2, v2 = _ffn(h, *ffn_a, g[1], post_w=_odd_proj_weights(swa_w_in[i], d, kvw),
                                post_splits=((d, Q_SCALE), (2 * kvw, 1.0), (2 * kvw, 1.0)))
            r3 = lambda a: a.reshape(batch, seq, a.shape[-1])
            o_c = _attn_c(r3(q), r3(k2), r3(v2), bias_c.reshape(n_kv, group * BLOCK, 2 * BLOCK),
                          swa_sinks[i].astype(F32), batch, seq, group)
            mix, w_out = [o_c.reshape(batch * seq, -1)], swa_w_out[i]
        h = _ffn(h, *ffn_b, final_g, mix=mix, mix_w=_split_rows(w_out, mix), final=layer == depth - 1)
    return h.reshape(batch, seq, d)
```

```python
import functools
import math

import numpy as np
import jax
import jax.numpy as jnp
from jax import lax
from jax.experimental import pallas as pl
from jax.experimental.pallas import tpu as pltpu

F32 = jnp.float32
BF16 = jnp.bfloat16
I32 = jnp.int32

EPS = 1e-6
HEAD_DIM = 64
BLOCK = 128
LANES = 128
NUM_BUCKETS = 32
MAX_DISTANCE = 2048
A_BRANCHES = ((128, 1), (512, 4), (2048, 16))
C_WINDOW = 128
TOPK_MAX = 256
IDX_HEADS = 8
IDX_DIM = 32
NEG = -1e30
LOG2E = math.log2(math.e)
Q_SCALE = HEAD_DIM ** -0.5 * LOG2E
INT_MIN = -(2 ** 31)
VMEM_LIMIT = 56 * 1024 * 1024

TM = 1024
TM_FFN = 1024
CW = 2 * BLOCK
C_UNROLL = 5
ONES_ROWS = 16


def _cparams(n_axes):
    return pltpu.CompilerParams(
        dimension_semantics=("arbitrary",) * n_axes, vmem_limit_bytes=VMEM_LIMIT)


def _rms(x, g):
    return x * lax.rsqrt(jnp.mean(x * x, axis=-1, keepdims=True) + EPS) * g


def _dot(a, b):
    return jnp.dot(a, b, preferred_element_type=F32)


def _dot_nt(a, b):
    return lax.dot_general(a, b, (((1,), (1,)), ((), ())), preferred_element_type=F32)


def _lane_id_bf16(shape):
    return lax.broadcasted_iota(I32, shape, 1).astype(F32).astype(BF16)


def _rows(start, size):
    if isinstance(start, int):
        return pl.ds(start, size)
    return pl.ds(pl.multiple_of(start, size), size)


def _const_spec(shape):
    nd = len(shape)
    return pl.BlockSpec(shape, lambda *_: (0,) * nd, pipeline_mode=pl.Buffered(1))


def _t5_bucket_np(dist):
    dist = np.maximum(dist, 0)
    max_exact = NUM_BUCKETS // 2
    d_f = np.maximum(dist, 1).astype(np.float32)
    large = max_exact + (np.log(d_f / max_exact) / math.log(MAX_DISTANCE / max_exact)
                         * (NUM_BUCKETS - max_exact)).astype(np.int32)
    large = np.minimum(large, NUM_BUCKETS - 1)
    return np.where(dist < max_exact, dist, large).astype(np.int32)


def _band_bucket_idx(dil, max_dist):
    loc = np.arange(BLOCK)[:, None] + BLOCK - np.arange(2 * BLOCK)[None, :]
    valid = (loc >= 0) & (loc <= max_dist)
    return np.where(valid, _t5_bucket_np(dil * loc), -1).astype(np.int32)


def _dense_bucket_idx(seq):
    nb = seq // BLOCK
    d = np.arange(nb)[:, None, None] * BLOCK - np.arange(BLOCK)[None, :, None] + np.arange(BLOCK)[None, None, :]
    return np.where(d >= 0, _t5_bucket_np(d), -1).astype(np.int32).reshape(seq, BLOCK)


def _bias_kernel(tab_ref, idx_ref, o_ref, *, head0, buckets):
    h = pl.program_id(0) + head0
    for blk, present in enumerate(buckets):
        rows = slice(blk * BLOCK, (blk + 1) * BLOCK)
        idx = idx_ref[rows, :]
        out = jnp.full(idx.shape, NEG, F32)
        for b in present:
            out = jnp.where(idx == b, tab_ref[b, h] * LOG2E, out)
        o_ref[0, rows, :] = out


def _build_bias(table, idx_np, head0, n_heads):
    r, c = idx_np.shape
    buckets = tuple(tuple(int(b) for b in np.unique(idx_np[r0:r0 + BLOCK]) if b >= 0) for r0 in range(0, r, BLOCK))
    return pl.pallas_call(
        functools.partial(_bias_kernel, head0=head0, buckets=buckets),
        grid=(n_heads,),
        in_specs=[pl.BlockSpec(memory_space=pltpu.SMEM),
                  pl.BlockSpec((r, c), lambda h: (0, 0))],
        out_specs=pl.BlockSpec((1, r, c), lambda h: (h, 0, 0)),
        out_shape=jax.ShapeDtypeStruct((n_heads, r, c), F32),
        compiler_params=_cparams(1),
        name="bias_build",
    )(table.astype(F32), jnp.asarray(idx_np))


def _ffn_kernel(*refs, n_mix, final, post_splits):
    it = iter(refs)
    x_ref = next(it)
    mix = [next(it) for _ in range(n_mix)]
    mix_w = [next(it) for _ in range(n_mix)]
    g_ref, w1_ref, w3_ref, w2_ref, g2_ref = (next(it) for _ in range(5))
    wp_ref = next(it) if post_splits else None
    o_ref = next(it)
    post_refs = [next(it) for _ in post_splits]
    x = x_ref[...]
    for a_ref, w_ref in zip(mix, mix_w):
        x = x + _dot(a_ref[...], w_ref[...])
    xn = _rms(x, g_ref[...]).astype(BF16)
    h1 = _dot(xn, w1_ref[...])
    h3 = _dot(xn, w3_ref[...])
    act = (h1 * (1.0 / (1.0 + jnp.exp(-h1)))) * h3
    y = _dot(act.astype(BF16), w2_ref[...])
    out = x + 0.5 * y
    if final:
        out = _rms(out, g2_ref[...])
    o_ref[...] = out
    if post_splits:
        p = _dot(_rms(out, g2_ref[...]).astype(BF16), wp_ref[...])
        c0 = 0
        for p_ref, (width, scale) in zip(post_refs, post_splits):
            blk = p[:, c0:c0 + width]
            p_ref[...] = (blk if scale == 1.0 else blk * scale).astype(BF16)
            c0 += width


def _ffn(h, g, ffn_w, which, g2, *, mix=(), mix_w=(), final=False, post_w=None, post_splits=()):
    t, d = h.shape
    dff = ffn_w[0].shape[-1]
    tok = lambda width: pl.BlockSpec((TM_FFN, width), lambda i: (i, 0))
    pick = lambda rows, cols: pl.BlockSpec((None, None, rows, cols), lambda i: (*which, 0, 0),
                                           pipeline_mode=pl.Buffered(1))
    sh = jax.ShapeDtypeStruct
    in_specs = [tok(d)] + [tok(a.shape[1]) for a in mix] + [_const_spec(w.shape) for w in mix_w]
    in_specs += [_const_spec((1, d)), pick(d, dff), pick(d, dff), pick(dff, d), _const_spec((1, d))]
    args = [h, *mix, *mix_w, g.reshape(1, d), *ffn_w, g2.reshape(1, d)]
    out_specs, out_shape = [tok(d)], [sh((t, d), F32)]
    if post_splits:
        in_specs.append(_const_spec(post_w.shape))
        args.append(post_w)
        out_specs += [tok(width) for width, _ in post_splits]
        out_shape += [sh((t, width), BF16) for width, _ in post_splits]
    res = pl.pallas_call(
        functools.partial(_ffn_kernel, n_mix=len(mix), final=final, post_splits=tuple(post_splits)),
        grid=(t // TM_FFN,),
        in_specs=in_specs,
        out_specs=out_specs,
        out_shape=out_shape,
        compiler_params=_cparams(1),
        name="ffn",
    )(*args)
    return res if post_splits else res[0]


def _fold_store(scr, scr4, blk, outs, n_slab):
    o1, o4, o16 = outs
    o1[...] = blk.astype(BF16)
    for c in range(n_slab):
        scr[c] = blk[:, c * LANES:(c + 1) * LANES]
    rows4, rows16 = TM // 4, TM // 16
    for r4 in range(4):
        for c in range(n_slab):
            cols = slice(c * LANES, (c + 1) * LANES)
            f4 = scr[c, pl.ds(r4, rows4, stride=4), :]
            o4[0, r4, :, cols] = f4.astype(BF16)
            scr4[c] = f4
            for q in range(4):
                o16[0, 4 * q + r4, :, cols] = scr4[c, pl.ds(q, rows16, stride=4), :].astype(BF16)


def _proj_even_kernel(h_ref, g_ref, w_ref, cqg_ref, wqb_ref, wqi_ref,
                      q1, k1, v1, q4, k4, v4, q16, k16, v16, qb, kb, vb, qi, ki, wi, scr, scr4,
                      *, aw, bw, rank):
    u = _rms(h_ref[...], g_ref[...]).astype(BF16)
    p = _dot(u, w_ref[...])
    proj = lambda c0, width: p[:, c0:c0 + width]
    n_slab = aw // LANES
    scale = Q_SCALE
    _fold_store(scr, scr4, proj(0, aw) * scale, (q1, q4, q16), n_slab)
    _fold_store(scr, scr4, proj(aw, aw), (k1, k4, k16), n_slab)
    _fold_store(scr, scr4, proj(2 * aw, aw), (v1, v4, v16), n_slab)
    c0 = 3 * aw
    kb[...] = proj(c0, bw).astype(BF16)
    vb[0] = proj(c0 + bw, bw).T.astype(BF16)
    c1 = c0 + 2 * bw
    cq = _rms(proj(c1, rank), cqg_ref[...]).astype(BF16)
    qb[...] = (_dot(cq, wqb_ref[...]) * scale).astype(BF16)
    qi[...] = _dot(cq, wqi_ref[...]).astype(BF16)
    c2 = c1 + rank
    nidx = IDX_HEADS * IDX_DIM
    ki[...] = proj(c2, nidx).astype(BF16)
    idx_scale = (IDX_DIM ** -0.5) * (IDX_HEADS ** -0.5)
    wi[...] = proj(c2 + nidx, LANES) * idx_scale


def _proj_even(h, g, w_in, cq_g, wq_b, wq_idx, batch, seq):
    t, d = h.shape
    bw = wq_b.shape[1]
    rank = wq_b.shape[0]
    aw = (w_in.shape[1] - rank - 2 * bw - IDX_DIM - IDX_HEADS) // 3
    nidx = IDX_HEADS * IDX_DIM
    offs = np.cumsum([0, aw, aw, aw, rank, bw, bw, IDX_DIM, IDX_HEADS])
    col = lambda i: w_in[:, offs[i]:offs[i + 1]]
    w_kidx = jnp.tile(col(6), (1, IDX_HEADS))
    w_widx = jnp.pad(col(7), ((0, 0), (0, LANES - IDX_HEADS)))
    w_all = jnp.concatenate([col(0), col(1), col(2), col(4), col(5), col(3), w_kidx, w_widx], axis=1).astype(BF16)
    ncol = w_all.shape[1]
    spt = seq // TM
    tok = lambda width: pl.BlockSpec((TM, width), lambda i: (i, 0))
    fold = lambda dil: pl.BlockSpec((1, dil, TM // dil, aw), lambda i: (i // spt, 0, i % spt, 0))
    sh = jax.ShapeDtypeStruct
    nat = sh((t, aw), BF16)
    f4 = sh((batch, 4, seq // 4, aw), BF16)
    f16 = sh((batch, 16, seq // 16, aw), BF16)
    return pl.pallas_call(
        functools.partial(_proj_even_kernel, aw=aw, bw=bw, rank=rank),
        grid=(t // TM,),
        in_specs=[tok(d), _const_spec((1, d)), _const_spec((d, ncol)), _const_spec((1, rank)),
                  _const_spec((rank, bw)), _const_spec((rank, nidx))],
        out_specs=[tok(aw), tok(aw), tok(aw), fold(4), fold(4), fold(4), fold(16), fold(16), fold(16),
                   tok(bw), tok(bw), pl.BlockSpec((1, bw, TM), lambda i: (i // spt, 0, i % spt)),
                   tok(nidx), tok(nidx), tok(LANES)],
        out_shape=[nat, nat, nat, f4, f4, f4, f16, f16, f16,
                   sh((t, bw), BF16), sh((t, bw), BF16), sh((batch, bw, seq), BF16),
                   sh((t, nidx), BF16), sh((t, nidx), BF16), sh((t, LANES), F32)],
        scratch_shapes=[pltpu.VMEM((aw // LANES, TM, LANES), F32), pltpu.VMEM((aw // LANES, TM // 4, LANES), F32)],
        compiler_params=_cparams(1),
        name="proj_even",
    )(h, g.reshape(1, d), w_all, cq_g.reshape(1, rank), wq_b.astype(BF16), wq_idx.astype(BF16))


def _stack_heads(q):
    lane_b = _lane_id_bf16(q.shape)
    z = jnp.zeros_like(q)
    return jnp.concatenate([jnp.where(lane_b < HEAD_DIM, q, z), jnp.where(lane_b >= HEAD_DIM, q, z)], axis=0)


def _band_block(q2, kw, vw, bias2):
    s = _dot_nt(q2, kw) + bias2
    m = jnp.max(s, axis=-1, keepdims=True)
    p = jnp.exp2(s - m)
    den = jnp.sum(p, axis=-1, keepdims=True)
    return _dot(p.astype(BF16), vw), m, den


def _attn_a_kernel(q1, k1, v1, q4, k4, v4, q16, k16, v16, bias_ref, o_ref, acc_s, m_s, den_s, *, seq):
    lo = lax.broadcasted_iota(I32, (BLOCK, LANES), 1) < HEAD_DIM
    shape = (BLOCK, LANES)
    refs = ((q1, k1, v1), (q4, k4, v4), (q16, k16, v16))
    for br, (_, dil) in enumerate(A_BRANCHES):
        qr, kr, vr = refs[br]
        for r in range(dil):
            at = (lambda ref, s, n: ref[0, pl.ds(s, n), :]) if dil == 1 else \
                 (lambda ref, s, n, r=r: ref[0, r, pl.ds(s, n), :])
            for n in range(seq // dil // BLOCK):
                q2 = _stack_heads(at(qr, n * BLOCK, BLOCK))
                if n == 0:
                    acc, m, den = _band_block(q2, at(kr, 0, BLOCK), at(vr, 0, BLOCK), bias_ref[br, 0, :, BLOCK:])
                else:
                    k0 = (n - 1) * BLOCK
                    acc, m, den = _band_block(q2, at(kr, k0, 2 * BLOCK), at(vr, k0, 2 * BLOCK), bias_ref[br, 0])
                rows = pl.ds(n * BLOCK, BLOCK) if dil == 1 else pl.ds(n * BLOCK * dil + r, BLOCK, stride=dil)
                acc_s[br, rows, :] = jnp.where(lo, acc[:BLOCK], acc[BLOCK:])
                m_s[br, rows, :] = jnp.where(lo, jnp.broadcast_to(m[:BLOCK], shape), jnp.broadcast_to(m[BLOCK:], shape))
                den_s[br, rows, :] = jnp.where(lo, jnp.broadcast_to(den[:BLOCK], shape),
                                               jnp.broadcast_to(den[BLOCK:], shape))

    chunk = 2 * BLOCK

    def merge(c, carry):
        rows = pl.ds(pl.multiple_of(c * chunk, chunk), chunk)
        ms = [m_s[br, rows, :] for br in range(3)]
        mx = jnp.maximum(jnp.maximum(ms[0], ms[1]), ms[2])
        num = jnp.zeros((chunk, LANES), F32)
        den = jnp.zeros((chunk, LANES), F32)
        for br in range(3):
            e = jnp.exp2(ms[br] - mx)
            num = num + e * acc_s[br, rows, :]
            den = den + e * den_s[br, rows, :]
        o_ref[0, rows, :] = (num / den).astype(BF16)
        return carry

    lax.fori_loop(0, seq // chunk, merge, 0)


def _attn_a(qkv, bias, batch, seq):
    q1, k1, v1, q4, k4, v4, q16, k16, v16 = qkv
    aw = q1.shape[-1]
    npair = aw // LANES
    nat = pl.BlockSpec((1, seq, LANES), lambda b, p: (b, 0, p))
    fold = lambda dil: pl.BlockSpec((1, dil, seq // dil, LANES), lambda b, p: (b, 0, 0, p))
    r3 = lambda a: a.reshape(batch, seq, aw)
    return pl.pallas_call(
        functools.partial(_attn_a_kernel, seq=seq),
        grid=(batch, npair),
        in_specs=[nat, nat, nat, fold(4), fold(4), fold(4), fold(16), fold(16), fold(16),
                  pl.BlockSpec((3, 1, 2 * BLOCK, 2 * BLOCK), lambda b, p: (0, p, 0, 0))],
        out_specs=nat,
        out_shape=jax.ShapeDtypeStruct((batch, seq, aw), BF16),
        scratch_shapes=[pltpu.VMEM((3, seq, LANES), F32)] * 3,
        compiler_params=_cparams(2),
        name="attn_dilated",
    )(r3(q1), r3(k1), r3(v1), q4, k4, v4, q16, k16, v16, bias)


def _attn_c_kernel(sink_ref, q_ref, k_ref, v_ref, bias_ref, o_ref, *, seq, group):
    g = pl.program_id(1)
    npair = group // 2
    lo = lax.broadcasted_iota(I32, (BLOCK, LANES), 1) < HEAD_DIM

    def block(q0, k0, width, bias2):
        qn = q_ref[0, pl.ds(q0, BLOCK), :]
        q2 = jnp.concatenate([_stack_heads(qn[:, p * LANES:(p + 1) * LANES]) for p in range(npair)], axis=0)
        acc, m, den = _band_block(q2, k_ref[0, pl.ds(k0, width), :], v_ref[0, pl.ds(k0, width), :], bias2)
        for p in range(npair):
            outs = []
            for hh in range(2):
                h = 2 * p + hh
                r = slice(h * BLOCK, (h + 1) * BLOCK)
                outs.append(acc[r] / (den[r] + jnp.exp2(sink_ref[g * group + h] * LOG2E - m[r])))
            o_ref[0, pl.ds(q0, BLOCK), p * LANES:(p + 1) * LANES] = jnp.where(lo, outs[0], outs[1]).astype(BF16)

    block(0, 0, BLOCK, bias_ref[0, :, BLOCK:])

    def body(n, carry):
        block(pl.multiple_of(n * BLOCK, BLOCK), pl.multiple_of((n - 1) * BLOCK, BLOCK), 2 * BLOCK, bias_ref[0])
        return carry

    lax.fori_loop(1, seq // BLOCK, body, 0, unroll=C_UNROLL)


def _attn_c(q, k2, v2, bias, sinks, batch, seq, group):
    qw = q.shape[-1]
    n_kv = k2.shape[-1] // LANES
    gw = qw // n_kv
    return pl.pallas_call(
        functools.partial(_attn_c_kernel, seq=seq, group=group),
        grid=(batch, n_kv),
        in_specs=[pl.BlockSpec(memory_space=pltpu.SMEM),
                  pl.BlockSpec((1, seq, gw), lambda b, g: (b, 0, g)),
                  pl.BlockSpec((1, seq, LANES), lambda b, g: (b, 0, g)),
                  pl.BlockSpec((1, seq, LANES), lambda b, g: (b, 0, g)),
                  pl.BlockSpec((1, group * BLOCK, 2 * BLOCK), lambda b, g: (g, 0, 0))],
        out_specs=pl.BlockSpec((1, seq, gw), lambda b, g: (b, 0, g)),
        out_shape=jax.ShapeDtypeStruct((batch, seq, qw), BF16),
        compiler_params=_cparams(2),
        name="attn_swa",
    )(sinks, q, k2, v2, bias)


class _Job:
    def __init__(self, i, nchs, slot, qs_ref, wt_ref, q2_ref, sk_ref, mask_ref):
        self.i, self.nchs = i, nchs
        self.qs, self.wt, self.q2 = qs_ref.at[slot], wt_ref.at[slot], q2_ref.at[slot]
        self.sk, self.mask = sk_ref.at[slot], mask_ref.at[slot]
        key_pos = lax.broadcasted_iota(I32, (CW, BLOCK), 0) + (nchs - 1) * CW
        qry_pos = lax.broadcasted_iota(I32, (CW, BLOCK), 1) + i * BLOCK
        self.causal_last = key_pos <= qry_pos


def _dsa_scores(job, ki_ref):
    for c in range(job.nchs):
        st = _dot_nt(ki_ref[0, c * CW:(c + 1) * CW, :], job.qs[...])
        score = None
        for h in range(IDX_HEADS):
            term = jnp.maximum(st[:, h * BLOCK:(h + 1) * BLOCK], 0.0) * job.wt[h:h + 1, :]
            score = term if score is None else score + term
        if c == job.nchs - 1:
            score = jnp.where(job.causal_last, score, -jnp.inf)
        job.sk[c] = score


def _key_to_float(key):
    sk = key ^ INT_MIN
    return lax.bitcast_convert_type(sk ^ (lax.shift_right_arithmetic(sk, 31) & 0x7FFFFFFF), F32)


def _dsa_count(job, pred):
    grp = 8 * 8
    acc = jnp.zeros((grp, BLOCK), F32)
    for c in range(job.nchs):
        for g in range(CW // grp):
            acc = jnp.where(pred(job.sk[c, g * grp:(g + 1) * grp, :]), acc + 1.0, acc)
    return jnp.sum(acc, axis=0, keepdims=True)


def _dsa_select(job, thr, topk):
    for c in range(job.nchs):
        sel = job.sk[c] >= thr
        if c == job.nchs - 1:
            sel = sel & job.causal_last
        job.mask[c] = jnp.where(sel, 0.0, NEG)
    n_ge = _dsa_count(job, lambda x: x >= thr)
    return jnp.max(jnp.where((n_ge > float(topk)) & (thr > -jnp.inf), 1.0, 0.0))


def _dsa_break_ties(job, thr, topk):
    need = float(topk) - _dsa_count(job, lambda x: x > thr)
    r_i = lax.broadcasted_iota(I32, (BLOCK, BLOCK), 0)
    c_i = lax.broadcasted_iota(I32, (BLOCK, BLOCK), 1)
    lower = jnp.where(c_i <= r_i, 1.0, 0.0).astype(BF16)
    carry = jnp.zeros((1, BLOCK), F32)
    for c in range(job.nchs):
        for t in range(CW // BLOCK):
            rows = slice(t * BLOCK, (t + 1) * BLOCK)
            xs = job.sk[c, rows, :]
            eq = xs == thr
            eqf = jnp.where(eq, 1.0, 0.0)
            rank = _dot(lower, eqf.astype(BF16)) + carry
            keep = (xs > thr) | (eq & (rank <= need))
            causal = r_i + (c * CW + t * BLOCK) <= c_i + job.i * BLOCK
            job.mask[c, rows, :] = jnp.where(keep & causal, 0.0, NEG)
            carry = carry + jnp.sum(eqf, axis=0, keepdims=True)


def _dsa_attend(job, kb_ref, vt_ref, bias_ref, o_ref, npair):
    i, nchs, mask_ref, q2_ref = job.i, job.nchs, job.mask, job.q2
    tpc = CW // BLOCK
    out_rows = pl.ds(pl.multiple_of(i * BLOCK, BLOCK), BLOCK)
    first_half = lax.broadcasted_iota(I32, (LANES, BLOCK), 0) < HEAD_DIM
    for p in range(npair):
        lanes = slice(p * LANES, (p + 1) * LANES)
        logits = []
        for c in range(nchs):
            s = _dot_nt(kb_ref[0, c * CW:(c + 1) * CW, lanes], q2_ref[p])
            msk = mask_ref[c]
            per_head = []
            for hh in range(2):
                bias = jnp.concatenate(
                    [bias_ref[2 * p + hh, jnp.maximum(i - (c * tpc + t), 0)] for t in range(tpc)], axis=0)
                per_head.append(s[:, hh * BLOCK:(hh + 1) * BLOCK] + bias + msk)
            logits.append(per_head)
        pes = [[None, None] for _ in range(nchs)]
        for hh in range(2):
            m = None
            for c in range(nchs):
                mc = jnp.max(logits[c][hh], axis=0, keepdims=True)
                m = mc if m is None else jnp.maximum(m, mc)
            for c in range(nchs):
                pes[c][hh] = jnp.exp2(logits[c][hh] - m).astype(BF16)
        out_t = jnp.zeros((LANES + ONES_ROWS, 2 * BLOCK), F32)
        ones = jnp.ones((ONES_ROWS, CW), BF16)
        for c in range(nchs):
            vt1 = jnp.concatenate([vt_ref[0, lanes, c * CW:(c + 1) * CW], ones], axis=0)
            out_t = out_t + _dot(vt1, jnp.concatenate(pes[c], axis=1))
        o_t = out_t[:LANES] / out_t[LANES:LANES + 1]
        o_ref[0, out_rows, lanes] = jnp.where(first_half, o_t[:, :BLOCK], o_t[:, BLOCK:]).T.astype(BF16)


def _dsa_kernel(qi_ref, wi_ref, ki_ref, qb_ref, kb_ref, vt_ref, bias_ref, o_ref,
                qs_ref, wt_ref, q2_ref, sk_ref, mask_ref, *, topk, n_heads, n_blocks):
    j = pl.program_id(1)
    npair = n_heads // 2
    tpc = CW // BLOCK
    n_chunks = n_blocks // tpc
    blocks = (j, n_blocks - 1 - j)

    for slot, i in enumerate(blocks):
        rows = pl.ds(pl.multiple_of(i * BLOCK, BLOCK), BLOCK)
        qi = qi_ref[0, rows, :]
        lane_i = _lane_id_bf16(qi.shape)
        for h in range(IDX_HEADS):
            in_head = (lane_i >= h * IDX_DIM) & (lane_i < (h + 1) * IDX_DIM)
            qs_ref[slot, h * BLOCK:(h + 1) * BLOCK, :] = jnp.where(in_head, qi, jnp.zeros_like(qi))
        wt_ref[slot] = wi_ref[0, rows, :].T
        for p in range(npair):
            q2_ref[slot, p] = _stack_heads(qb_ref[0, rows, p * LANES:(p + 1) * LANES])

    for v in range(n_chunks // 2):
        @pl.when(j // tpc == v)
        def _(v=v):
            jobs = [_Job(i, nchs, slot, qs_ref, wt_ref, q2_ref, sk_ref, mask_ref)
                    for slot, (i, nchs) in enumerate(zip(blocks, (v + 1, n_chunks - v)))]
            for job in jobs:
                _dsa_scores(job, ki_ref)

            def search(it, answers):
                bit = lax.shift_left(jnp.int32(1), 31 - it)
                out = []
                for job, ans in zip(jobs, answers):
                    cand = _key_to_float(ans | bit)
                    n = _dsa_count(job, lambda x, c=cand: x >= c)
                    out.append(jnp.where(n >= float(topk), ans | bit, ans))
                return tuple(out)

            answers = lax.fori_loop(0, 32, search, tuple(jnp.zeros((1, BLOCK), I32) for _ in jobs))
            thrs = [jnp.where(ans == 0, -jnp.inf, _key_to_float(ans)) for ans in answers]
            surplus = [_dsa_select(job, thr, topk) for job, thr in zip(jobs, thrs)]

            @pl.when(functools.reduce(jnp.maximum, surplus) > 0.0)
            def _():
                for job, thr in zip(jobs, thrs):
                    _dsa_break_ties(job, thr, topk)

            for job in jobs:
                _dsa_attend(job, kb_ref, vt_ref, bias_ref, o_ref, npair)


def _dsa(qi, wi, ki, qb, kb, vt, bias, batch, seq, topk):
    bw = qb.shape[-1]
    nidx = qi.shape[-1]
    nb = seq // BLOCK
    n_heads = bw // HEAD_DIM
    npair = n_heads // 2
    assert nb % (2 * (CW // BLOCK)) == 0
    full = lambda width: pl.BlockSpec((1, seq, width), lambda b, j: (b, 0, 0))
    r3 = lambda a: a.reshape(batch, seq, a.shape[-1])
    vm = pltpu.VMEM
    out = pl.pallas_call(
        functools.partial(_dsa_kernel, topk=topk, n_heads=n_heads, n_blocks=nb),
        grid=(batch, nb // 2),
        in_specs=[full(nidx), full(LANES), full(nidx), full(bw), full(bw),
                  pl.BlockSpec((1, bw, seq), lambda b, j: (b, 0, 0)),
                  _const_spec((n_heads, nb, BLOCK, BLOCK))],
        out_specs=full(bw),
        out_shape=jax.ShapeDtypeStruct((batch, seq, bw), BF16),
        scratch_shapes=[vm((2, IDX_HEADS * BLOCK, nidx), BF16), vm((2, LANES, BLOCK), F32),
                        vm((2, npair, 2 * BLOCK, LANES), BF16),
                        vm((2, seq // CW, CW, BLOCK), F32), vm((2, seq // CW, CW, BLOCK), F32)],
        compiler_params=_cparams(2),
        name="dsa",
    )(r3(qi), r3(wi), r3(ki), r3(qb), r3(kb), vt, bias)
    return out.reshape(batch * seq, bw)


def _split_rows(w_out, parts):
    ws, r0 = [], 0
    for a in parts:
        ws.append(w_out[r0:r0 + a.shape[1]].astype(BF16))
        r0 += a.shape[1]
    return ws


def _odd_proj_weights(w_in, qw, kvw):
    d = w_in.shape[0]
    dup = lambda w: jnp.repeat(w.reshape(d, kvw // HEAD_DIM, 1, HEAD_DIM), 2, axis=2).reshape(d, 2 * kvw)
    return jnp.concatenate([w_in[:, :qw], dup(w_in[:, qw:qw + kvw]), dup(w_in[:, qw + kvw:])], axis=1).astype(BF16)


def kernel(x, norm_g, final_g, rel_bias_table, ffn_w1, ffn_w3, ffn_w2, hyb_w_in, hyb_cq_g, hyb_wq_b,
           hyb_wq_idx, hyb_w_out, swa_w_in, swa_sinks, swa_w_out):
    batch, seq, d = x.shape
    depth = norm_g.shape[0]
    n_heads = d // HEAD_DIM
    a_heads = n_heads // 2
    topk = min(TOPK_MAX, seq // 4)
    assert seq % (BLOCK * 16) == 0 and seq % CW == 0 and (batch * seq) % TM_FFN == 0 and seq % TM == 0

    bias_a = _build_bias(
        rel_bias_table,
        np.concatenate([_band_bucket_idx(dil, win // dil) for win, dil in A_BRANCHES], axis=0),
        0, a_heads).reshape(a_heads // 2, 2, 3, BLOCK, 2 * BLOCK).transpose(2, 0, 1, 3, 4)
    bias_a = bias_a.reshape(3, a_heads // 2, 2 * BLOCK, 2 * BLOCK)
    bias_b = _build_bias(rel_bias_table, _dense_bucket_idx(seq), a_heads, n_heads - a_heads)
    bias_b = bias_b.reshape(n_heads - a_heads, seq // BLOCK, BLOCK, BLOCK)
    bias_c = _build_bias(rel_bias_table, _band_bucket_idx(1, C_WINDOW - 1), 0, n_heads)

    ffn_w = (ffn_w1.astype(BF16), ffn_w3.astype(BF16), ffn_w2.astype(BF16))
    h = x.reshape(batch * seq, d)
    for layer in range(depth):
        g = norm_g[layer]
        i = layer // 2
        ffn_a = (g[0], ffn_w, (layer, 0))
        ffn_b = (g[2], ffn_w, (layer, 1))
        if layer % 2 == 0:
            h = _ffn(h, *ffn_a, final_g)
            outs = _proj_even(h, g[1], hyb_w_in[i], hyb_cq_g[i], hyb_wq_b[i], hyb_wq_idx[i], batch, seq)
            o_a = _attn_a(outs[:9], bias_a, batch, seq)
            qb, kb, vt, qi, ki, wi = outs[9:]
            o_b = _dsa(qi, wi, ki, qb, kb, vt, bias_b, batch, seq, topk)
            mix, w_out = [o_a.reshape(batch * seq, -1), o_b], hyb_w_out[i]
        else:
            kvw = (swa_w_in.shape[-1] - d) // 2
            n_kv = kvw // HEAD_DIM
            group = n_heads // n_kv
            h, q, k2, v2 = _ffn(h, *ffn_a, g[1], post_w=_odd_proj_weights(swa_w_in[i], d, kvw),
                                post_splits=((d, Q_SCALE), (2 * kvw, 1.0), (2 * kvw, 1.0)))
            r3 = lambda a: a.reshape(batch, seq, a.shape[-1])
            o_c = _attn_c(r3(q), r3(k2), r3(v2), bias_c.reshape(n_kv, group * BLOCK, 2 * BLOCK),
                          swa_sinks[i].astype(F32), batch, seq, group)
            mix, w_out = [o_c.reshape(batch * seq, -1)], swa_w_out[i]
        h = _ffn(h, *ffn_b, final_g, mix=mix, mix_w=_split_rows(w_out, mix), final=layer == depth - 1)
    return h.reshape(batch, seq, d)
```

```python
import functools
import math

import numpy as np
import jax
import jax.numpy as jnp
from jax import lax
from jax.experimental import pallas as pl
from jax.experimental.pallas import tpu as pltpu

F32 = jnp.float32
BF16 = jnp.bfloat16
I32 = jnp.int32

EPS = 1e-6
HEAD_DIM = 64
BLOCK = 128
LANES = 128
NUM_BUCKETS = 32
MAX_DISTANCE = 2048
A_BRANCHES = ((128, 1), (512, 4), (2048, 16))
C_WINDOW = 128
TOPK_MAX = 256
IDX_HEADS = 8
IDX_DIM = 32
NEG = -1e30
LOG2E = math.log2(math.e)
Q_SCALE = HEAD_DIM ** -0.5 * LOG2E
INT_MIN = -(2 ** 31)
VMEM_LIMIT = 56 * 1024 * 1024

TM = 1024
TM_FFN = 1024
CW = 2 * BLOCK
DSA_BLOCKS = 4
C_UNROLL = 5
ONES_ROWS = 16


def _cparams(n_axes):
    return pltpu.CompilerParams(
        dimension_semantics=("arbitrary",) * n_axes, vmem_limit_bytes=VMEM_LIMIT)


def _rms(x, g):
    return x * lax.rsqrt(jnp.mean(x * x, axis=-1, keepdims=True) + EPS) * g


def _dot(a, b):
    return jnp.dot(a, b, preferred_element_type=F32)


def _dot_nt(a, b):
    return lax.dot_general(a, b, (((1,), (1,)), ((), ())), preferred_element_type=F32)


def _lane_id_bf16(shape):
    return lax.broadcasted_iota(I32, shape, 1).astype(F32).astype(BF16)


def _rows(start, size):
    if isinstance(start, int):
        return pl.ds(start, size)
    return pl.ds(pl.multiple_of(start, size), size)


def _const_spec(shape):
    nd = len(shape)
    return pl.BlockSpec(shape, lambda *_: (0,) * nd, pipeline_mode=pl.Buffered(1))


def _t5_bucket_np(dist):
    dist = np.maximum(dist, 0)
    max_exact = NUM_BUCKETS // 2
    d_f = np.maximum(dist, 1).astype(np.float32)
    large = max_exact + (np.log(d_f / max_exact) / math.log(MAX_DISTANCE / max_exact)
                         * (NUM_BUCKETS - max_exact)).astype(np.int32)
    large = np.minimum(large, NUM_BUCKETS - 1)
    return np.where(dist < max_exact, dist, large).astype(np.int32)


def _band_bucket_idx(dil, max_dist):
    loc = np.arange(BLOCK)[:, None] + BLOCK - np.arange(2 * BLOCK)[None, :]
    valid = (loc >= 0) & (loc <= max_dist)
    return np.where(valid, _t5_bucket_np(dil * loc), -1).astype(np.int32)


def _dense_bucket_idx(seq):
    nb = seq // BLOCK
    d = np.arange(nb)[:, None, None] * BLOCK - np.arange(BLOCK)[None, :, None] + np.arange(BLOCK)[None, None, :]
    return np.where(d >= 0, _t5_bucket_np(d), -1).astype(np.int32).reshape(seq, BLOCK)


def _bias_kernel(tab_ref, idx_ref, o_ref, *, head0, buckets):
    h = pl.program_id(0) + head0
    for blk, present in enumerate(buckets):
        rows = slice(blk * BLOCK, (blk + 1) * BLOCK)
        idx = idx_ref[rows, :]
        out = jnp.full(idx.shape, NEG, F32)
        for b in present:
            out = jnp.where(idx == b, tab_ref[b, h] * LOG2E, out)
        o_ref[0, rows, :] = out


def _build_bias(table, idx_np, head0, n_heads):
    r, c = idx_np.shape
    buckets = tuple(tuple(int(b) for b in np.unique(idx_np[r0:r0 + BLOCK]) if b >= 0) for r0 in range(0, r, BLOCK))
    return pl.pallas_call(
        functools.partial(_bias_kernel, head0=head0, buckets=buckets),
        grid=(n_heads,),
        in_specs=[pl.BlockSpec(memory_space=pltpu.SMEM),
                  pl.BlockSpec((r, c), lambda h: (0, 0))],
        out_specs=pl.BlockSpec((1, r, c), lambda h: (h, 0, 0)),
        out_shape=jax.ShapeDtypeStruct((n_heads, r, c), F32),
        compiler_params=_cparams(1),
        name="bias_build",
    )(table.astype(F32), jnp.asarray(idx_np))


def _ffn_kernel(*refs, n_mix, final, post_splits):
    it = iter(refs)
    x_ref = next(it)
    mix = [next(it) for _ in range(n_mix)]
    mix_w = [next(it) for _ in range(n_mix)]
    g_ref, w1_ref, w3_ref, w2_ref, g2_ref = (next(it) for _ in range(5))
    wp_ref = next(it) if post_splits else None
    o_ref = next(it)
    post_refs = [next(it) for _ in post_splits]
    x = x_ref[...]
    for a_ref, w_ref in zip(mix, mix_w):
        x = x + _dot(a_ref[...], w_ref[...])
    xn = _rms(x, g_ref[...]).astype(BF16)
    h1 = _dot(xn, w1_ref[...])
    h3 = _dot(xn, w3_ref[...])
    act = (h1 * (1.0 / (1.0 + jnp.exp(-h1)))) * h3
    y = _dot(act.astype(BF16), w2_ref[...])
    out = x + 0.5 * y
    if final:
        out = _rms(out, g2_ref[...])
    o_ref[...] = out
    if post_splits:
        p = _dot(_rms(out, g2_ref[...]).astype(BF16), wp_ref[...])
        c0 = 0
        for p_ref, (width, scale) in zip(post_refs, post_splits):
            blk = p[:, c0:c0 + width]
            p_ref[...] = (blk if scale == 1.0 else blk * scale).astype(BF16)
            c0 += width


def _ffn(h, g, ffn_w, which, g2, *, mix=(), mix_w=(), final=False, post_w=None, post_splits=()):
    t, d = h.shape
    dff = ffn_w[0].shape[-1]
    tok = lambda width: pl.BlockSpec((TM_FFN, width), lambda i: (i, 0))
    pick = lambda rows, cols: pl.BlockSpec((None, None, rows, cols), lambda i: (*which, 0, 0),
                                           pipeline_mode=pl.Buffered(1))
    sh = jax.ShapeDtypeStruct
    in_specs = [tok(d)] + [tok(a.shape[1]) for a in mix] + [_const_spec(w.shape) for w in mix_w]
    in_specs += [_const_spec((1, d)), pick(d, dff), pick(d, dff), pick(dff, d), _const_spec((1, d))]
    args = [h, *mix, *mix_w, g.reshape(1, d), *ffn_w, g2.reshape(1, d)]
    out_specs, out_shape = [tok(d)], [sh((t, d), F32)]
    if post_splits:
        in_specs.append(_const_spec(post_w.shape))
        args.append(post_w)
        out_specs += [tok(width) for width, _ in post_splits]
        out_shape += [sh((t, width), BF16) for width, _ in post_splits]
    res = pl.pallas_call(
        functools.partial(_ffn_kernel, n_mix=len(mix), final=final, post_splits=tuple(post_splits)),
        grid=(t // TM_FFN,),
        in_specs=in_specs,
        out_specs=out_specs,
        out_shape=out_shape,
        compiler_params=_cparams(1),
        name="ffn",
    )(*args)
    return res if post_splits else res[0]


def _fold_store(scr, scr4, blk, outs, n_slab):
    o1, o4, o16 = outs
    o1[...] = blk.astype(BF16)
    for c in range(n_slab):
        scr[c] = blk[:, c * LANES:(c + 1) * LANES]
    rows4, rows16 = TM // 4, TM // 16
    for r4 in range(4):
        for c in range(n_slab):
            cols = slice(c * LANES, (c + 1) * LANES)
            f4 = scr[c, pl.ds(r4, rows4, stride=4), :]
            o4[0, r4, :, cols] = f4.astype(BF16)
            scr4[c] = f4
            for q in range(4):
                o16[0, 4 * q + r4, :, cols] = scr4[c, pl.ds(q, rows16, stride=4), :].astype(BF16)


def _proj_even_kernel(h_ref, g_ref, w_ref, cqg_ref, wqb_ref, wqi_ref,
                      q1, k1, v1, q4, k4, v4, q16, k16, v16, qb, kb, vb, qi, ki, wi, scr, scr4,
                      *, aw, bw, rank):
    u = _rms(h_ref[...], g_ref[...]).astype(BF16)
    p = _dot(u, w_ref[...])
    proj = lambda c0, width: p[:, c0:c0 + width]
    n_slab = aw // LANES
    scale = Q_SCALE
    _fold_store(scr, scr4, proj(0, aw) * scale, (q1, q4, q16), n_slab)
    _fold_store(scr, scr4, proj(aw, aw), (k1, k4, k16), n_slab)
    _fold_store(scr, scr4, proj(2 * aw, aw), (v1, v4, v16), n_slab)
    c0 = 3 * aw
    kb[...] = proj(c0, bw).astype(BF16)
    vb[0] = proj(c0 + bw, bw).T.astype(BF16)
    c1 = c0 + 2 * bw
    cq = _rms(proj(c1, rank), cqg_ref[...]).astype(BF16)
    qb[...] = (_dot(cq, wqb_ref[...]) * scale).astype(BF16)
    qi[...] = _dot(cq, wqi_ref[...]).astype(BF16)
    c2 = c1 + rank
    nidx = IDX_HEADS * IDX_DIM
    ki[...] = proj(c2, nidx).astype(BF16)
    idx_scale = (IDX_DIM ** -0.5) * (IDX_HEADS ** -0.5)
    wi[...] = proj(c2 + nidx, LANES) * idx_scale


def _proj_even(h, g, w_in, cq_g, wq_b, wq_idx, batch, seq):
    t, d = h.shape
    bw = wq_b.shape[1]
    rank = wq_b.shape[0]
    aw = (w_in.shape[1] - rank - 2 * bw - IDX_DIM - IDX_HEADS) // 3
    nidx = IDX_HEADS * IDX_DIM
    offs = np.cumsum([0, aw, aw, aw, rank, bw, bw, IDX_DIM, IDX_HEADS])
    col = lambda i: w_in[:, offs[i]:offs[i + 1]]
    w_kidx = jnp.tile(col(6), (1, IDX_HEADS))
    w_widx = jnp.pad(col(7), ((0, 0), (0, LANES - IDX_HEADS)))
    w_all = jnp.concatenate([col(0), col(1), col(2), col(4), col(5), col(3), w_kidx, w_widx], axis=1).astype(BF16)
    ncol = w_all.shape[1]
    spt = seq // TM
    tok = lambda width: pl.BlockSpec((TM, width), lambda i: (i, 0))
    fold = lambda dil: pl.BlockSpec((1, dil, TM // dil, aw), lambda i: (i // spt, 0, i % spt, 0))
    sh = jax.ShapeDtypeStruct
    nat = sh((t, aw), BF16)
    f4 = sh((batch, 4, seq // 4, aw), BF16)
    f16 = sh((batch, 16, seq // 16, aw), BF16)
    return pl.pallas_call(
        functools.partial(_proj_even_kernel, aw=aw, bw=bw, rank=rank),
        grid=(t // TM,),
        in_specs=[tok(d), _const_spec((1, d)), _const_spec((d, ncol)), _const_spec((1, rank)),
                  _const_spec((rank, bw)), _const_spec((rank, nidx))],
        out_specs=[tok(aw), tok(aw), tok(aw), fold(4), fold(4), fold(4), fold(16), fold(16), fold(16),
                   tok(bw), tok(bw), pl.BlockSpec((1, bw, TM), lambda i: (i // spt, 0, i % spt)),
                   tok(nidx), tok(nidx), tok(LANES)],
        out_shape=[nat, nat, nat, f4, f4, f4, f16, f16, f16,
                   sh((t, bw), BF16), sh((t, bw), BF16), sh((batch, bw, seq), BF16),
                   sh((t, nidx), BF16), sh((t, nidx), BF16), sh((t, LANES), F32)],
        scratch_shapes=[pltpu.VMEM((aw // LANES, TM, LANES), F32), pltpu.VMEM((aw // LANES, TM // 4, LANES), F32)],
        compiler_params=_cparams(1),
        name="proj_even",
    )(h, g.reshape(1, d), w_all, cq_g.reshape(1, rank), wq_b.astype(BF16), wq_idx.astype(BF16))


def _stack_heads(q):
    lane_b = _lane_id_bf16(q.shape)
    z = jnp.zeros_like(q)
    return jnp.concatenate([jnp.where(lane_b < HEAD_DIM, q, z), jnp.where(lane_b >= HEAD_DIM, q, z)], axis=0)


def _band_block(q2, kw, vw, bias2):
    s = _dot_nt(q2, kw) + bias2
    m = jnp.max(s, axis=-1, keepdims=True)
    p = jnp.exp2(s - m)
    den = jnp.sum(p, axis=-1, keepdims=True)
    return _dot(p.astype(BF16), vw), m, den


def _attn_a_kernel(q1, k1, v1, q4, k4, v4, q16, k16, v16, bias_ref, o_ref, acc_s, m_s, den_s, *, seq):
    lo = lax.broadcasted_iota(I32, (BLOCK, LANES), 1) < HEAD_DIM
    shape = (BLOCK, LANES)
    refs = ((q1, k1, v1), (q4, k4, v4), (q16, k16, v16))
    for br, (_, dil) in enumerate(A_BRANCHES):
        qr, kr, vr = refs[br]
        for r in range(dil):
            at = (lambda ref, s, n: ref[0, pl.ds(s, n), :]) if dil == 1 else \
                 (lambda ref, s, n, r=r: ref[0, r, pl.ds(s, n), :])
            for n in range(seq // dil // BLOCK):
                q2 = _stack_heads(at(qr, n * BLOCK, BLOCK))
                if n == 0:
                    acc, m, den = _band_block(q2, at(kr, 0, BLOCK), at(vr, 0, BLOCK), bias_ref[br, 0, :, BLOCK:])
                else:
                    k0 = (n - 1) * BLOCK
                    acc, m, den = _band_block(q2, at(kr, k0, 2 * BLOCK), at(vr, k0, 2 * BLOCK), bias_ref[br, 0])
                rows = pl.ds(n * BLOCK, BLOCK) if dil == 1 else pl.ds(n * BLOCK * dil + r, BLOCK, stride=dil)
                acc_s[br, rows, :] = jnp.where(lo, acc[:BLOCK], acc[BLOCK:])
                m_s[br, rows, :] = jnp.where(lo, jnp.broadcast_to(m[:BLOCK], shape), jnp.broadcast_to(m[BLOCK:], shape))
                den_s[br, rows, :] = jnp.where(lo, jnp.broadcast_to(den[:BLOCK], shape),
                                               jnp.broadcast_to(den[BLOCK:], shape))

    chunk = 2 * BLOCK

    def merge(c, carry):
        rows = pl.ds(pl.multiple_of(c * chunk, chunk), chunk)
        ms = [m_s[br, rows, :] for br in range(3)]
        mx = jnp.maximum(jnp.maximum(ms[0], ms[1]), ms[2])
        num = jnp.zeros((chunk, LANES), F32)
        den = jnp.zeros((chunk, LANES), F32)
        for br in range(3):
            e = jnp.exp2(ms[br] - mx)
            num = num + e * acc_s[br, rows, :]
            den = den + e * den_s[br, rows, :]
        o_ref[0, rows, :] = (num / den).astype(BF16)
        return carry

    lax.fori_loop(0, seq // chunk, merge, 0)


def _attn_a(qkv, bias, batch, seq):
    q1, k1, v1, q4, k4, v4, q16, k16, v16 = qkv
    aw = q1.shape[-1]
    npair = aw // LANES
    nat = pl.BlockSpec((1, seq, LANES), lambda b, p: (b, 0, p))
    fold = lambda dil: pl.BlockSpec((1, dil, seq // dil, LANES), lambda b, p: (b, 0, 0, p))
    r3 = lambda a: a.reshape(batch, seq, aw)
    return pl.pallas_call(
        functools.partial(_attn_a_kernel, seq=seq),
        grid=(batch, npair),
        in_specs=[nat, nat, nat, fold(4), fold(4), fold(4), fold(16), fold(16), fold(16),
                  pl.BlockSpec((3, 1, 2 * BLOCK, 2 * BLOCK), lambda b, p: (0, p, 0, 0))],
        out_specs=nat,
        out_shape=jax.ShapeDtypeStruct((batch, seq, aw), BF16),
        scratch_shapes=[pltpu.VMEM((3, seq, LANES), F32)] * 3,
        compiler_params=_cparams(2),
        name="attn_dilated",
    )(r3(q1), r3(k1), r3(v1), q4, k4, v4, q16, k16, v16, bias)


def _attn_c_kernel(sink_ref, q_ref, k_ref, v_ref, bias_ref, o_ref, *, seq, group):
    g = pl.program_id(1)
    npair = group // 2
    lo = lax.broadcasted_iota(I32, (BLOCK, LANES), 1) < HEAD_DIM

    def block(q0, k0, width, bias2):
        qn = q_ref[0, pl.ds(q0, BLOCK), :]
        q2 = jnp.concatenate([_stack_heads(qn[:, p * LANES:(p + 1) * LANES]) for p in range(npair)], axis=0)
        acc, m, den = _band_block(q2, k_ref[0, pl.ds(k0, width), :], v_ref[0, pl.ds(k0, width), :], bias2)
        for p in range(npair):
            outs = []
            for hh in range(2):
                h = 2 * p + hh
                r = slice(h * BLOCK, (h + 1) * BLOCK)
                outs.append(acc[r] / (den[r] + jnp.exp2(sink_ref[g * group + h] * LOG2E - m[r])))
            o_ref[0, pl.ds(q0, BLOCK), p * LANES:(p + 1) * LANES] = jnp.where(lo, outs[0], outs[1]).astype(BF16)

    block(0, 0, BLOCK, bias_ref[0, :, BLOCK:])

    def body(n, carry):
        block(pl.multiple_of(n * BLOCK, BLOCK), pl.multiple_of((n - 1) * BLOCK, BLOCK), 2 * BLOCK, bias_ref[0])
        return carry

    lax.fori_loop(1, seq // BLOCK, body, 0, unroll=C_UNROLL)


def _attn_c(q, k2, v2, bias, sinks, batch, seq, group):
    qw = q.shape[-1]
    n_kv = k2.shape[-1] // LANES
    gw = qw // n_kv
    return pl.pallas_call(
        functools.partial(_attn_c_kernel, seq=seq, group=group),
        grid=(batch, n_kv),
        in_specs=[pl.BlockSpec(memory_space=pltpu.SMEM),
                  pl.BlockSpec((1, seq, gw), lambda b, g: (b, 0, g)),
                  pl.BlockSpec((1, seq, LANES), lambda b, g: (b, 0, g)),
                  pl.BlockSpec((1, seq, LANES), lambda b, g: (b, 0, g)),
                  pl.BlockSpec((1, group * BLOCK, 2 * BLOCK), lambda b, g: (g, 0, 0))],
        out_specs=pl.BlockSpec((1, seq, gw), lambda b, g: (b, 0, g)),
        out_shape=jax.ShapeDtypeStruct((batch, seq, qw), BF16),
        compiler_params=_cparams(2),
        name="attn_swa",
    )(sinks, q, k2, v2, bias)


class _Job:
    def __init__(self, i, nchs, slot, qs_ref, wt_ref, q2_ref, sk_ref, mask_ref):
        self.i, self.nchs = i, nchs
        self.qs, self.wt, self.q2 = qs_ref.at[slot], wt_ref.at[slot], q2_ref.at[slot]
        self.sk, self.mask = sk_ref.at[slot], mask_ref.at[slot]
        key_pos = lax.broadcasted_iota(I32, (CW, BLOCK), 0) + (nchs - 1) * CW
        qry_pos = lax.broadcasted_iota(I32, (CW, BLOCK), 1) + i * BLOCK
        self.causal_last = key_pos <= qry_pos


def _dsa_scores(job, ki_ref):
    for c in range(job.nchs):
        st = _dot_nt(ki_ref[0, c * CW:(c + 1) * CW, :], job.qs[...])
        score = None
        for h in range(IDX_HEADS):
            term = jnp.maximum(st[:, h * BLOCK:(h + 1) * BLOCK], 0.0) * job.wt[h:h + 1, :]
            score = term if score is None else score + term
        if c == job.nchs - 1:
            score = jnp.where(job.causal_last, score, -jnp.inf)
        job.sk[c] = score


def _key_to_float(key):
    sk = key ^ INT_MIN
    return lax.bitcast_convert_type(sk ^ (lax.shift_right_arithmetic(sk, 31) & 0x7FFFFFFF), F32)


def _dsa_count(job, pred):
    grp = 8 * 8
    acc = jnp.zeros((grp, BLOCK), F32)
    for c in range(job.nchs):
        for g in range(CW // grp):
            acc = jnp.where(pred(job.sk[c, g * grp:(g + 1) * grp, :]), acc + 1.0, acc)
    return jnp.sum(acc, axis=0, keepdims=True)


def _dsa_select(job, thr, topk):
    for c in range(job.nchs):
        sel = job.sk[c] >= thr
        if c == job.nchs - 1:
            sel = sel & job.causal_last
        job.mask[c] = jnp.where(sel, 0.0, NEG)
    n_ge = _dsa_count(job, lambda x: x >= thr)
    return jnp.max(jnp.where((n_ge > float(topk)) & (thr > -jnp.inf), 1.0, 0.0))


def _dsa_break_ties(job, thr, topk):
    need = float(topk) - _dsa_count(job, lambda x: x > thr)
    r_i = lax.broadcasted_iota(I32, (BLOCK, BLOCK), 0)
    c_i = lax.broadcasted_iota(I32, (BLOCK, BLOCK), 1)
    lower = jnp.where(c_i <= r_i, 1.0, 0.0).astype(BF16)
    carry = jnp.zeros((1, BLOCK), F32)
    for c in range(job.nchs):
        for t in range(CW // BLOCK):
            rows = slice(t * BLOCK, (t + 1) * BLOCK)
            xs = job.sk[c, rows, :]
            eq = xs == thr
            eqf = jnp.where(eq, 1.0, 0.0)
            rank = _dot(lower, eqf.astype(BF16)) + carry
            keep = (xs > thr) | (eq & (rank <= need))
            causal = r_i + (c * CW + t * BLOCK) <= c_i + job.i * BLOCK
            job.mask[c, rows, :] = jnp.where(keep & causal, 0.0, NEG)
            carry = carry + jnp.sum(eqf, axis=0, keepdims=True)


def _dsa_attend(job, kb_ref, vt_ref, bias_ref, o_ref, npair):
    i, nchs, mask_ref, q2_ref = job.i, job.nchs, job.mask, job.q2
    tpc = CW // BLOCK
    out_rows = pl.ds(pl.multiple_of(i * BLOCK, BLOCK), BLOCK)
    first_half = lax.broadcasted_iota(I32, (LANES, BLOCK), 0) < HEAD_DIM
    for p in range(npair):
        lanes = slice(p * LANES, (p + 1) * LANES)
        logits = []
        for c in range(nchs):
            s = _dot_nt(kb_ref[0, c * CW:(c + 1) * CW, lanes], q2_ref[p])
            msk = mask_ref[c]
            per_head = []
            for hh in range(2):
                bias = jnp.concatenate(
                    [bias_ref[2 * p + hh, jnp.maximum(i - (c * tpc + t), 0)] for t in range(tpc)], axis=0)
                per_head.append(s[:, hh * BLOCK:(hh + 1) * BLOCK] + bias + msk)
            logits.append(per_head)
        pes = [[None, None] for _ in range(nchs)]
        for hh in range(2):
            m = None
            for c in range(nchs):
                mc = jnp.max(logits[c][hh], axis=0, keepdims=True)
                m = mc if m is None else jnp.maximum(m, mc)
            for c in range(nchs):
                pes[c][hh] = jnp.exp2(logits[c][hh] - m).astype(BF16)
        out_t = jnp.zeros((LANES + ONES_ROWS, 2 * BLOCK), F32)
        ones = jnp.ones((ONES_ROWS, CW), BF16)
        for c in range(nchs):
            vt1 = jnp.concatenate([vt_ref[0, lanes, c * CW:(c + 1) * CW], ones], axis=0)
            out_t = out_t + _dot(vt1, jnp.concatenate(pes[c], axis=1))
        o_t = out_t[:LANES] / out_t[LANES:LANES + 1]
        o_ref[0, out_rows, lanes] = jnp.where(first_half, o_t[:, :BLOCK], o_t[:, BLOCK:]).T.astype(BF16)


def _dsa_kernel(qi_ref, wi_ref, ki_ref, qb_ref, kb_ref, vt_ref, bias_ref, o_ref,
                qs_ref, wt_ref, q2_ref, sk_ref, mask_ref, *, topk, n_heads, n_blocks):
    j = pl.program_id(1)
    npair = n_heads // 2
    tpc = CW // BLOCK
    span = 2 * n_blocks // DSA_BLOCKS
    blocks, counts = [], []
    for base in range(0, n_blocks, span):
        blocks += [base + j, base + span - 1 - j]
        counts.append(lambda v, base=base: (base // tpc + v + 1, (base + span) // tpc - v))

    for slot, i in enumerate(blocks):
        rows = pl.ds(pl.multiple_of(i * BLOCK, BLOCK), BLOCK)
        qi = qi_ref[0, rows, :]
        lane_i = _lane_id_bf16(qi.shape)
        for h in range(IDX_HEADS):
            in_head = (lane_i >= h * IDX_DIM) & (lane_i < (h + 1) * IDX_DIM)
            qs_ref[slot, h * BLOCK:(h + 1) * BLOCK, :] = jnp.where(in_head, qi, jnp.zeros_like(qi))
        wt_ref[slot] = wi_ref[0, rows, :].T
        for p in range(npair):
            q2_ref[slot, p] = _stack_heads(qb_ref[0, rows, p * LANES:(p + 1) * LANES])

    for v in range(span // (2 * tpc)):
        @pl.when(j // tpc == v)
        def _(v=v):
            nchs_all = [n for count in counts for n in count(v)]
            jobs = [_Job(i, nchs, slot, qs_ref, wt_ref, q2_ref, sk_ref, mask_ref)
                    for slot, (i, nchs) in enumerate(zip(blocks, nchs_all))]
            for job in jobs:
                _dsa_scores(job, ki_ref)

            def search(it, answers):
                bit = lax.shift_left(jnp.int32(1), 31 - it)
                out = []
                for job, ans in zip(jobs, answers):
                    cand = _key_to_float(ans | bit)
                    n = _dsa_count(job, lambda x, c=cand: x >= c)
                    out.append(jnp.where(n >= float(topk), ans | bit, ans))
                return tuple(out)

            answers = lax.fori_loop(0, 32, search, tuple(jnp.zeros((1, BLOCK), I32) for _ in jobs))
            thrs = [jnp.where(ans == 0, -jnp.inf, _key_to_float(ans)) for ans in answers]
            surplus = [_dsa_select(job, thr, topk) for job, thr in zip(jobs, thrs)]

            @pl.when(functools.reduce(jnp.maximum, surplus) > 0.0)
            def _():
                for job, thr in zip(jobs, thrs):
                    _dsa_break_ties(job, thr, topk)

            for job in jobs:
                _dsa_attend(job, kb_ref, vt_ref, bias_ref, o_ref, npair)


def _dsa(qi, wi, ki, qb, kb, vt, bias, batch, seq, topk):
    bw = qb.shape[-1]
    nidx = qi.shape[-1]
    nb = seq // BLOCK
    n_heads = bw // HEAD_DIM
    npair = n_heads // 2
    assert (2 * nb // DSA_BLOCKS) % (2 * (CW // BLOCK)) == 0
    full = lambda width: pl.BlockSpec((1, seq, width), lambda b, j: (b, 0, 0))
    r3 = lambda a: a.reshape(batch, seq, a.shape[-1])
    vm = lambda shape, dtype: pltpu.VMEM((DSA_BLOCKS,) + shape, dtype)
    out = pl.pallas_call(
        functools.partial(_dsa_kernel, topk=topk, n_heads=n_heads, n_blocks=nb),
        grid=(batch, nb // DSA_BLOCKS),
        in_specs=[full(nidx), full(LANES), full(nidx), full(bw), full(bw),
                  pl.BlockSpec((1, bw, seq), lambda b, j: (b, 0, 0)),
                  _const_spec((n_heads, nb, BLOCK, BLOCK))],
        out_specs=full(bw),
        out_shape=jax.ShapeDtypeStruct((batch, seq, bw), BF16),
        scratch_shapes=[vm((IDX_HEADS * BLOCK, nidx), BF16), vm((LANES, BLOCK), F32),
                        vm((npair, 2 * BLOCK, LANES), BF16),
                        vm((seq // CW, CW, BLOCK), F32), vm((seq // CW, CW, BLOCK), F32)],
        compiler_params=_cparams(2),
        name="dsa",
    )(r3(qi), r3(wi), r3(ki), r3(qb), r3(kb), vt, bias)
    return out.reshape(batch * seq, bw)


def _split_rows(w_out, parts):
    ws, r0 = [], 0
    for a in parts:
        ws.append(w_out[r0:r0 + a.shape[1]].astype(BF16))
        r0 += a.shape[1]
    return ws


def _odd_proj_weights(w_in, qw, kvw):
    d = w_in.shape[0]
    dup = lambda w: jnp.repeat(w.reshape(d, kvw // HEAD_DIM, 1, HEAD_DIM), 2, axis=2).reshape(d, 2 * kvw)
    return jnp.concatenate([w_in[:, :qw], dup(w_in[:, qw:qw + kvw]), dup(w_in[:, qw + kvw:])], axis=1).astype(BF16)


def kernel(x, norm_g, final_g, rel_bias_table, ffn_w1, ffn_w3, ffn_w2, hyb_w_in, hyb_cq_g, hyb_wq_b,
           hyb_wq_idx, hyb_w_out, swa_w_in, swa_sinks, swa_w_out):
    batch, seq, d = x.shape
    depth = norm_g.shape[0]
    n_heads = d // HEAD_DIM
    a_heads = n_heads // 2
    topk = min(TOPK_MAX, seq // 4)
    assert seq % (BLOCK * 16) == 0 and seq % CW == 0 and (batch * seq) % TM_FFN == 0 and seq % TM == 0

    bias_a = _build_bias(
        rel_bias_table,
        np.concatenate([_band_bucket_idx(dil, win // dil) for win, dil in A_BRANCHES], axis=0),
        0, a_heads).reshape(a_heads // 2, 2, 3, BLOCK, 2 * BLOCK).transpose(2, 0, 1, 3, 4)
    bias_a = bias_a.reshape(3, a_heads // 2, 2 * BLOCK, 2 * BLOCK)
    bias_b = _build_bias(rel_bias_table, _dense_bucket_idx(seq), a_heads, n_heads - a_heads)
    bias_b = bias_b.reshape(n_heads - a_heads, seq // BLOCK, BLOCK, BLOCK)
    bias_c = _build_bias(rel_bias_table, _band_bucket_idx(1, C_WINDOW - 1), 0, n_heads)

    ffn_w = (ffn_w1.astype(BF16), ffn_w3.astype(BF16), ffn_w2.astype(BF16))
    h = x.reshape(batch * seq, d)
    for layer in range(depth):
        g = norm_g[layer]
        i = layer // 2
        ffn_a = (g[0], ffn_w, (layer, 0))
        ffn_b = (g[2], ffn_w, (layer, 1))
        if layer % 2 == 0:
            h = _ffn(h, *ffn_a, final_g)
            outs = _proj_even(h, g[1], hyb_w_in[i], hyb_cq_g[i], hyb_wq_b[i], hyb_wq_idx[i], batch, seq)
            o_a = _attn_a(outs[:9], bias_a, batch, seq)
            qb, kb, vt, qi, ki, wi = outs[9:]
            o_b = _dsa(qi, wi, ki, qb, kb, vt, bias_b, batch, seq, topk)
            mix, w_out = [o_a.reshape(batch * seq, -1), o_b], hyb_w_out[i]
        else:
            kvw = (swa_w_in.shape[-1] - d) // 2
            n_kv = kvw // HEAD_DIM
            group = n_heads // n_kv
            h, q, k2, v2 = _ffn(h, *ffn_a, g[1], post_w=_odd_proj_weights(swa_w_in[i], d, kvw),
                                post_splits=((d, Q_SCALE), (2 * kvw, 1.0), (2 * kvw, 1.0)))
            r3 = lambda a: a.reshape(batch, seq, a.shape[-1])
            o_c = _attn_c(r3(q), r3(k2), r3(v2), bias_c.reshape(n_kv, group * BLOCK, 2 * BLOCK),
                          swa_sinks[i].astype(F32), batch, seq, group)
            mix, w_out = [o_c.reshape(batch * seq, -1)], swa_w_out[i]
        h = _ffn(h, *ffn_b, final_g, mix=mix, mix_w=_split_rows(w_out, mix), final=layer == depth - 1)
    return h.reshape(batch, seq, d)
```

```python
import functools
import math

import numpy as np
import jax
import jax.numpy as jnp
from jax import lax
from jax.experimental import pallas as pl
from jax.experimental.pallas import tpu as pltpu

F32 = jnp.float32
BF16 = jnp.bfloat16
I32 = jnp.int32

EPS = 1e-6
HEAD_DIM = 64
BLOCK = 128
LANES = 128
NUM_BUCKETS = 32
MAX_DISTANCE = 2048
A_BRANCHES = ((128, 1), (512, 4), (2048, 16))
C_WINDOW = 128
TOPK_MAX = 256
IDX_HEADS = 8
IDX_DIM = 32
NEG = -1e30
LOG2E = math.log2(math.e)
Q_SCALE = HEAD_DIM ** -0.5 * LOG2E
INT_MIN = -(2 ** 31)
VMEM_LIMIT = 56 * 1024 * 1024

TM = 1024
TM_FFN = 1024
CW = 2 * BLOCK
DSA_BLOCKS = 4
C_UNROLL = 15
ONES_ROWS = 16


def _cparams(n_axes):
    return pltpu.CompilerParams(
        dimension_semantics=("arbitrary",) * n_axes, vmem_limit_bytes=VMEM_LIMIT)


def _rms(x, g):
    return x * lax.rsqrt(jnp.mean(x * x, axis=-1, keepdims=True) + EPS) * g


def _dot(a, b):
    return jnp.dot(a, b, preferred_element_type=F32)


def _dot_nt(a, b):
    return lax.dot_general(a, b, (((1,), (1,)), ((), ())), preferred_element_type=F32)


def _lane_id_bf16(shape):
    return lax.broadcasted_iota(I32, shape, 1).astype(F32).astype(BF16)


def _rows(start, size):
    if isinstance(start, int):
        return pl.ds(start, size)
    return pl.ds(pl.multiple_of(start, size), size)


def _const_spec(shape):
    nd = len(shape)
    return pl.BlockSpec(shape, lambda *_: (0,) * nd, pipeline_mode=pl.Buffered(1))


def _t5_bucket_np(dist):
    dist = np.maximum(dist, 0)
    max_exact = NUM_BUCKETS // 2
    d_f = np.maximum(dist, 1).astype(np.float32)
    large = max_exact + (np.log(d_f / max_exact) / math.log(MAX_DISTANCE / max_exact)
                         * (NUM_BUCKETS - max_exact)).astype(np.int32)
    large = np.minimum(large, NUM_BUCKETS - 1)
    return np.where(dist < max_exact, dist, large).astype(np.int32)


def _band_bucket_idx(dil, max_dist):
    loc = np.arange(BLOCK)[:, None] + BLOCK - np.arange(2 * BLOCK)[None, :]
    valid = (loc >= 0) & (loc <= max_dist)
    return np.where(valid, _t5_bucket_np(dil * loc), -1).astype(np.int32)


def _dense_bucket_idx(seq):
    nb = seq // BLOCK
    d = np.arange(nb)[:, None, None] * BLOCK - np.arange(BLOCK)[None, :, None] + np.arange(BLOCK)[None, None, :]
    return np.where(d >= 0, _t5_bucket_np(d), -1).astype(np.int32).reshape(seq, BLOCK)


def _bias_kernel(tab_ref, idx_ref, o_ref, *, head0, buckets):
    h = pl.program_id(0) + head0
    for blk, present in enumerate(buckets):
        rows = slice(blk * BLOCK, (blk + 1) * BLOCK)
        idx = idx_ref[rows, :]
        out = jnp.full(idx.shape, NEG, F32)
        for b in present:
            out = jnp.where(idx == b, tab_ref[b, h] * LOG2E, out)
        o_ref[0, rows, :] = out


def _build_bias(table, idx_np, head0, n_heads):
    r, c = idx_np.shape
    buckets = tuple(tuple(int(b) for b in np.unique(idx_np[r0:r0 + BLOCK]) if b >= 0) for r0 in range(0, r, BLOCK))
    return pl.pallas_call(
        functools.partial(_bias_kernel, head0=head0, buckets=buckets),
        grid=(n_heads,),
        in_specs=[pl.BlockSpec(memory_space=pltpu.SMEM),
                  pl.BlockSpec((r, c), lambda h: (0, 0))],
        out_specs=pl.BlockSpec((1, r, c), lambda h: (h, 0, 0)),
        out_shape=jax.ShapeDtypeStruct((n_heads, r, c), F32),
        compiler_params=_cparams(1),
        name="bias_build",
    )(table.astype(F32), jnp.asarray(idx_np))


def _ffn_kernel(*refs, n_mix, final, post_splits):
    it = iter(refs)
    x_ref = next(it)
    mix = [next(it) for _ in range(n_mix)]
    mix_w = [next(it) for _ in range(n_mix)]
    g_ref, w1_ref, w3_ref, w2_ref, g2_ref = (next(it) for _ in range(5))
    wp_ref = next(it) if post_splits else None
    o_ref = next(it)
    post_refs = [next(it) for _ in post_splits]
    x = x_ref[...]
    for a_ref, w_ref in zip(mix, mix_w):
        x = x + _dot(a_ref[...], w_ref[...])
    xn = _rms(x, g_ref[...]).astype(BF16)
    h1 = _dot(xn, w1_ref[...])
    h3 = _dot(xn, w3_ref[...])
    act = (h1 * (1.0 / (1.0 + jnp.exp(-h1)))) * h3
    y = _dot(act.astype(BF16), w2_ref[...])
    out = x + 0.5 * y
    if final:
        out = _rms(out, g2_ref[...])
    o_ref[...] = out
    if post_splits:
        p = _dot(_rms(out, g2_ref[...]).astype(BF16), wp_ref[...])
        c0 = 0
        for p_ref, (width, scale) in zip(post_refs, post_splits):
            blk = p[:, c0:c0 + width]
            p_ref[...] = (blk if scale == 1.0 else blk * scale).astype(BF16)
            c0 += width


def _ffn(h, g, ffn_w, which, g2, *, mix=(), mix_w=(), final=False, post_w=None, post_splits=()):
    t, d = h.shape
    dff = ffn_w[0].shape[-1]
    tok = lambda width: pl.BlockSpec((TM_FFN, width), lambda i: (i, 0))
    pick = lambda rows, cols: pl.BlockSpec((None, None, rows, cols), lambda i: (*which, 0, 0),
                                           pipeline_mode=pl.Buffered(1))
    sh = jax.ShapeDtypeStruct
    in_specs = [tok(d)] + [tok(a.shape[1]) for a in mix] + [_const_spec(w.shape) for w in mix_w]
    in_specs += [_const_spec((1, d)), pick(d, dff), pick(d, dff), pick(dff, d), _const_spec((1, d))]
    args = [h, *mix, *mix_w, g.reshape(1, d), *ffn_w, g2.reshape(1, d)]
    out_specs, out_shape = [tok(d)], [sh((t, d), F32)]
    if post_splits:
        in_specs.append(_const_spec(post_w.shape))
        args.append(post_w)
        out_specs += [tok(width) for width, _ in post_splits]
        out_shape += [sh((t, width), BF16) for width, _ in post_splits]
    res = pl.pallas_call(
        functools.partial(_ffn_kernel, n_mix=len(mix), final=final, post_splits=tuple(post_splits)),
        grid=(t // TM_FFN,),
        in_specs=in_specs,
        out_specs=out_specs,
        out_shape=out_shape,
        compiler_params=_cparams(1),
        name="ffn",
    )(*args)
    return res if post_splits else res[0]


def _fold_store(scr, scr4, blk, outs, n_slab):
    o1, o4, o16 = outs
    o1[...] = blk.astype(BF16)
    for c in range(n_slab):
        scr[c] = blk[:, c * LANES:(c + 1) * LANES]
    rows4, rows16 = TM // 4, TM // 16
    for r4 in range(4):
        for c in range(n_slab):
            cols = slice(c * LANES, (c + 1) * LANES)
            f4 = scr[c, pl.ds(r4, rows4, stride=4), :]
            o4[0, r4, :, cols] = f4.astype(BF16)
            scr4[c] = f4
            for q in range(4):
                o16[0, 4 * q + r4, :, cols] = scr4[c, pl.ds(q, rows16, stride=4), :].astype(BF16)


def _proj_even_kernel(h_ref, g_ref, w_ref, cqg_ref, wqb_ref, wqi_ref,
                      q1, k1, v1, q4, k4, v4, q16, k16, v16, qb, kb, vb, qi, ki, wi, scr, scr4,
                      *, aw, bw, rank):
    u = _rms(h_ref[...], g_ref[...]).astype(BF16)
    p = _dot(u, w_ref[...])
    proj = lambda c0, width: p[:, c0:c0 + width]
    n_slab = aw // LANES
    scale = Q_SCALE
    _fold_store(scr, scr4, proj(0, aw) * scale, (q1, q4, q16), n_slab)
    _fold_store(scr, scr4, proj(aw, aw), (k1, k4, k16), n_slab)
    _fold_store(scr, scr4, proj(2 * aw, aw), (v1, v4, v16), n_slab)
    c0 = 3 * aw
    kb[...] = proj(c0, bw).astype(BF16)
    vb[0] = proj(c0 + bw, bw).T.astype(BF16)
    c1 = c0 + 2 * bw
    cq = _rms(proj(c1, rank), cqg_ref[...]).astype(BF16)
    qb[...] = (_dot(cq, wqb_ref[...]) * scale).astype(BF16)
    qi[...] = _dot(cq, wqi_ref[...]).astype(BF16)
    c2 = c1 + rank
    nidx = IDX_HEADS * IDX_DIM
    ki[...] = proj(c2, nidx).astype(BF16)
    idx_scale = (IDX_DIM ** -0.5) * (IDX_HEADS ** -0.5)
    wi[...] = proj(c2 + nidx, LANES) * idx_scale


def _proj_even(h, g, w_in, cq_g, wq_b, wq_idx, batch, seq):
    t, d = h.shape
    bw = wq_b.shape[1]
    rank = wq_b.shape[0]
    aw = (w_in.shape[1] - rank - 2 * bw - IDX_DIM - IDX_HEADS) // 3
    nidx = IDX_HEADS * IDX_DIM
    offs = np.cumsum([0, aw, aw, aw, rank, bw, bw, IDX_DIM, IDX_HEADS])
    col = lambda i: w_in[:, offs[i]:offs[i + 1]]
    w_kidx = jnp.tile(col(6), (1, IDX_HEADS))
    w_widx = jnp.pad(col(7), ((0, 0), (0, LANES - IDX_HEADS)))
    w_all = jnp.concatenate([col(0), col(1), col(2), col(4), col(5), col(3), w_kidx, w_widx], axis=1).astype(BF16)
    ncol = w_all.shape[1]
    spt = seq // TM
    tok = lambda width: pl.BlockSpec((TM, width), lambda i: (i, 0))
    fold = lambda dil: pl.BlockSpec((1, dil, TM // dil, aw), lambda i: (i // spt, 0, i % spt, 0))
    sh = jax.ShapeDtypeStruct
    nat = sh((t, aw), BF16)
    f4 = sh((batch, 4, seq // 4, aw), BF16)
    f16 = sh((batch, 16, seq // 16, aw), BF16)
    return pl.pallas_call(
        functools.partial(_proj_even_kernel, aw=aw, bw=bw, rank=rank),
        grid=(t // TM,),
        in_specs=[tok(d), _const_spec((1, d)), _const_spec((d, ncol)), _const_spec((1, rank)),
                  _const_spec((rank, bw)), _const_spec((rank, nidx))],
        out_specs=[tok(aw), tok(aw), tok(aw), fold(4), fold(4), fold(4), fold(16), fold(16), fold(16),
                   tok(bw), tok(bw), pl.BlockSpec((1, bw, TM), lambda i: (i // spt, 0, i % spt)),
                   tok(nidx), tok(nidx), tok(LANES)],
        out_shape=[nat, nat, nat, f4, f4, f4, f16, f16, f16,
                   sh((t, bw), BF16), sh((t, bw), BF16), sh((batch, bw, seq), BF16),
                   sh((t, nidx), BF16), sh((t, nidx), BF16), sh((t, LANES), F32)],
        scratch_shapes=[pltpu.VMEM((aw // LANES, TM, LANES), F32), pltpu.VMEM((aw // LANES, TM // 4, LANES), F32)],
        compiler_params=_cparams(1),
        name="proj_even",
    )(h, g.reshape(1, d), w_all, cq_g.reshape(1, rank), wq_b.astype(BF16), wq_idx.astype(BF16))


def _stack_heads(q):
    lane_b = _lane_id_bf16(q.shape)
    z = jnp.zeros_like(q)
    return jnp.concatenate([jnp.where(lane_b < HEAD_DIM, q, z), jnp.where(lane_b >= HEAD_DIM, q, z)], axis=0)


def _band_block(q2, kw, vw, bias2):
    s = _dot_nt(q2, kw) + bias2
    m = jnp.max(s, axis=-1, keepdims=True)
    p = jnp.exp2(s - m)
    den = jnp.sum(p, axis=-1, keepdims=True)
    return _dot(p.astype(BF16), vw), m, den


def _attn_a_kernel(q1, k1, v1, q4, k4, v4, q16, k16, v16, bias_ref, o_ref, acc_s, m_s, den_s, *, seq):
    lo = lax.broadcasted_iota(I32, (BLOCK, LANES), 1) < HEAD_DIM
    shape = (BLOCK, LANES)
    refs = ((q1, k1, v1), (q4, k4, v4), (q16, k16, v16))
    for br, (_, dil) in enumerate(A_BRANCHES):
        qr, kr, vr = refs[br]
        for r in range(dil):
            at = (lambda ref, s, n: ref[0, pl.ds(s, n), :]) if dil == 1 else \
                 (lambda ref, s, n, r=r: ref[0, r, pl.ds(s, n), :])
            for n in range(seq // dil // BLOCK):
                q2 = _stack_heads(at(qr, n * BLOCK, BLOCK))
                if n == 0:
                    acc, m, den = _band_block(q2, at(kr, 0, BLOCK), at(vr, 0, BLOCK), bias_ref[br, 0, :, BLOCK:])
                else:
                    k0 = (n - 1) * BLOCK
                    acc, m, den = _band_block(q2, at(kr, k0, 2 * BLOCK), at(vr, k0, 2 * BLOCK), bias_ref[br, 0])
                rows = pl.ds(n * BLOCK, BLOCK) if dil == 1 else pl.ds(n * BLOCK * dil + r, BLOCK, stride=dil)
                acc_s[br, rows, :] = jnp.where(lo, acc[:BLOCK], acc[BLOCK:])
                m_s[br, rows, :] = jnp.where(lo, jnp.broadcast_to(m[:BLOCK], shape), jnp.broadcast_to(m[BLOCK:], shape))
                den_s[br, rows, :] = jnp.where(lo, jnp.broadcast_to(den[:BLOCK], shape),
                                               jnp.broadcast_to(den[BLOCK:], shape))

    chunk = 2 * BLOCK

    def merge(c, carry):
        rows = pl.ds(pl.multiple_of(c * chunk, chunk), chunk)
        ms = [m_s[br, rows, :] for br in range(3)]
        mx = jnp.maximum(jnp.maximum(ms[0], ms[1]), ms[2])
        num = jnp.zeros((chunk, LANES), F32)
        den = jnp.zeros((chunk, LANES), F32)
        for br in range(3):
            e = jnp.exp2(ms[br] - mx)
            num = num + e * acc_s[br, rows, :]
            den = den + e * den_s[br, rows, :]
        o_ref[0, rows, :] = (num / den).astype(BF16)
        return carry

    lax.fori_loop(0, seq // chunk, merge, 0)


def _attn_a(qkv, bias, batch, seq):
    q1, k1, v1, q4, k4, v4, q16, k16, v16 = qkv
    aw = q1.shape[-1]
    npair = aw // LANES
    nat = pl.BlockSpec((1, seq, LANES), lambda b, p: (b, 0, p))
    fold = lambda dil: pl.BlockSpec((1, dil, seq // dil, LANES), lambda b, p: (b, 0, 0, p))
    r3 = lambda a: a.reshape(batch, seq, aw)
    return pl.pallas_call(
        functools.partial(_attn_a_kernel, seq=seq),
        grid=(batch, npair),
        in_specs=[nat, nat, nat, fold(4), fold(4), fold(4), fold(16), fold(16), fold(16),
                  pl.BlockSpec((3, 1, 2 * BLOCK, 2 * BLOCK), lambda b, p: (0, p, 0, 0))],
        out_specs=nat,
        out_shape=jax.ShapeDtypeStruct((batch, seq, aw), BF16),
        scratch_shapes=[pltpu.VMEM((3, seq, LANES), F32)] * 3,
        compiler_params=_cparams(2),
        name="attn_dilated",
    )(r3(q1), r3(k1), r3(v1), q4, k4, v4, q16, k16, v16, bias)


def _attn_c_kernel(sink_ref, q_ref, k_ref, v_ref, bias_ref, o_ref, *, seq, group):
    g = pl.program_id(1)
    npair = group // 2
    lo = lax.broadcasted_iota(I32, (BLOCK, LANES), 1) < HEAD_DIM

    def block(q0, k0, width, bias2):
        qn = q_ref[0, pl.ds(q0, BLOCK), :]
        q2 = jnp.concatenate([_stack_heads(qn[:, p * LANES:(p + 1) * LANES]) for p in range(npair)], axis=0)
        acc, m, den = _band_block(q2, k_ref[0, pl.ds(k0, width), :], v_ref[0, pl.ds(k0, width), :], bias2)
        for p in range(npair):
            outs = []
            for hh in range(2):
                h = 2 * p + hh
                r = slice(h * BLOCK, (h + 1) * BLOCK)
                outs.append(acc[r] / (den[r] + jnp.exp2(sink_ref[g * group + h] * LOG2E - m[r])))
            o_ref[0, pl.ds(q0, BLOCK), p * LANES:(p + 1) * LANES] = jnp.where(lo, outs[0], outs[1]).astype(BF16)

    block(0, 0, BLOCK, bias_ref[0, :, BLOCK:])

    def body(n, carry):
        block(pl.multiple_of(n * BLOCK, BLOCK), pl.multiple_of((n - 1) * BLOCK, BLOCK), 2 * BLOCK, bias_ref[0])
        return carry

    lax.fori_loop(1, seq // BLOCK, body, 0, unroll=C_UNROLL)


def _attn_c(q, k2, v2, bias, sinks, batch, seq, group):
    qw = q.shape[-1]
    n_kv = k2.shape[-1] // LANES
    gw = qw // n_kv
    return pl.pallas_call(
        functools.partial(_attn_c_kernel, seq=seq, group=group),
        grid=(batch, n_kv),
        in_specs=[pl.BlockSpec(memory_space=pltpu.SMEM),
                  pl.BlockSpec((1, seq, gw), lambda b, g: (b, 0, g)),
                  pl.BlockSpec((1, seq, LANES), lambda b, g: (b, 0, g)),
                  pl.BlockSpec((1, seq, LANES), lambda b, g: (b, 0, g)),
                  pl.BlockSpec((1, group * BLOCK, 2 * BLOCK), lambda b, g: (g, 0, 0))],
        out_specs=pl.BlockSpec((1, seq, gw), lambda b, g: (b, 0, g)),
        out_shape=jax.ShapeDtypeStruct((batch, seq, qw), BF16),
        compiler_params=_cparams(2),
        name="attn_swa",
    )(sinks, q, k2, v2, bias)


class _Job:
    def __init__(self, i, nchs, slot, qs_ref, wt_ref, q2_ref, sk_ref, mask_ref):
        self.i, self.nchs = i, nchs
        self.qs, self.wt, self.q2 = qs_ref.at[slot], wt_ref.at[slot], q2_ref.at[slot]
        self.sk, self.mask = sk_ref.at[slot], mask_ref.at[slot]
        key_pos = lax.broadcasted_iota(I32, (CW, BLOCK), 0) + (nchs - 1) * CW
        qry_pos = lax.broadcasted_iota(I32, (CW, BLOCK), 1) + i * BLOCK
        self.causal_last = key_pos <= qry_pos


def _dsa_scores(job, ki_ref):
    for c in range(job.nchs):
        st = _dot_nt(ki_ref[0, c * CW:(c + 1) * CW, :], job.qs[...])
        score = None
        for h in range(IDX_HEADS):
            term = jnp.maximum(st[:, h * BLOCK:(h + 1) * BLOCK], 0.0) * job.wt[h:h + 1, :]
            score = term if score is None else score + term
        if c == job.nchs - 1:
            score = jnp.where(job.causal_last, score, -jnp.inf)
        job.sk[c] = score


def _key_to_float(key):
    sk = key ^ INT_MIN
    return lax.bitcast_convert_type(sk ^ (lax.shift_right_arithmetic(sk, 31) & 0x7FFFFFFF), F32)


def _dsa_count(job, pred):
    grp = 8 * 8
    acc = jnp.zeros((grp, BLOCK), F32)
    for c in range(job.nchs):
        for g in range(CW // grp):
            acc = jnp.where(pred(job.sk[c, g * grp:(g + 1) * grp, :]), acc + 1.0, acc)
    return jnp.sum(acc, axis=0, keepdims=True)


def _dsa_select(job, thr, n_ge, topk):
    for c in range(job.nchs):
        sel = job.sk[c] >= thr
        if c == job.nchs - 1:
            sel = sel & job.causal_last
        job.mask[c] = jnp.where(sel, 0.0, NEG)
    return jnp.max(jnp.where((n_ge > float(topk)) & (thr > -jnp.inf), 1.0, 0.0))


def _dsa_break_ties(job, thr, topk):
    need = float(topk) - _dsa_count(job, lambda x: x > thr)
    r_i = lax.broadcasted_iota(I32, (BLOCK, BLOCK), 0)
    c_i = lax.broadcasted_iota(I32, (BLOCK, BLOCK), 1)
    lower = jnp.where(c_i <= r_i, 1.0, 0.0).astype(BF16)
    carry = jnp.zeros((1, BLOCK), F32)
    for c in range(job.nchs):
        for t in range(CW // BLOCK):
            rows = slice(t * BLOCK, (t + 1) * BLOCK)
            xs = job.sk[c, rows, :]
            eq = xs == thr
            eqf = jnp.where(eq, 1.0, 0.0)
            rank = _dot(lower, eqf.astype(BF16)) + carry
            keep = (xs > thr) | (eq & (rank <= need))
            causal = r_i + (c * CW + t * BLOCK) <= c_i + job.i * BLOCK
            job.mask[c, rows, :] = jnp.where(keep & causal, 0.0, NEG)
            carry = carry + jnp.sum(eqf, axis=0, keepdims=True)


def _dsa_attend(job, kb_ref, vt_ref, bias_ref, o_ref, npair):
    i, nchs, mask_ref, q2_ref = job.i, job.nchs, job.mask, job.q2
    tpc = CW // BLOCK
    out_rows = pl.ds(pl.multiple_of(i * BLOCK, BLOCK), BLOCK)
    first_half = lax.broadcasted_iota(I32, (LANES, BLOCK), 0) < HEAD_DIM
    for p in range(npair):
        lanes = slice(p * LANES, (p + 1) * LANES)
        logits = []
        for c in range(nchs):
            s = _dot_nt(kb_ref[0, c * CW:(c + 1) * CW, lanes], q2_ref[p])
            msk = mask_ref[c]
            per_head = []
            for hh in range(2):
                bias = jnp.concatenate(
                    [bias_ref[2 * p + hh, jnp.maximum(i - (c * tpc + t), 0)] for t in range(tpc)], axis=0)
                per_head.append(s[:, hh * BLOCK:(hh + 1) * BLOCK] + bias + msk)
            logits.append(per_head)
        pes = [[None, None] for _ in range(nchs)]
        for hh in range(2):
            m = None
            for c in range(nchs):
                mc = jnp.max(logits[c][hh], axis=0, keepdims=True)
                m = mc if m is None else jnp.maximum(m, mc)
            for c in range(nchs):
                pes[c][hh] = jnp.exp2(logits[c][hh] - m).astype(BF16)
        out_t = jnp.zeros((LANES + ONES_ROWS, 2 * BLOCK), F32)
        ones = jnp.ones((ONES_ROWS, CW), BF16)
        for c in range(nchs):
            vt1 = jnp.concatenate([vt_ref[0, lanes, c * CW:(c + 1) * CW], ones], axis=0)
            out_t = out_t + _dot(vt1, jnp.concatenate(pes[c], axis=1))
        o_t = out_t[:LANES] / out_t[LANES:LANES + 1]
        o_ref[0, out_rows, lanes] = jnp.where(first_half, o_t[:, :BLOCK], o_t[:, BLOCK:]).T.astype(BF16)


def _dsa_kernel(qi_ref, wi_ref, ki_ref, qb_ref, kb_ref, vt_ref, bias_ref, o_ref,
                qs_ref, wt_ref, q2_ref, sk_ref, mask_ref, *, topk, n_heads, n_blocks):
    j = pl.program_id(1)
    npair = n_heads // 2
    tpc = CW // BLOCK
    span = 2 * n_blocks // DSA_BLOCKS
    blocks, counts = [], []
    for base in range(0, n_blocks, span):
        blocks += [base + j, base + span - 1 - j]
        counts.append(lambda v, base=base: (base // tpc + v + 1, (base + span) // tpc - v))

    for slot, i in enumerate(blocks):
        rows = pl.ds(pl.multiple_of(i * BLOCK, BLOCK), BLOCK)
        qi = qi_ref[0, rows, :]
        lane_i = _lane_id_bf16(qi.shape)
        for h in range(IDX_HEADS):
            in_head = (lane_i >= h * IDX_DIM) & (lane_i < (h + 1) * IDX_DIM)
            qs_ref[slot, h * BLOCK:(h + 1) * BLOCK, :] = jnp.where(in_head, qi, jnp.zeros_like(qi))
        wt_ref[slot] = wi_ref[0, rows, :].T
        for p in range(npair):
            q2_ref[slot, p] = _stack_heads(qb_ref[0, rows, p * LANES:(p + 1) * LANES])

    for v in range(span // (2 * tpc)):
        @pl.when(j // tpc == v)
        def _(v=v):
            nchs_all = [n for count in counts for n in count(v)]
            jobs = [_Job(i, nchs, slot, qs_ref, wt_ref, q2_ref, sk_ref, mask_ref)
                    for slot, (i, nchs) in enumerate(zip(blocks, nchs_all))]
            for job in jobs:
                _dsa_scores(job, ki_ref)

            def search(it, state):
                bit = lax.shift_left(jnp.int32(1), 31 - it)
                out = []
                for job, (ans, n_ans) in zip(jobs, state):
                    n = _dsa_count(job, lambda x, c=_key_to_float(ans | bit): x >= c)
                    keep = n >= float(topk)
                    out.append((jnp.where(keep, ans | bit, ans), jnp.where(keep, n, n_ans)))
                return tuple(out)

            start = tuple((jnp.zeros((1, BLOCK), I32), jnp.zeros((1, BLOCK), F32)) for _ in jobs)
            found = lax.fori_loop(0, 32, search, start)
            thrs = [jnp.where(ans == 0, -jnp.inf, _key_to_float(ans)) for ans, _ in found]
            surplus = [_dsa_select(job, thr, n_ge, topk) for job, thr, (_, n_ge) in zip(jobs, thrs, found)]

            @pl.when(functools.reduce(jnp.maximum, surplus) > 0.0)
            def _():
                for job, thr in zip(jobs, thrs):
                    _dsa_break_ties(job, thr, topk)

            for job in jobs:
                _dsa_attend(job, kb_ref, vt_ref, bias_ref, o_ref, npair)


def _dsa(qi, wi, ki, qb, kb, vt, bias, batch, seq, topk):
    bw = qb.shape[-1]
    nidx = qi.shape[-1]
    nb = seq // BLOCK
    n_heads = bw // HEAD_DIM
    npair = n_heads // 2
    assert (2 * nb // DSA_BLOCKS) % (2 * (CW // BLOCK)) == 0
    full = lambda width: pl.BlockSpec((1, seq, width), lambda b, j: (b, 0, 0))
    r3 = lambda a: a.reshape(batch, seq, a.shape[-1])
    vm = lambda shape, dtype: pltpu.VMEM((DSA_BLOCKS,) + shape, dtype)
    out = pl.pallas_call(
        functools.partial(_dsa_kernel, topk=topk, n_heads=n_heads, n_blocks=nb),
        grid=(batch, nb // DSA_BLOCKS),
        in_specs=[full(nidx), full(LANES), full(nidx), full(bw), full(bw),
                  pl.BlockSpec((1, bw, seq), lambda b, j: (b, 0, 0)),
                  _const_spec((n_heads, nb, BLOCK, BLOCK))],
        out_specs=full(bw),
        out_shape=jax.ShapeDtypeStruct((batch, seq, bw), BF16),
        scratch_shapes=[vm((IDX_HEADS * BLOCK, nidx), BF16), vm((LANES, BLOCK), F32),
                        vm((npair, 2 * BLOCK, LANES), BF16),
                        vm((seq // CW, CW, BLOCK), F32), vm((seq // CW, CW, BLOCK), F32)],
        compiler_params=_cparams(2),
        name="dsa",
    )(r3(qi), r3(wi), r3(ki), r3(qb), r3(kb), vt, bias)
    return out.reshape(batch * seq, bw)


def _split_rows(w_out, parts):
    ws, r0 = [], 0
    for a in parts:
        ws.append(w_out[r0:r0 + a.shape[1]].astype(BF16))
        r0 += a.shape[1]
    return ws


def _odd_proj_weights(w_in, qw, kvw):
    d = w_in.shape[0]
    dup = lambda w: jnp.repeat(w.reshape(d, kvw // HEAD_DIM, 1, HEAD_DIM), 2, axis=2).reshape(d, 2 * kvw)
    return jnp.concatenate([w_in[:, :qw], dup(w_in[:, qw:qw + kvw]), dup(w_in[:, qw + kvw:])], axis=1).astype(BF16)


def kernel(x, norm_g, final_g, rel_bias_table, ffn_w1, ffn_w3, ffn_w2, hyb_w_in, hyb_cq_g, hyb_wq_b,
           hyb_wq_idx, hyb_w_out, swa_w_in, swa_sinks, swa_w_out):
    batch, seq, d = x.shape
    depth = norm_g.shape[0]
    n_heads = d // HEAD_DIM
    a_heads = n_heads // 2
    topk = min(TOPK_MAX, seq // 4)
    assert seq % (BLOCK * 16) == 0 and seq % CW == 0 and (batch * seq) % TM_FFN == 0 and seq % TM == 0

    bias_a = _build_bias(
        rel_bias_table,
        np.concatenate([_band_bucket_idx(dil, win // dil) for win, dil in A_BRANCHES], axis=0),
        0, a_heads).reshape(a_heads // 2, 2, 3, BLOCK, 2 * BLOCK).transpose(2, 0, 1, 3, 4)
    bias_a = bias_a.reshape(3, a_heads // 2, 2 * BLOCK, 2 * BLOCK)
    bias_b = _build_bias(rel_bias_table, _dense_bucket_idx(seq), a_heads, n_heads - a_heads)
    bias_b = bias_b.reshape(n_heads - a_heads, seq // BLOCK, BLOCK, BLOCK)
    bias_c = _build_bias(rel_bias_table, _band_bucket_idx(1, C_WINDOW - 1), 0, n_heads)

    ffn_w = (ffn_w1.astype(BF16), ffn_w3.astype(BF16), ffn_w2.astype(BF16))
    h = x.reshape(batch * seq, d)
    for layer in range(depth):
        g = norm_g[layer]
        i = layer // 2
        ffn_a = (g[0], ffn_w, (layer, 0))
        ffn_b = (g[2], ffn_w, (layer, 1))
        if layer % 2 == 0:
            h = _ffn(h, *ffn_a, final_g)
            outs = _proj_even(h, g[1], hyb_w_in[i], hyb_cq_g[i], hyb_wq_b[i], hyb_wq_idx[i], batch, seq)
            o_a = _attn_a(outs[:9], bias_a, batch, seq)
            qb, kb, vt, qi, ki, wi = outs[9:]
            o_b = _dsa(qi, wi, ki, qb, kb, vt, bias_b, batch, seq, topk)
            mix, w_out = [o_a.reshape(batch * seq, -1), o_b], hyb_w_out[i]
        else:
            kvw = (swa_w_in.shape[-1] - d) // 2
            n_kv = kvw // HEAD_DIM
            group = n_heads // n_kv
            h, q, k2, v2 = _ffn(h, *ffn_a, g[1], post_w=_odd_proj_weights(swa_w_in[i], d, kvw),
                                post_splits=((d, Q_SCALE), (2 * kvw, 1.0), (2 * kvw, 1.0)))
            r3 = lambda a: a.reshape(batch, seq, a.shape[-1])
            o_c = _attn_c(r3(q), r3(k2), r3(v2), bias_c.reshape(n_kv, group * BLOCK, 2 * BLOCK),
                          swa_sinks[i].astype(F32), batch, seq, group)
            mix, w_out = [o_c.reshape(batch * seq, -1)], swa_w_out[i]
        h = _ffn(h, *ffn_b, final_g, mix=mix, mix_w=_split_rows(w_out, mix), final=layer == depth - 1)
    return h.reshape(batch, seq, d)
```

```python
import functools
import math

import numpy as np
import jax
import jax.numpy as jnp
from jax import lax
from jax.experimental import pallas as pl
from jax.experimental.pallas import tpu as pltpu

F32 = jnp.float32
BF16 = jnp.bfloat16
I32 = jnp.int32

EPS = 1e-6
HEAD_DIM = 64
BLOCK = 128
LANES = 128
NUM_BUCKETS = 32
MAX_DISTANCE = 2048
A_BRANCHES = ((128, 1), (512, 4), (2048, 16))
C_WINDOW = 128
TOPK_MAX = 256
IDX_HEADS = 8
IDX_DIM = 32
NEG = -1e30
LOG2E = math.log2(math.e)
Q_SCALE = HEAD_DIM ** -0.5 * LOG2E
INT_MIN = -(2 ** 31)
VMEM_LIMIT = 56 * 1024 * 1024

TM = 1024
TM_FFN = 1024
CW = 2 * BLOCK
DSA_BLOCKS = 4
C_UNROLL = 15
ONES_ROWS = 16


def _cparams(n_axes):
    return pltpu.CompilerParams(
        dimension_semantics=("arbitrary",) * n_axes, vmem_limit_bytes=VMEM_LIMIT)


def _rms(x, g):
    return x * lax.rsqrt(jnp.mean(x * x, axis=-1, keepdims=True) + EPS) * g


def _dot(a, b):
    return jnp.dot(a, b, preferred_element_type=F32)


def _dot_nt(a, b):
    return lax.dot_general(a, b, (((1,), (1,)), ((), ())), preferred_element_type=F32)


def _lane_id_bf16(shape):
    return lax.broadcasted_iota(I32, shape, 1).astype(F32).astype(BF16)


def _rows(start, size):
    if isinstance(start, int):
        return pl.ds(start, size)
    return pl.ds(pl.multiple_of(start, size), size)


def _const_spec(shape):
    nd = len(shape)
    return pl.BlockSpec(shape, lambda *_: (0,) * nd, pipeline_mode=pl.Buffered(1))


def _t5_bucket_np(dist):
    dist = np.maximum(dist, 0)
    max_exact = NUM_BUCKETS // 2
    d_f = np.maximum(dist, 1).astype(np.float32)
    large = max_exact + (np.log(d_f / max_exact) / math.log(MAX_DISTANCE / max_exact)
                         * (NUM_BUCKETS - max_exact)).astype(np.int32)
    large = np.minimum(large, NUM_BUCKETS - 1)
    return np.where(dist < max_exact, dist, large).astype(np.int32)


def _band_bucket_idx(dil, max_dist):
    loc = np.arange(BLOCK)[:, None] + BLOCK - np.arange(2 * BLOCK)[None, :]
    valid = (loc >= 0) & (loc <= max_dist)
    return np.where(valid, _t5_bucket_np(dil * loc), -1).astype(np.int32)


def _dense_bucket_idx(seq):
    nb = seq // BLOCK
    d = np.arange(nb)[:, None, None] * BLOCK - np.arange(BLOCK)[None, :, None] + np.arange(BLOCK)[None, None, :]
    return np.where(d >= 0, _t5_bucket_np(d), -1).astype(np.int32).reshape(seq, BLOCK)


def _bias_kernel(tab_ref, idx_ref, o_ref, *, head0, buckets):
    h = pl.program_id(0) + head0
    for blk, present in enumerate(buckets):
        rows = slice(blk * BLOCK, (blk + 1) * BLOCK)
        idx = idx_ref[rows, :]
        out = jnp.full(idx.shape, NEG, F32)
        for b in present:
            out = jnp.where(idx == b, tab_ref[b, h] * LOG2E, out)
        o_ref[0, rows, :] = out


def _build_bias(table, idx_np, head0, n_heads):
    r, c = idx_np.shape
    buckets = tuple(tuple(int(b) for b in np.unique(idx_np[r0:r0 + BLOCK]) if b >= 0) for r0 in range(0, r, BLOCK))
    return pl.pallas_call(
        functools.partial(_bias_kernel, head0=head0, buckets=buckets),
        grid=(n_heads,),
        in_specs=[pl.BlockSpec(memory_space=pltpu.SMEM),
                  pl.BlockSpec((r, c), lambda h: (0, 0))],
        out_specs=pl.BlockSpec((1, r, c), lambda h: (h, 0, 0)),
        out_shape=jax.ShapeDtypeStruct((n_heads, r, c), F32),
        compiler_params=_cparams(1),
        name="bias_build",
    )(table.astype(F32), jnp.asarray(idx_np))


def _ffn_kernel(*refs, n_mix, final, post_splits):
    it = iter(refs)
    x_ref = next(it)
    mix = [next(it) for _ in range(n_mix)]
    mix_w = [next(it) for _ in range(n_mix)]
    g_ref, w1_ref, w3_ref, w2_ref, g2_ref = (next(it) for _ in range(5))
    wp_ref = next(it) if post_splits else None
    o_ref = next(it)
    post_refs = [next(it) for _ in post_splits]
    x = x_ref[...]
    for a_ref, w_ref in zip(mix, mix_w):
        x = x + _dot(a_ref[...], w_ref[...])
    xn = _rms(x, g_ref[...]).astype(BF16)
    h1 = _dot(xn, w1_ref[...])
    h3 = _dot(xn, w3_ref[...])
    act = (h1 * (1.0 / (1.0 + jnp.exp(-h1)))) * h3
    y = _dot(act.astype(BF16), w2_ref[...])
    out = x + 0.5 * y
    if final:
        out = _rms(out, g2_ref[...])
    o_ref[...] = out
    if post_splits:
        p = _dot(_rms(out, g2_ref[...]).astype(BF16), wp_ref[...])
        c0 = 0
        for p_ref, (width, scale) in zip(post_refs, post_splits):
            blk = p[:, c0:c0 + width]
            p_ref[...] = (blk if scale == 1.0 else blk * scale).astype(BF16)
            c0 += width


def _ffn(h, g, ffn_w, which, g2, *, mix=(), mix_w=(), final=False, post_w=None, post_splits=()):
    t, d = h.shape
    dff = ffn_w[0].shape[-1]
    tok = lambda width: pl.BlockSpec((TM_FFN, width), lambda i: (i, 0))
    pick = lambda rows, cols: pl.BlockSpec((None, None, rows, cols), lambda i: (*which, 0, 0),
                                           pipeline_mode=pl.Buffered(1))
    sh = jax.ShapeDtypeStruct
    in_specs = [tok(d)] + [tok(a.shape[1]) for a in mix] + [_const_spec(w.shape) for w in mix_w]
    in_specs += [_const_spec((1, d)), pick(d, dff), pick(d, dff), pick(dff, d), _const_spec((1, d))]
    args = [h, *mix, *mix_w, g.reshape(1, d), *ffn_w, g2.reshape(1, d)]
    out_specs, out_shape = [tok(d)], [sh((t, d), F32)]
    if post_splits:
        in_specs.append(_const_spec(post_w.shape))
        args.append(post_w)
        out_specs += [tok(width) for width, _ in post_splits]
        out_shape += [sh((t, width), BF16) for width, _ in post_splits]
    res = pl.pallas_call(
        functools.partial(_ffn_kernel, n_mix=len(mix), final=final, post_splits=tuple(post_splits)),
        grid=(t // TM_FFN,),
        in_specs=in_specs,
        out_specs=out_specs,
        out_shape=out_shape,
        compiler_params=_cparams(1),
        name="ffn",
    )(*args)
    return res if post_splits else res[0]


def _fold_store(scr, scr4, blk, outs, n_slab):
    o1, o4, o16 = outs
    o1[...] = blk.astype(BF16)
    for c in range(n_slab):
        scr[c] = blk[:, c * LANES:(c + 1) * LANES]
    rows4, rows16 = TM // 4, TM // 16
    for r4 in range(4):
        for c in range(n_slab):
            cols = slice(c * LANES, (c + 1) * LANES)
            f4 = scr[c, pl.ds(r4, rows4, stride=4), :]
            o4[0, r4, :, cols] = f4.astype(BF16)
            scr4[c] = f4
            for q in range(4):
                o16[0, 4 * q + r4, :, cols] = scr4[c, pl.ds(q, rows16, stride=4), :].astype(BF16)


def _proj_even_kernel(h_ref, g_ref, w_ref, cqg_ref, wqb_ref, wqi_ref,
                      q1, k1, v1, q4, k4, v4, q16, k16, v16, qb, kb, vb, qi, ki, wi, scr, scr4,
                      *, aw, bw, rank):
    u = _rms(h_ref[...], g_ref[...]).astype(BF16)
    p = _dot(u, w_ref[...])
    proj = lambda c0, width: p[:, c0:c0 + width]
    n_slab = aw // LANES
    scale = Q_SCALE
    _fold_store(scr, scr4, proj(0, aw) * scale, (q1, q4, q16), n_slab)
    _fold_store(scr, scr4, proj(aw, aw), (k1, k4, k16), n_slab)
    _fold_store(scr, scr4, proj(2 * aw, aw), (v1, v4, v16), n_slab)
    c0 = 3 * aw
    kb[...] = proj(c0, bw).astype(BF16)
    vb[0] = proj(c0 + bw, bw).T.astype(BF16)
    c1 = c0 + 2 * bw
    cq = _rms(proj(c1, rank), cqg_ref[...]).astype(BF16)
    qb[...] = (_dot(cq, wqb_ref[...]) * scale).astype(BF16)
    qi[...] = _dot(cq, wqi_ref[...]).astype(BF16)
    c2 = c1 + rank
    nidx = IDX_HEADS * IDX_DIM
    ki[...] = proj(c2, nidx).astype(BF16)
    idx_scale = (IDX_DIM ** -0.5) * (IDX_HEADS ** -0.5)
    wi[...] = proj(c2 + nidx, LANES) * idx_scale


def _proj_even(h, g, w_in, cq_g, wq_b, wq_idx, batch, seq):
    t, d = h.shape
    bw = wq_b.shape[1]
    rank = wq_b.shape[0]
    aw = (w_in.shape[1] - rank - 2 * bw - IDX_DIM - IDX_HEADS) // 3
    nidx = IDX_HEADS * IDX_DIM
    offs = np.cumsum([0, aw, aw, aw, rank, bw, bw, IDX_DIM, IDX_HEADS])
    col = lambda i: w_in[:, offs[i]:offs[i + 1]]
    w_kidx = jnp.tile(col(6), (1, IDX_HEADS))
    w_widx = jnp.pad(col(7), ((0, 0), (0, LANES - IDX_HEADS)))
    w_all = jnp.concatenate([col(0), col(1), col(2), col(4), col(5), col(3), w_kidx, w_widx], axis=1).astype(BF16)
    ncol = w_all.shape[1]
    spt = seq // TM
    tok = lambda width: pl.BlockSpec((TM, width), lambda i: (i, 0))
    fold = lambda dil: pl.BlockSpec((1, dil, TM // dil, aw), lambda i: (i // spt, 0, i % spt, 0))
    sh = jax.ShapeDtypeStruct
    nat = sh((t, aw), BF16)
    f4 = sh((batch, 4, seq // 4, aw), BF16)
    f16 = sh((batch, 16, seq // 16, aw), BF16)
    return pl.pallas_call(
        functools.partial(_proj_even_kernel, aw=aw, bw=bw, rank=rank),
        grid=(t // TM,),
        in_specs=[tok(d), _const_spec((1, d)), _const_spec((d, ncol)), _const_spec((1, rank)),
                  _const_spec((rank, bw)), _const_spec((rank, nidx))],
        out_specs=[tok(aw), tok(aw), tok(aw), fold(4), fold(4), fold(4), fold(16), fold(16), fold(16),
                   tok(bw), tok(bw), pl.BlockSpec((1, bw, TM), lambda i: (i // spt, 0, i % spt)),
                   tok(nidx), tok(nidx), tok(LANES)],
        out_shape=[nat, nat, nat, f4, f4, f4, f16, f16, f16,
                   sh((t, bw), BF16), sh((t, bw), BF16), sh((batch, bw, seq), BF16),
                   sh((t, nidx), BF16), sh((t, nidx), BF16), sh((t, LANES), F32)],
        scratch_shapes=[pltpu.VMEM((aw // LANES, TM, LANES), F32), pltpu.VMEM((aw // LANES, TM // 4, LANES), F32)],
        compiler_params=_cparams(1),
        name="proj_even",
    )(h, g.reshape(1, d), w_all, cq_g.reshape(1, rank), wq_b.astype(BF16), wq_idx.astype(BF16))


def _stack_heads(q):
    lane_b = _lane_id_bf16(q.shape)
    z = jnp.zeros_like(q)
    return jnp.concatenate([jnp.where(lane_b < HEAD_DIM, q, z), jnp.where(lane_b >= HEAD_DIM, q, z)], axis=0)


def _band_block(q2, kw, vw, bias2):
    s = _dot_nt(q2, kw) + bias2
    m = jnp.max(s, axis=-1, keepdims=True)
    p = jnp.exp2(s - m)
    den = jnp.sum(p, axis=-1, keepdims=True)
    return _dot(p.astype(BF16), vw), m, den


def _attn_a_kernel(q1, k1, v1, q4, k4, v4, q16, k16, v16, bias_ref, o_ref, acc_s, m_s, den_s, *, seq):
    lo = lax.broadcasted_iota(I32, (BLOCK, LANES), 1) < HEAD_DIM
    shape = (BLOCK, LANES)
    refs = ((q1, k1, v1), (q4, k4, v4), (q16, k16, v16))
    for br, (_, dil) in enumerate(A_BRANCHES):
        qr, kr, vr = refs[br]
        for r in range(dil):
            at = (lambda ref, s, n: ref[0, pl.ds(s, n), :]) if dil == 1 else \
                 (lambda ref, s, n, r=r: ref[0, r, pl.ds(s, n), :])
            for n in range(seq // dil // BLOCK):
                q2 = _stack_heads(at(qr, n * BLOCK, BLOCK))
                if n == 0:
                    acc, m, den = _band_block(q2, at(kr, 0, BLOCK), at(vr, 0, BLOCK), bias_ref[br, 0, :, BLOCK:])
                else:
                    k0 = (n - 1) * BLOCK
                    acc, m, den = _band_block(q2, at(kr, k0, 2 * BLOCK), at(vr, k0, 2 * BLOCK), bias_ref[br, 0])
                rows = pl.ds(n * BLOCK, BLOCK) if dil == 1 else pl.ds(n * BLOCK * dil + r, BLOCK, stride=dil)
                acc_s[br, rows, :] = jnp.where(lo, acc[:BLOCK], acc[BLOCK:])
                m_s[br, rows, :] = jnp.where(lo, jnp.broadcast_to(m[:BLOCK], shape), jnp.broadcast_to(m[BLOCK:], shape))
                den_s[br, rows, :] = jnp.where(lo, jnp.broadcast_to(den[:BLOCK], shape),
                                               jnp.broadcast_to(den[BLOCK:], shape))

    chunk = 2 * BLOCK

    def merge(c, carry):
        rows = pl.ds(pl.multiple_of(c * chunk, chunk), chunk)
        ms = [m_s[br, rows, :] for br in range(3)]
        mx = jnp.maximum(jnp.maximum(ms[0], ms[1]), ms[2])
        num = jnp.zeros((chunk, LANES), F32)
        den = jnp.zeros((chunk, LANES), F32)
        for br in range(3):
            e = jnp.exp2(ms[br] - mx)
            num = num + e * acc_s[br, rows, :]
            den = den + e * den_s[br, rows, :]
        o_ref[0, rows, :] = (num / den).astype(BF16)
        return carry

    lax.fori_loop(0, seq // chunk, merge, 0)


def _attn_a(qkv, bias, batch, seq):
    q1, k1, v1, q4, k4, v4, q16, k16, v16 = qkv
    aw = q1.shape[-1]
    npair = aw // LANES
    nat = pl.BlockSpec((1, seq, LANES), lambda b, p: (b, 0, p))
    fold = lambda dil: pl.BlockSpec((1, dil, seq // dil, LANES), lambda b, p: (b, 0, 0, p))
    r3 = lambda a: a.reshape(batch, seq, aw)
    return pl.pallas_call(
        functools.partial(_attn_a_kernel, seq=seq),
        grid=(batch, npair),
        in_specs=[nat, nat, nat, fold(4), fold(4), fold(4), fold(16), fold(16), fold(16),
                  pl.BlockSpec((3, 1, 2 * BLOCK, 2 * BLOCK), lambda b, p: (0, p, 0, 0))],
        out_specs=nat,
        out_shape=jax.ShapeDtypeStruct((batch, seq, aw), BF16),
        scratch_shapes=[pltpu.VMEM((3, seq, LANES), F32)] * 3,
        compiler_params=_cparams(2),
        name="attn_dilated",
    )(r3(q1), r3(k1), r3(v1), q4, k4, v4, q16, k16, v16, bias)


def _attn_c_kernel(sink_ref, q_ref, k_ref, v_ref, bias_ref, o_ref, *, seq, group):
    g = pl.program_id(1)
    npair = group // 2
    lo = lax.broadcasted_iota(I32, (BLOCK, LANES), 1) < HEAD_DIM

    def block(q0, k0, width, bias2):
        qn = q_ref[0, pl.ds(q0, BLOCK), :]
        q2 = jnp.concatenate([_stack_heads(qn[:, p * LANES:(p + 1) * LANES]) for p in range(npair)], axis=0)
        acc, m, den = _band_block(q2, k_ref[0, pl.ds(k0, width), :], v_ref[0, pl.ds(k0, width), :], bias2)
        for p in range(npair):
            outs = []
            for hh in range(2):
                h = 2 * p + hh
                r = slice(h * BLOCK, (h + 1) * BLOCK)
                outs.append(acc[r] / (den[r] + jnp.exp2(sink_ref[g * group + h] * LOG2E - m[r])))
            o_ref[0, pl.ds(q0, BLOCK), p * LANES:(p + 1) * LANES] = jnp.where(lo, outs[0], outs[1]).astype(BF16)

    block(0, 0, BLOCK, bias_ref[0, :, BLOCK:])

    def body(n, carry):
        block(pl.multiple_of(n * BLOCK, BLOCK), pl.multiple_of((n - 1) * BLOCK, BLOCK), 2 * BLOCK, bias_ref[0])
        return carry

    lax.fori_loop(1, seq // BLOCK, body, 0, unroll=C_UNROLL)


def _attn_c(q, k2, v2, bias, sinks, batch, seq, group):
    qw = q.shape[-1]
    n_kv = k2.shape[-1] // LANES
    gw = qw // n_kv
    return pl.pallas_call(
        functools.partial(_attn_c_kernel, seq=seq, group=group),
        grid=(batch, n_kv),
        in_specs=[pl.BlockSpec(memory_space=pltpu.SMEM),
                  pl.BlockSpec((1, seq, gw), lambda b, g: (b, 0, g)),
                  pl.BlockSpec((1, seq, LANES), lambda b, g: (b, 0, g)),
                  pl.BlockSpec((1, seq, LANES), lambda b, g: (b, 0, g)),
                  pl.BlockSpec((1, group * BLOCK, 2 * BLOCK), lambda b, g: (g, 0, 0))],
        out_specs=pl.BlockSpec((1, seq, gw), lambda b, g: (b, 0, g)),
        out_shape=jax.ShapeDtypeStruct((batch, seq, qw), BF16),
        compiler_params=_cparams(2),
        name="attn_swa",
    )(sinks, q, k2, v2, bias)


class _Job:
    def __init__(self, i, nchs, slot, qs_ref, wt_ref, q2_ref, sk_ref, mask_ref):
        self.i, self.nchs = i, nchs
        self.qs, self.wt, self.q2 = qs_ref.at[slot], wt_ref.at[slot], q2_ref.at[slot]
        self.sk, self.mask = sk_ref.at[slot], mask_ref.at[slot]
        key_pos = lax.broadcasted_iota(I32, (CW, BLOCK), 0) + (nchs - 1) * CW
        qry_pos = lax.broadcasted_iota(I32, (CW, BLOCK), 1) + i * BLOCK
        self.causal_last = key_pos <= qry_pos


def _dsa_scores(job, ki_ref):
    for c in range(job.nchs):
        st = _dot_nt(ki_ref[0, c * CW:(c + 1) * CW, :], job.qs[...])
        score = None
        for h in range(IDX_HEADS):
            term = jnp.maximum(st[:, h * BLOCK:(h + 1) * BLOCK], 0.0) * job.wt[h:h + 1, :]
            score = term if score is None else score + term
        if c == job.nchs - 1:
            score = jnp.where(job.causal_last, score, -jnp.inf)
        job.sk[c] = score


def _key_to_float(key):
    sk = key ^ INT_MIN
    return lax.bitcast_convert_type(sk ^ (lax.shift_right_arithmetic(sk, 31) & 0x7FFFFFFF), F32)


def _dsa_count(job, pred):
    grp = 8 * 8
    acc = jnp.zeros((grp, BLOCK), F32)
    for c in range(job.nchs):
        for g in range(CW // grp):
            acc = jnp.where(pred(job.sk[c, g * grp:(g + 1) * grp, :]), acc + 1.0, acc)
    return jnp.sum(acc, axis=0, keepdims=True)


def _dsa_select(job, thr, n_ge, topk):
    for c in range(job.nchs):
        sel = job.sk[c] >= thr
        if c == job.nchs - 1:
            sel = sel & job.causal_last
        job.mask[c] = jnp.where(sel, 0.0, NEG)
    return jnp.max(jnp.where((n_ge > float(topk)) & (thr > -jnp.inf), 1.0, 0.0))


def _dsa_break_ties(job, thr, topk):
    need = float(topk) - _dsa_count(job, lambda x: x > thr)
    r_i = lax.broadcasted_iota(I32, (BLOCK, BLOCK), 0)
    c_i = lax.broadcasted_iota(I32, (BLOCK, BLOCK), 1)
    lower = jnp.where(c_i <= r_i, 1.0, 0.0).astype(BF16)
    carry = jnp.zeros((1, BLOCK), F32)
    for c in range(job.nchs):
        for t in range(CW // BLOCK):
            rows = slice(t * BLOCK, (t + 1) * BLOCK)
            xs = job.sk[c, rows, :]
            eq = xs == thr
            eqf = jnp.where(eq, 1.0, 0.0)
            rank = _dot(lower, eqf.astype(BF16)) + carry
            keep = (xs > thr) | (eq & (rank <= need))
            causal = r_i + (c * CW + t * BLOCK) <= c_i + job.i * BLOCK
            job.mask[c, rows, :] = jnp.where(keep & causal, 0.0, NEG)
            carry = carry + jnp.sum(eqf, axis=0, keepdims=True)


def _dsa_attend(job, kb_ref, vt_ref, bias_ref, o_ref, pairs):
    i, nchs, mask_ref, q2_ref = job.i, job.nchs, job.mask, job.q2
    tpc = CW // BLOCK
    out_rows = pl.ds(pl.multiple_of(i * BLOCK, BLOCK), BLOCK)
    first_half = lax.broadcasted_iota(I32, (LANES, BLOCK), 0) < HEAD_DIM
    for p in pairs:
        lanes = slice(p * LANES, (p + 1) * LANES)
        logits = []
        for c in range(nchs):
            s = _dot_nt(kb_ref[0, c * CW:(c + 1) * CW, lanes], q2_ref[p])
            msk = mask_ref[c]
            per_head = []
            for hh in range(2):
                bias = jnp.concatenate(
                    [bias_ref[2 * p + hh, jnp.maximum(i - (c * tpc + t), 0)] for t in range(tpc)], axis=0)
                per_head.append(s[:, hh * BLOCK:(hh + 1) * BLOCK] + bias + msk)
            logits.append(per_head)
        pes = [[None, None] for _ in range(nchs)]
        for hh in range(2):
            m = None
            for c in range(nchs):
                mc = jnp.max(logits[c][hh], axis=0, keepdims=True)
                m = mc if m is None else jnp.maximum(m, mc)
            for c in range(nchs):
                pes[c][hh] = jnp.exp2(logits[c][hh] - m).astype(BF16)
        out_t = jnp.zeros((LANES + ONES_ROWS, 2 * BLOCK), F32)
        ones = jnp.ones((ONES_ROWS, CW), BF16)
        for c in range(nchs):
            vt1 = jnp.concatenate([vt_ref[0, lanes, c * CW:(c + 1) * CW], ones], axis=0)
            out_t = out_t + _dot(vt1, jnp.concatenate(pes[c], axis=1))
        o_t = out_t[:LANES] / out_t[LANES:LANES + 1]
        o_ref[0, out_rows, lanes] = jnp.where(first_half, o_t[:, :BLOCK], o_t[:, BLOCK:]).T.astype(BF16)


def _dsa_kernel(qi_ref, wi_ref, ki_ref, qb_ref, kb_ref, vt_ref, bias_ref, o_ref,
                qs_ref, wt_ref, q2_ref, sk_ref, mask_ref, *, topk, n_heads, n_blocks):
    j = pl.program_id(1)
    npair = n_heads // 2
    tpc = CW // BLOCK
    span = 2 * n_blocks // DSA_BLOCKS
    blocks, counts = [], []
    for base in range(0, n_blocks, span):
        blocks += [base + j, base + span - 1 - j]
        counts.append(lambda v, base=base: (base // tpc + v + 1, (base + span) // tpc - v))

    for slot, i in enumerate(blocks):
        rows = pl.ds(pl.multiple_of(i * BLOCK, BLOCK), BLOCK)
        qi = qi_ref[0, rows, :]
        lane_i = _lane_id_bf16(qi.shape)
        for h in range(IDX_HEADS):
            in_head = (lane_i >= h * IDX_DIM) & (lane_i < (h + 1) * IDX_DIM)
            qs_ref[slot, h * BLOCK:(h + 1) * BLOCK, :] = jnp.where(in_head, qi, jnp.zeros_like(qi))
        wt_ref[slot] = wi_ref[0, rows, :].T
        for p in range(npair):
            q2_ref[slot, p] = _stack_heads(qb_ref[0, rows, p * LANES:(p + 1) * LANES])

    for v in range(span // (2 * tpc)):
        @pl.when(j // tpc == v)
        def _(v=v):
            nchs_all = [n for count in counts for n in count(v)]
            jobs = [_Job(i, nchs, slot, qs_ref, wt_ref, q2_ref, sk_ref, mask_ref)
                    for slot, (i, nchs) in enumerate(zip(blocks, nchs_all))]
            for job in jobs:
                _dsa_scores(job, ki_ref)

            def search(it, state):
                bit = lax.shift_left(jnp.int32(1), 31 - it)
                out = []
                for job, (ans, n_ans) in zip(jobs, state):
                    n = _dsa_count(job, lambda x, c=_key_to_float(ans | bit): x >= c)
                    keep = n >= float(topk)
                    out.append((jnp.where(keep, ans | bit, ans), jnp.where(keep, n, n_ans)))
                return tuple(out)

            start = tuple((jnp.zeros((1, BLOCK), I32), jnp.zeros((1, BLOCK), F32)) for _ in jobs)
            found = lax.fori_loop(0, 32, search, start)
            thrs = [jnp.where(ans == 0, -jnp.inf, _key_to_float(ans)) for ans, _ in found]
            surplus = [_dsa_select(job, thr, n_ge, topk) for job, thr, (_, n_ge) in zip(jobs, thrs, found)]

            @pl.when(functools.reduce(jnp.maximum, surplus) > 0.0)
            def _():
                for job, thr in zip(jobs, thrs):
                    _dsa_break_ties(job, thr, topk)

            for p in range(npair):
                for job in sorted(jobs, key=lambda jb: -jb.nchs):
                    _dsa_attend(job, kb_ref, vt_ref, bias_ref, o_ref, (p,))


def _dsa(qi, wi, ki, qb, kb, vt, bias, batch, seq, topk):
    bw = qb.shape[-1]
    nidx = qi.shape[-1]
    nb = seq // BLOCK
    n_heads = bw // HEAD_DIM
    npair = n_heads // 2
    assert (2 * nb // DSA_BLOCKS) % (2 * (CW // BLOCK)) == 0
    full = lambda width: pl.BlockSpec((1, seq, width), lambda b, j: (b, 0, 0))
    r3 = lambda a: a.reshape(batch, seq, a.shape[-1])
    vm = lambda shape, dtype: pltpu.VMEM((DSA_BLOCKS,) + shape, dtype)
    out = pl.pallas_call(
        functools.partial(_dsa_kernel, topk=topk, n_heads=n_heads, n_blocks=nb),
        grid=(batch, nb // DSA_BLOCKS),
        in_specs=[full(nidx), full(LANES), full(nidx), full(bw), full(bw),
                  pl.BlockSpec((1, bw, seq), lambda b, j: (b, 0, 0)),
                  _const_spec((n_heads, nb, BLOCK, BLOCK))],
        out_specs=full(bw),
        out_shape=jax.ShapeDtypeStruct((batch, seq, bw), BF16),
        scratch_shapes=[vm((IDX_HEADS * BLOCK, nidx), BF16), vm((LANES, BLOCK), F32),
                        vm((npair, 2 * BLOCK, LANES), BF16),
                        vm((seq // CW, CW, BLOCK), F32), vm((seq // CW, CW, BLOCK), F32)],
        compiler_params=_cparams(2),
        name="dsa",
    )(r3(qi), r3(wi), r3(ki), r3(qb), r3(kb), vt, bias)
    return out.reshape(batch * seq, bw)


def _split_rows(w_out, parts):
    ws, r0 = [], 0
    for a in parts:
        ws.append(w_out[r0:r0 + a.shape[1]].astype(BF16))
        r0 += a.shape[1]
    return ws


def _odd_proj_weights(w_in, qw, kvw):
    d = w_in.shape[0]
    dup = lambda w: jnp.repeat(w.reshape(d, kvw // HEAD_DIM, 1, HEAD_DIM), 2, axis=2).reshape(d, 2 * kvw)
    return jnp.concatenate([w_in[:, :qw], dup(w_in[:, qw:qw + kvw]), dup(w_in[:, qw + kvw:])], axis=1).astype(BF16)


def kernel(x, norm_g, final_g, rel_bias_table, ffn_w1, ffn_w3, ffn_w2, hyb_w_in, hyb_cq_g, hyb_wq_b,
           hyb_wq_idx, hyb_w_out, swa_w_in, swa_sinks, swa_w_out):
    batch, seq, d = x.shape
    depth = norm_g.shape[0]
    n_heads = d // HEAD_DIM
    a_heads = n_heads // 2
    topk = min(TOPK_MAX, seq // 4)
    assert seq % (BLOCK * 16) == 0 and seq % CW == 0 and (batch * seq) % TM_FFN == 0 and seq % TM == 0

    bias_a = _build_bias(
        rel_bias_table,
        np.concatenate([_band_bucket_idx(dil, win // dil) for win, dil in A_BRANCHES], axis=0),
        0, a_heads).reshape(a_heads // 2, 2, 3, BLOCK, 2 * BLOCK).transpose(2, 0, 1, 3, 4)
    bias_a = bias_a.reshape(3, a_heads // 2, 2 * BLOCK, 2 * BLOCK)
    bias_b = _build_bias(rel_bias_table, _dense_bucket_idx(seq), a_heads, n_heads - a_heads)
    bias_b = bias_b.reshape(n_heads - a_heads, seq // BLOCK, BLOCK, BLOCK)
    bias_c = _build_bias(rel_bias_table, _band_bucket_idx(1, C_WINDOW - 1), 0, n_heads)

    ffn_w = (ffn_w1.astype(BF16), ffn_w3.astype(BF16), ffn_w2.astype(BF16))
    h = x.reshape(batch * seq, d)
    for layer in range(depth):
        g = norm_g[layer]
        i = layer // 2
        ffn_a = (g[0], ffn_w, (layer, 0))
        ffn_b = (g[2], ffn_w, (layer, 1))
        if layer % 2 == 0:
            h = _ffn(h, *ffn_a, final_g)
            outs = _proj_even(h, g[1], hyb_w_in[i], hyb_cq_g[i], hyb_wq_b[i], hyb_wq_idx[i], batch, seq)
            o_a = _attn_a(outs[:9], bias_a, batch, seq)
            qb, kb, vt, qi, ki, wi = outs[9:]
            o_b = _dsa(qi, wi, ki, qb, kb, vt, bias_b, batch, seq, topk)
            mix, w_out = [o_a.reshape(batch * seq, -1), o_b], hyb_w_out[i]
        else:
            kvw = (swa_w_in.shape[-1] - d) // 2
            n_kv = kvw // HEAD_DIM
            group = n_heads // n_kv
            h, q, k2, v2 = _ffn(h, *ffn_a, g[1], post_w=_odd_proj_weights(swa_w_in[i], d, kvw),
                                post_splits=((d, Q_SCALE), (2 * kvw, 1.0), (2 * kvw, 1.0)))
            r3 = lambda a: a.reshape(batch, seq, a.shape[-1])
            o_c = _attn_c(r3(q), r3(k2), r3(v2), bias_c.reshape(n_kv, group * BLOCK, 2 * BLOCK),
                          swa_sinks[i].astype(F32), batch, seq, group)
            mix, w_out = [o_c.reshape(batch * seq, -1)], swa_w_out[i]
        h = _ffn(h, *ffn_b, final_g, mix=mix, mix_w=_split_rows(w_out, mix), final=layer == depth - 1)
    return h.reshape(batch, seq, d)
```

```python
import functools
import math

import numpy as np
import jax
import jax.numpy as jnp
from jax import lax
from jax.experimental import pallas as pl
from jax.experimental.pallas import tpu as pltpu

F32 = jnp.float32
BF16 = jnp.bfloat16
I32 = jnp.int32

EPS = 1e-6
HEAD_DIM = 64
BLOCK = 128
LANES = 128
NUM_BUCKETS = 32
MAX_DISTANCE = 2048
A_BRANCHES = ((128, 1), (512, 4), (2048, 16))
C_WINDOW = 128
TOPK_MAX = 256
IDX_HEADS = 8
IDX_DIM = 32
NEG = -1e30
LOG2E = math.log2(math.e)
Q_SCALE = HEAD_DIM ** -0.5 * LOG2E
INT_MIN = -(2 ** 31)
VMEM_LIMIT = 56 * 1024 * 1024

TM = 1024
TM_FFN = 1024
CW = 2 * BLOCK
DSA_BLOCKS = 4
C_UNROLL = 15
ONES_ROWS = 16


def _cparams(n_axes):
    return pltpu.CompilerParams(
        dimension_semantics=("arbitrary",) * n_axes, vmem_limit_bytes=VMEM_LIMIT)


def _rms(x, g):
    return x * lax.rsqrt(jnp.mean(x * x, axis=-1, keepdims=True) + EPS) * g


def _dot(a, b):
    return jnp.dot(a, b, preferred_element_type=F32)


def _dot_nt(a, b):
    return lax.dot_general(a, b, (((1,), (1,)), ((), ())), preferred_element_type=F32)


def _lane_id_bf16(shape):
    return lax.broadcasted_iota(I32, shape, 1).astype(F32).astype(BF16)


def _const_spec(shape):
    nd = len(shape)
    return pl.BlockSpec(shape, lambda *_: (0,) * nd, pipeline_mode=pl.Buffered(1))


def _t5_bucket_np(dist):
    dist = np.maximum(dist, 0)
    max_exact = NUM_BUCKETS // 2
    d_f = np.maximum(dist, 1).astype(np.float32)
    large = max_exact + (np.log(d_f / max_exact) / math.log(MAX_DISTANCE / max_exact)
                         * (NUM_BUCKETS - max_exact)).astype(np.int32)
    large = np.minimum(large, NUM_BUCKETS - 1)
    return np.where(dist < max_exact, dist, large).astype(np.int32)


def _band_bucket_idx(dil, max_dist):
    loc = np.arange(BLOCK)[:, None] + BLOCK - np.arange(2 * BLOCK)[None, :]
    valid = (loc >= 0) & (loc <= max_dist)
    return np.where(valid, _t5_bucket_np(dil * loc), -1).astype(np.int32)


def _dense_bucket_idx(seq):
    nb = seq // BLOCK
    d = np.arange(nb)[:, None, None] * BLOCK - np.arange(BLOCK)[None, :, None] + np.arange(BLOCK)[None, None, :]
    return np.where(d >= 0, _t5_bucket_np(d), -1).astype(np.int32).reshape(seq, BLOCK)


def _bias_kernel(tab_ref, idx_ref, o_ref, *, head0, buckets):
    h = pl.program_id(0) + head0
    for blk, present in enumerate(buckets):
        rows = slice(blk * BLOCK, (blk + 1) * BLOCK)
        idx = idx_ref[rows, :]
        out = jnp.full(idx.shape, NEG, F32)
        for b in present:
            out = jnp.where(idx == b, tab_ref[b, h] * LOG2E, out)
        o_ref[0, rows, :] = out


def _build_bias(table, idx_np, head0, n_heads):
    r, c = idx_np.shape
    buckets = tuple(tuple(int(b) for b in np.unique(idx_np[r0:r0 + BLOCK]) if b >= 0) for r0 in range(0, r, BLOCK))
    return pl.pallas_call(
        functools.partial(_bias_kernel, head0=head0, buckets=buckets),
        grid=(n_heads,),
        in_specs=[pl.BlockSpec(memory_space=pltpu.SMEM),
                  pl.BlockSpec((r, c), lambda h: (0, 0))],
        out_specs=pl.BlockSpec((1, r, c), lambda h: (h, 0, 0)),
        out_shape=jax.ShapeDtypeStruct((n_heads, r, c), F32),
        compiler_params=_cparams(1),
        name="bias_build",
    )(table.astype(F32), jnp.asarray(idx_np))


def _ffn_kernel(*refs, n_mix, final, post_splits):
    it = iter(refs)
    x_ref = next(it)
    mix = [next(it) for _ in range(n_mix)]
    mix_w = [next(it) for _ in range(n_mix)]
    g_ref, w1_ref, w3_ref, w2_ref, g2_ref = (next(it) for _ in range(5))
    wp_ref = next(it) if post_splits else None
    o_ref = next(it)
    post_refs = [next(it) for _ in post_splits]
    x = x_ref[...]
    for a_ref, w_ref in zip(mix, mix_w):
        x = x + _dot(a_ref[...], w_ref[...])
    xn = _rms(x, g_ref[...]).astype(BF16)
    h1 = _dot(xn, w1_ref[...])
    h3 = _dot(xn, w3_ref[...])
    act = (h1 * (1.0 / (1.0 + jnp.exp(-h1)))) * h3
    y = _dot(act.astype(BF16), w2_ref[...])
    out = x + 0.5 * y
    if final:
        out = _rms(out, g2_ref[...])
    o_ref[...] = out
    if post_splits:
        p = _dot(_rms(out, g2_ref[...]).astype(BF16), wp_ref[...])
        c0 = 0
        for p_ref, (width, scale) in zip(post_refs, post_splits):
            blk = p[:, c0:c0 + width]
            p_ref[...] = (blk if scale == 1.0 else blk * scale).astype(BF16)
            c0 += width


def _ffn(h, g, ffn_w, which, g2, *, mix=(), mix_w=(), final=False, post_w=None, post_splits=()):
    t, d = h.shape
    dff = ffn_w[0].shape[-1]
    tok = lambda width: pl.BlockSpec((TM_FFN, width), lambda i: (i, 0))
    pick = lambda rows, cols: pl.BlockSpec((None, None, rows, cols), lambda i: (*which, 0, 0),
                                           pipeline_mode=pl.Buffered(1))
    sh = jax.ShapeDtypeStruct
    in_specs = [tok(d)] + [tok(a.shape[1]) for a in mix] + [_const_spec(w.shape) for w in mix_w]
    in_specs += [_const_spec((1, d)), pick(d, dff), pick(d, dff), pick(dff, d), _const_spec((1, d))]
    args = [h, *mix, *mix_w, g.reshape(1, d), *ffn_w, g2.reshape(1, d)]
    out_specs, out_shape = [tok(d)], [sh((t, d), F32)]
    if post_splits:
        in_specs.append(_const_spec(post_w.shape))
        args.append(post_w)
        out_specs += [tok(width) for width, _ in post_splits]
        out_shape += [sh((t, width), BF16) for width, _ in post_splits]
    res = pl.pallas_call(
        functools.partial(_ffn_kernel, n_mix=len(mix), final=final, post_splits=tuple(post_splits)),
        grid=(t // TM_FFN,),
        in_specs=in_specs,
        out_specs=out_specs,
        out_shape=out_shape,
        compiler_params=_cparams(1),
        name="ffn",
    )(*args)
    return res if post_splits else res[0]


def _fold_store(scr, scr4, blk, outs, n_slab):
    o1, o4, o16 = outs
    o1[...] = blk.astype(BF16)
    for c in range(n_slab):
        scr[c] = blk[:, c * LANES:(c + 1) * LANES]
    rows4, rows16 = TM // 4, TM // 16
    for r4 in range(4):
        for c in range(n_slab):
            cols = slice(c * LANES, (c + 1) * LANES)
            f4 = scr[c, pl.ds(r4, rows4, stride=4), :]
            o4[0, r4, :, cols] = f4.astype(BF16)
            scr4[c] = f4
            for q in range(4):
                o16[0, 4 * q + r4, :, cols] = scr4[c, pl.ds(q, rows16, stride=4), :].astype(BF16)


def _proj_even_kernel(h_ref, g_ref, w_ref, cqg_ref, wqb_ref, wqi_ref,
                      q1, k1, v1, q4, k4, v4, q16, k16, v16, qb, kb, vb, qi, ki, wi, scr, scr4,
                      *, aw, bw, rank):
    u = _rms(h_ref[...], g_ref[...]).astype(BF16)
    p = _dot(u, w_ref[...])
    proj = lambda c0, width: p[:, c0:c0 + width]
    n_slab = aw // LANES
    scale = Q_SCALE
    _fold_store(scr, scr4, proj(0, aw) * scale, (q1, q4, q16), n_slab)
    _fold_store(scr, scr4, proj(aw, aw), (k1, k4, k16), n_slab)
    _fold_store(scr, scr4, proj(2 * aw, aw), (v1, v4, v16), n_slab)
    c0 = 3 * aw
    kb[...] = proj(c0, bw).astype(BF16)
    vb[0] = proj(c0 + bw, bw).T.astype(BF16)
    c1 = c0 + 2 * bw
    cq = _rms(proj(c1, rank), cqg_ref[...]).astype(BF16)
    qb[...] = (_dot(cq, wqb_ref[...]) * scale).astype(BF16)
    qi[...] = _dot(cq, wqi_ref[...]).astype(BF16)
    c2 = c1 + rank
    nidx = IDX_HEADS * IDX_DIM
    ki[...] = proj(c2, nidx).astype(BF16)
    idx_scale = (IDX_DIM ** -0.5) * (IDX_HEADS ** -0.5)
    wi[...] = proj(c2 + nidx, LANES) * idx_scale


def _proj_even(h, g, w_in, cq_g, wq_b, wq_idx, batch, seq):
    t, d = h.shape
    bw = wq_b.shape[1]
    rank = wq_b.shape[0]
    aw = (w_in.shape[1] - rank - 2 * bw - IDX_DIM - IDX_HEADS) // 3
    nidx = IDX_HEADS * IDX_DIM
    offs = np.cumsum([0, aw, aw, aw, rank, bw, bw, IDX_DIM, IDX_HEADS])
    col = lambda i: w_in[:, offs[i]:offs[i + 1]]
    w_kidx = jnp.tile(col(6), (1, IDX_HEADS))
    w_widx = jnp.pad(col(7), ((0, 0), (0, LANES - IDX_HEADS)))
    w_all = jnp.concatenate([col(0), col(1), col(2), col(4), col(5), col(3), w_kidx, w_widx], axis=1).astype(BF16)
    ncol = w_all.shape[1]
    spt = seq // TM
    tok = lambda width: pl.BlockSpec((TM, width), lambda i: (i, 0))
    fold = lambda dil: pl.BlockSpec((1, dil, TM // dil, aw), lambda i: (i // spt, 0, i % spt, 0))
    sh = jax.ShapeDtypeStruct
    nat = sh((t, aw), BF16)
    f4 = sh((batch, 4, seq // 4, aw), BF16)
    f16 = sh((batch, 16, seq // 16, aw), BF16)
    return pl.pallas_call(
        functools.partial(_proj_even_kernel, aw=aw, bw=bw, rank=rank),
        grid=(t // TM,),
        in_specs=[tok(d), _const_spec((1, d)), _const_spec((d, ncol)), _const_spec((1, rank)),
                  _const_spec((rank, bw)), _const_spec((rank, nidx))],
        out_specs=[tok(aw), tok(aw), tok(aw), fold(4), fold(4), fold(4), fold(16), fold(16), fold(16),
                   tok(bw), tok(bw), pl.BlockSpec((1, bw, TM), lambda i: (i // spt, 0, i % spt)),
                   tok(nidx), tok(nidx), tok(LANES)],
        out_shape=[nat, nat, nat, f4, f4, f4, f16, f16, f16,
                   sh((t, bw), BF16), sh((t, bw), BF16), sh((batch, bw, seq), BF16),
                   sh((t, nidx), BF16), sh((t, nidx), BF16), sh((t, LANES), F32)],
        scratch_shapes=[pltpu.VMEM((aw // LANES, TM, LANES), F32), pltpu.VMEM((aw // LANES, TM // 4, LANES), F32)],
        compiler_params=_cparams(1),
        name="proj_even",
    )(h, g.reshape(1, d), w_all, cq_g.reshape(1, rank), wq_b.astype(BF16), wq_idx.astype(BF16))


def _stack_heads(q):
    lane_b = _lane_id_bf16(q.shape)
    z = jnp.zeros_like(q)
    return jnp.concatenate([jnp.where(lane_b < HEAD_DIM, q, z), jnp.where(lane_b >= HEAD_DIM, q, z)], axis=0)


def _band_block(q2, kw, vw, bias2):
    s = _dot_nt(q2, kw) + bias2
    m = jnp.max(s, axis=-1, keepdims=True)
    p = jnp.exp2(s - m)
    den = jnp.sum(p, axis=-1, keepdims=True)
    return _dot(p.astype(BF16), vw), m, den


def _attn_a_kernel(q1, k1, v1, q4, k4, v4, q16, k16, v16, bias_ref, o_ref, acc_s, m_s, den_s, *, seq):
    lo = lax.broadcasted_iota(I32, (BLOCK, LANES), 1) < HEAD_DIM
    shape = (BLOCK, LANES)
    refs = ((q1, k1, v1), (q4, k4, v4), (q16, k16, v16))
    for br, (_, dil) in enumerate(A_BRANCHES):
        qr, kr, vr = refs[br]
        for r in range(dil):
            at = (lambda ref, s, n: ref[0, pl.ds(s, n), :]) if dil == 1 else \
                 (lambda ref, s, n, r=r: ref[0, r, pl.ds(s, n), :])
            for n in range(seq // dil // BLOCK):
                q2 = _stack_heads(at(qr, n * BLOCK, BLOCK))
                if n == 0:
                    acc, m, den = _band_block(q2, at(kr, 0, BLOCK), at(vr, 0, BLOCK), bias_ref[br, 0, :, BLOCK:])
                else:
                    k0 = (n - 1) * BLOCK
                    acc, m, den = _band_block(q2, at(kr, k0, 2 * BLOCK), at(vr, k0, 2 * BLOCK), bias_ref[br, 0])
                rows = pl.ds(n * BLOCK, BLOCK) if dil == 1 else pl.ds(n * BLOCK * dil + r, BLOCK, stride=dil)
                acc_s[br, rows, :] = jnp.where(lo, acc[:BLOCK], acc[BLOCK:])
                m_s[br, rows, :] = jnp.where(lo, jnp.broadcast_to(m[:BLOCK], shape), jnp.broadcast_to(m[BLOCK:], shape))
                den_s[br, rows, :] = jnp.where(lo, jnp.broadcast_to(den[:BLOCK], shape),
                                               jnp.broadcast_to(den[BLOCK:], shape))

    chunk = 2 * BLOCK

    def merge(c, carry):
        rows = pl.ds(pl.multiple_of(c * chunk, chunk), chunk)
        ms = [m_s[br, rows, :] for br in range(3)]
        mx = jnp.maximum(jnp.maximum(ms[0], ms[1]), ms[2])
        num = jnp.zeros((chunk, LANES), F32)
        den = jnp.zeros((chunk, LANES), F32)
        for br in range(3):
            e = jnp.exp2(ms[br] - mx)
            num = num + e * acc_s[br, rows, :]
            den = den + e * den_s[br, rows, :]
        o_ref[0, rows, :] = (num / den).astype(BF16)
        return carry

    lax.fori_loop(0, seq // chunk, merge, 0)


def _attn_a(qkv, bias, batch, seq):
    q1, k1, v1, q4, k4, v4, q16, k16, v16 = qkv
    aw = q1.shape[-1]
    npair = aw // LANES
    nat = pl.BlockSpec((1, seq, LANES), lambda b, p: (b, 0, p))
    fold = lambda dil: pl.BlockSpec((1, dil, seq // dil, LANES), lambda b, p: (b, 0, 0, p))
    r3 = lambda a: a.reshape(batch, seq, aw)
    return pl.pallas_call(
        functools.partial(_attn_a_kernel, seq=seq),
        grid=(batch, npair),
        in_specs=[nat, nat, nat, fold(4), fold(4), fold(4), fold(16), fold(16), fold(16),
                  pl.BlockSpec((3, 1, 2 * BLOCK, 2 * BLOCK), lambda b, p: (0, p, 0, 0))],
        out_specs=nat,
        out_shape=jax.ShapeDtypeStruct((batch, seq, aw), BF16),
        scratch_shapes=[pltpu.VMEM((3, seq, LANES), F32)] * 3,
        compiler_params=_cparams(2),
        name="attn_dilated",
    )(r3(q1), r3(k1), r3(v1), q4, k4, v4, q16, k16, v16, bias)


def _attn_c_kernel(sink_ref, q_ref, k_ref, v_ref, bias_ref, o_ref, *, seq, group):
    g = pl.program_id(1)
    npair = group // 2
    lo = lax.broadcasted_iota(I32, (BLOCK, LANES), 1) < HEAD_DIM

    def block(q0, k0, width, bias2):
        qn = q_ref[0, pl.ds(q0, BLOCK), :]
        q2 = jnp.concatenate([_stack_heads(qn[:, p * LANES:(p + 1) * LANES]) for p in range(npair)], axis=0)
        acc, m, den = _band_block(q2, k_ref[0, pl.ds(k0, width), :], v_ref[0, pl.ds(k0, width), :], bias2)
        for p in range(npair):
            outs = []
            for hh in range(2):
                h = 2 * p + hh
                r = slice(h * BLOCK, (h + 1) * BLOCK)
                outs.append(acc[r] / (den[r] + jnp.exp2(sink_ref[g * group + h] * LOG2E - m[r])))
            o_ref[0, pl.ds(q0, BLOCK), p * LANES:(p + 1) * LANES] = jnp.where(lo, outs[0], outs[1]).astype(BF16)

    block(0, 0, BLOCK, bias_ref[0, :, BLOCK:])

    def body(n, carry):
        block(pl.multiple_of(n * BLOCK, BLOCK), pl.multiple_of((n - 1) * BLOCK, BLOCK), 2 * BLOCK, bias_ref[0])
        return carry

    lax.fori_loop(1, seq // BLOCK, body, 0, unroll=C_UNROLL)


def _attn_c(q, k2, v2, bias, sinks, batch, seq, group):
    qw = q.shape[-1]
    n_kv = k2.shape[-1] // LANES
    gw = qw // n_kv
    return pl.pallas_call(
        functools.partial(_attn_c_kernel, seq=seq, group=group),
        grid=(batch, n_kv),
        in_specs=[pl.BlockSpec(memory_space=pltpu.SMEM),
                  pl.BlockSpec((1, seq, gw), lambda b, g: (b, 0, g)),
                  pl.BlockSpec((1, seq, LANES), lambda b, g: (b, 0, g)),
                  pl.BlockSpec((1, seq, LANES), lambda b, g: (b, 0, g)),
                  pl.BlockSpec((1, group * BLOCK, 2 * BLOCK), lambda b, g: (g, 0, 0))],
        out_specs=pl.BlockSpec((1, seq, gw), lambda b, g: (b, 0, g)),
        out_shape=jax.ShapeDtypeStruct((batch, seq, qw), BF16),
        compiler_params=_cparams(2),
        name="attn_swa",
    )(sinks, q, k2, v2, bias)


class _Job:
    def __init__(self, i, nchs, slot, qs_ref, wt_ref, q2_ref, sk_ref, mask_ref):
        self.i, self.nchs = i, nchs
        self.qs, self.wt, self.q2 = qs_ref.at[slot], wt_ref.at[slot], q2_ref.at[slot]
        self.sk, self.mask = sk_ref.at[slot], mask_ref.at[slot]
        key_pos = lax.broadcasted_iota(I32, (CW, BLOCK), 0) + (nchs - 1) * CW
        qry_pos = lax.broadcasted_iota(I32, (CW, BLOCK), 1) + i * BLOCK
        self.causal_last = key_pos <= qry_pos


def _dsa_scores(job, ki_ref):
    for c in range(job.nchs):
        st = _dot_nt(ki_ref[0, c * CW:(c + 1) * CW, :], job.qs[...])
        score = None
        for h in range(IDX_HEADS):
            term = jnp.maximum(st[:, h * BLOCK:(h + 1) * BLOCK], 0.0) * job.wt[h:h + 1, :]
            score = term if score is None else score + term
        if c == job.nchs - 1:
            score = jnp.where(job.causal_last, score, -jnp.inf)
        job.sk[c] = score


def _key_to_float(key):
    sk = key ^ INT_MIN
    return lax.bitcast_convert_type(sk ^ (lax.shift_right_arithmetic(sk, 31) & 0x7FFFFFFF), F32)


def _dsa_count(job, pred):
    grp = 8 * 8
    acc = jnp.zeros((grp, BLOCK), F32)
    for c in range(job.nchs):
        for g in range(CW // grp):
            acc = jnp.where(pred(job.sk[c, g * grp:(g + 1) * grp, :]), acc + 1.0, acc)
    return jnp.sum(acc, axis=0, keepdims=True)


def _dsa_select(job, thr, n_ge, topk):
    for c in range(job.nchs):
        sel = job.sk[c] >= thr
        if c == job.nchs - 1:
            sel = sel & job.causal_last
        job.mask[c] = jnp.where(sel, 0.0, NEG)
    return jnp.max(jnp.where((n_ge > float(topk)) & (thr > -jnp.inf), 1.0, 0.0))


def _dsa_break_ties(job, thr, topk):
    need = float(topk) - _dsa_count(job, lambda x: x > thr)
    r_i = lax.broadcasted_iota(I32, (BLOCK, BLOCK), 0)
    c_i = lax.broadcasted_iota(I32, (BLOCK, BLOCK), 1)
    lower = jnp.where(c_i <= r_i, 1.0, 0.0).astype(BF16)
    carry = jnp.zeros((1, BLOCK), F32)
    for c in range(job.nchs):
        for t in range(CW // BLOCK):
            rows = slice(t * BLOCK, (t + 1) * BLOCK)
            xs = job.sk[c, rows, :]
            eq = xs == thr
            eqf = jnp.where(eq, 1.0, 0.0)
            rank = _dot(lower, eqf.astype(BF16)) + carry
            keep = (xs > thr) | (eq & (rank <= need))
            causal = r_i + (c * CW + t * BLOCK) <= c_i + job.i * BLOCK
            job.mask[c, rows, :] = jnp.where(keep & causal, 0.0, NEG)
            carry = carry + jnp.sum(eqf, axis=0, keepdims=True)


def _dsa_attend(job, kb_ref, vt_ref, bias_ref, o_ref, pairs):
    i, nchs, mask_ref, q2_ref = job.i, job.nchs, job.mask, job.q2
    tpc = CW // BLOCK
    out_rows = pl.ds(pl.multiple_of(i * BLOCK, BLOCK), BLOCK)
    first_half = lax.broadcasted_iota(I32, (LANES, BLOCK), 0) < HEAD_DIM
    for p in pairs:
        lanes = slice(p * LANES, (p + 1) * LANES)
        logits = []
        for c in range(nchs):
            s = _dot_nt(kb_ref[0, c * CW:(c + 1) * CW, lanes], q2_ref[p])
            msk = mask_ref[c]
            per_head = []
            for hh in range(2):
                bias = jnp.concatenate(
                    [bias_ref[2 * p + hh, jnp.maximum(i - (c * tpc + t), 0)] for t in range(tpc)], axis=0)
                per_head.append(s[:, hh * BLOCK:(hh + 1) * BLOCK] + bias + msk)
            logits.append(per_head)
        pes = [[None, None] for _ in range(nchs)]
        for hh in range(2):
            m = None
            for c in range(nchs):
                mc = jnp.max(logits[c][hh], axis=0, keepdims=True)
                m = mc if m is None else jnp.maximum(m, mc)
            for c in range(nchs):
                pes[c][hh] = jnp.exp2(logits[c][hh] - m).astype(BF16)
        out_t = jnp.zeros((LANES + ONES_ROWS, 2 * BLOCK), F32)
        ones = jnp.ones((ONES_ROWS, CW), BF16)
        for c in range(nchs):
            vt1 = jnp.concatenate([vt_ref[0, lanes, c * CW:(c + 1) * CW], ones], axis=0)
            out_t = out_t + _dot(vt1, jnp.concatenate(pes[c], axis=1))
        o_t = out_t[:LANES] / out_t[LANES:LANES + 1]
        o_ref[0, out_rows, lanes] = jnp.where(first_half, o_t[:, :BLOCK], o_t[:, BLOCK:]).T.astype(BF16)


def _dsa_kernel(qi_ref, wi_ref, ki_ref, qb_ref, kb_ref, vt_ref, bias_ref, o_ref,
                qs_ref, wt_ref, q2_ref, sk_ref, mask_ref, *, topk, n_heads, n_blocks):
    j = pl.program_id(1)
    npair = n_heads // 2
    tpc = CW // BLOCK
    span = 2 * n_blocks // DSA_BLOCKS
    blocks, counts = [], []
    for base in range(0, n_blocks, span):
        blocks += [base + j, base + span - 1 - j]
        counts.append(lambda v, base=base: (base // tpc + v + 1, (base + span) // tpc - v))

    for slot, i in enumerate(blocks):
        rows = pl.ds(pl.multiple_of(i * BLOCK, BLOCK), BLOCK)
        qi = qi_ref[0, rows, :]
        lane_i = _lane_id_bf16(qi.shape)
        for h in range(IDX_HEADS):
            in_head = (lane_i >= h * IDX_DIM) & (lane_i < (h + 1) * IDX_DIM)
            qs_ref[slot, h * BLOCK:(h + 1) * BLOCK, :] = jnp.where(in_head, qi, jnp.zeros_like(qi))
        wt_ref[slot] = wi_ref[0, rows, :].T
        for p in range(npair):
            q2_ref[slot, p] = _stack_heads(qb_ref[0, rows, p * LANES:(p + 1) * LANES])

    for v in range(span // (2 * tpc)):
        @pl.when(j // tpc == v)
        def _(v=v):
            nchs_all = [n for count in counts for n in count(v)]
            jobs = [_Job(i, nchs, slot, qs_ref, wt_ref, q2_ref, sk_ref, mask_ref)
                    for slot, (i, nchs) in enumerate(zip(blocks, nchs_all))]
            for job in jobs:
                _dsa_scores(job, ki_ref)

            def search(it, state):
                bit = lax.shift_left(jnp.int32(1), 31 - it)
                out = []
                for job, (ans, n_ans) in zip(jobs, state):
                    n = _dsa_count(job, lambda x, c=_key_to_float(ans | bit): x >= c)
                    keep = n >= float(topk)
                    out.append((jnp.where(keep, ans | bit, ans), jnp.where(keep, n, n_ans)))
                return tuple(out)

            start = tuple((jnp.zeros((1, BLOCK), I32), jnp.zeros((1, BLOCK), F32)) for _ in jobs)
            found = lax.fori_loop(0, 32, search, start)
            thrs = [jnp.where(ans == 0, -jnp.inf, _key_to_float(ans)) for ans, _ in found]
            surplus = [_dsa_select(job, thr, n_ge, topk) for job, thr, (_, n_ge) in zip(jobs, thrs, found)]

            @pl.when(functools.reduce(jnp.maximum, surplus) > 0.0)
            def _():
                for job, thr in zip(jobs, thrs):
                    _dsa_break_ties(job, thr, topk)

            for p in range(npair):
                for job in sorted(jobs, key=lambda jb: -jb.nchs):
                    _dsa_attend(job, kb_ref, vt_ref, bias_ref, o_ref, (p,))


def _dsa(qi, wi, ki, qb, kb, vt, bias, batch, seq, topk):
    bw = qb.shape[-1]
    nidx = qi.shape[-1]
    nb = seq // BLOCK
    n_heads = bw // HEAD_DIM
    npair = n_heads // 2
    assert (2 * nb // DSA_BLOCKS) % (2 * (CW // BLOCK)) == 0
    full = lambda width: pl.BlockSpec((1, seq, width), lambda b, j: (b, 0, 0))
    r3 = lambda a: a.reshape(batch, seq, a.shape[-1])
    vm = lambda shape, dtype: pltpu.VMEM((DSA_BLOCKS,) + shape, dtype)
    out = pl.pallas_call(
        functools.partial(_dsa_kernel, topk=topk, n_heads=n_heads, n_blocks=nb),
        grid=(batch, nb // DSA_BLOCKS),
        in_specs=[full(nidx), full(LANES), full(nidx), full(bw), full(bw),
                  pl.BlockSpec((1, bw, seq), lambda b, j: (b, 0, 0)),
                  _const_spec((n_heads, nb, BLOCK, BLOCK))],
        out_specs=full(bw),
        out_shape=jax.ShapeDtypeStruct((batch, seq, bw), BF16),
        scratch_shapes=[vm((IDX_HEADS * BLOCK, nidx), BF16), vm((LANES, BLOCK), F32),
                        vm((npair, 2 * BLOCK, LANES), BF16),
                        vm((seq // CW, CW, BLOCK), F32), vm((seq // CW, CW, BLOCK), F32)],
        compiler_params=_cparams(2),
        name="dsa",
    )(r3(qi), r3(wi), r3(ki), r3(qb), r3(kb), vt, bias)
    return out.reshape(batch * seq, bw)


def _split_rows(w_out, parts):
    ws, r0 = [], 0
    for a in parts:
        ws.append(w_out[r0:r0 + a.shape[1]].astype(BF16))
        r0 += a.shape[1]
    return ws


def _odd_proj_weights(w_in, qw, kvw):
    d = w_in.shape[0]
    dup = lambda w: jnp.repeat(w.reshape(d, kvw // HEAD_DIM, 1, HEAD_DIM), 2, axis=2).reshape(d, 2 * kvw)
    return jnp.concatenate([w_in[:, :qw], dup(w_in[:, qw:qw + kvw]), dup(w_in[:, qw + kvw:])], axis=1).astype(BF16)


def kernel(x, norm_g, final_g, rel_bias_table, ffn_w1, ffn_w3, ffn_w2, hyb_w_in, hyb_cq_g, hyb_wq_b,
           hyb_wq_idx, hyb_w_out, swa_w_in, swa_sinks, swa_w_out):
    batch, seq, d = x.shape
    depth = norm_g.shape[0]
    n_heads = d // HEAD_DIM
    a_heads = n_heads // 2
    topk = min(TOPK_MAX, seq // 4)
    assert seq % (BLOCK * 16) == 0 and seq % CW == 0 and (batch * seq) % TM_FFN == 0 and seq % TM == 0

    bias_a = _build_bias(
        rel_bias_table,
        np.concatenate([_band_bucket_idx(dil, win // dil) for win, dil in A_BRANCHES], axis=0),
        0, a_heads).reshape(a_heads // 2, 2, 3, BLOCK, 2 * BLOCK).transpose(2, 0, 1, 3, 4)
    bias_a = bias_a.reshape(3, a_heads // 2, 2 * BLOCK, 2 * BLOCK)
    bias_b = _build_bias(rel_bias_table, _dense_bucket_idx(seq), a_heads, n_heads - a_heads)
    bias_b = bias_b.reshape(n_heads - a_heads, seq // BLOCK, BLOCK, BLOCK)
    bias_c = _build_bias(rel_bias_table, _band_bucket_idx(1, C_WINDOW - 1), 0, n_heads)

    ffn_w = (ffn_w1.astype(BF16), ffn_w3.astype(BF16), ffn_w2.astype(BF16))
    h = x.reshape(batch * seq, d)
    for layer in range(depth):
        g = norm_g[layer]
        i = layer // 2
        ffn_a = (g[0], ffn_w, (layer, 0))
        ffn_b = (g[2], ffn_w, (layer, 1))
        if layer % 2 == 0:
            h = _ffn(h, *ffn_a, final_g)
            outs = _proj_even(h, g[1], hyb_w_in[i], hyb_cq_g[i], hyb_wq_b[i], hyb_wq_idx[i], batch, seq)
            o_a = _attn_a(outs[:9], bias_a, batch, seq)
            qb, kb, vt, qi, ki, wi = outs[9:]
            o_b = _dsa(qi, wi, ki, qb, kb, vt, bias_b, batch, seq, topk)
            mix, w_out = [o_a.reshape(batch * seq, -1), o_b], hyb_w_out[i]
        else:
            kvw = (swa_w_in.shape[-1] - d) // 2
            n_kv = kvw // HEAD_DIM
            group = n_heads // n_kv
            h, q, k2, v2 = _ffn(h, *ffn_a, g[1], post_w=_odd_proj_weights(swa_w_in[i], d, kvw),
                                post_splits=((d, Q_SCALE), (2 * kvw, 1.0), (2 * kvw, 1.0)))
            r3 = lambda a: a.reshape(batch, seq, a.shape[-1])
            o_c = _attn_c(r3(q), r3(k2), r3(v2), bias_c.reshape(n_kv, group * BLOCK, 2 * BLOCK),
                          swa_sinks[i].astype(F32), batch, seq, group)
            mix, w_out = [o_c.reshape(batch * seq, -1)], swa_w_out[i]
        h = _ffn(h, *ffn_b, final_g, mix=mix, mix_w=_split_rows(w_out, mix), final=layer == depth - 1)
    return h.reshape(batch, seq, d)
```

```python
import functools
import math

import numpy as np
import jax
import jax.numpy as jnp
from jax import lax
from jax.experimental import pallas as pl
from jax.experimental.pallas import tpu as pltpu

F32 = jnp.float32
BF16 = jnp.bfloat16
I32 = jnp.int32

EPS = 1e-6
HEAD_DIM = 64
BLOCK = 128
LANES = 128
NUM_BUCKETS = 32
MAX_DISTANCE = 2048
A_BRANCHES = ((128, 1), (512, 4), (2048, 16))
C_WINDOW = 128
TOPK_MAX = 256
IDX_HEADS = 8
IDX_DIM = 32
NEG = -1e30
LOG2E = math.log2(math.e)
Q_SCALE = HEAD_DIM ** -0.5 * LOG2E
INT_MIN = -(2 ** 31)
VMEM_LIMIT = 56 * 1024 * 1024

TM = 1024
TM_FFN = 1024
CW = 2 * BLOCK
DSA_BLOCKS = 4
C_UNROLL = 15
ONES_ROWS = 16


def _cparams(n_axes):
    return pltpu.CompilerParams(
        dimension_semantics=("arbitrary",) * n_axes, vmem_limit_bytes=VMEM_LIMIT)


def _rms(x, g):
    return x * lax.rsqrt(jnp.mean(x * x, axis=-1, keepdims=True) + EPS) * g


def _dot(a, b):
    return jnp.dot(a, b, preferred_element_type=F32)


def _dot_nt(a, b):
    return lax.dot_general(a, b, (((1,), (1,)), ((), ())), preferred_element_type=F32)


def _lane_id_bf16(shape):
    return lax.broadcasted_iota(I32, shape, 1).astype(F32).astype(BF16)


def _const_spec(shape):
    nd = len(shape)
    return pl.BlockSpec(shape, lambda *_: (0,) * nd, pipeline_mode=pl.Buffered(1))


def _t5_bucket_np(dist):
    dist = np.maximum(dist, 0)
    max_exact = NUM_BUCKETS // 2
    d_f = np.maximum(dist, 1).astype(np.float32)
    large = max_exact + (np.log(d_f / max_exact) / math.log(MAX_DISTANCE / max_exact)
                         * (NUM_BUCKETS - max_exact)).astype(np.int32)
    large = np.minimum(large, NUM_BUCKETS - 1)
    return np.where(dist < max_exact, dist, large).astype(np.int32)


def _band_bucket_idx(dil, max_dist):
    loc = np.arange(BLOCK)[:, None] + BLOCK - np.arange(2 * BLOCK)[None, :]
    valid = (loc >= 0) & (loc <= max_dist)
    return np.where(valid, _t5_bucket_np(dil * loc), -1).astype(np.int32)


def _dense_bucket_idx(seq):
    nb = seq // BLOCK
    d = np.arange(nb)[:, None, None] * BLOCK - np.arange(BLOCK)[None, :, None] + np.arange(BLOCK)[None, None, :]
    return np.where(d >= 0, _t5_bucket_np(d), -1).astype(np.int32).reshape(seq, BLOCK)


def _bias_kernel(tab_ref, idx_ref, o_ref, *, head0, buckets):
    h = pl.program_id(0) + head0
    for blk, present in enumerate(buckets):
        rows = slice(blk * BLOCK, (blk + 1) * BLOCK)
        idx = idx_ref[rows, :]
        out = jnp.full(idx.shape, NEG, F32)
        for b in present:
            out = jnp.where(idx == b, tab_ref[b, h] * LOG2E, out)
        o_ref[0, rows, :] = out


def _build_bias(table, idx_np, head0, n_heads):
    r, c = idx_np.shape
    buckets = tuple(tuple(int(b) for b in np.unique(idx_np[r0:r0 + BLOCK]) if b >= 0) for r0 in range(0, r, BLOCK))
    return pl.pallas_call(
        functools.partial(_bias_kernel, head0=head0, buckets=buckets),
        grid=(n_heads,),
        in_specs=[pl.BlockSpec(memory_space=pltpu.SMEM),
                  pl.BlockSpec((r, c), lambda h: (0, 0))],
        out_specs=pl.BlockSpec((1, r, c), lambda h: (h, 0, 0)),
        out_shape=jax.ShapeDtypeStruct((n_heads, r, c), F32),
        compiler_params=_cparams(1),
        name="bias_build",
    )(table.astype(F32), jnp.asarray(idx_np))


def _ffn_kernel(*refs, n_mix, final, post_splits):
    it = iter(refs)
    x_ref = next(it)
    mix = [next(it) for _ in range(n_mix)]
    mix_w = [next(it) for _ in range(n_mix)]
    g_ref, w1_ref, w3_ref, w2_ref, g2_ref = (next(it) for _ in range(5))
    wp_ref = next(it) if post_splits else None
    o_ref = next(it)
    post_refs = [next(it) for _ in post_splits]
    x = x_ref[...]
    for a_ref, w_ref in zip(mix, mix_w):
        x = x + _dot(a_ref[...], w_ref[...])
    xn = _rms(x, g_ref[...]).astype(BF16)
    h1 = _dot(xn, w1_ref[...])
    h3 = _dot(xn, w3_ref[...])
    act = (h1 * (1.0 / (1.0 + jnp.exp(-h1)))) * h3
    y = _dot(act.astype(BF16), w2_ref[...])
    out = x + 0.5 * y
    if final:
        out = _rms(out, g2_ref[...])
    o_ref[...] = out
    if post_splits:
        p = _dot(_rms(out, g2_ref[...]).astype(BF16), wp_ref[...])
        c0 = 0
        for p_ref, (width, scale) in zip(post_refs, post_splits):
            blk = p[:, c0:c0 + width]
            p_ref[...] = (blk if scale == 1.0 else blk * scale).astype(BF16)
            c0 += width


def _ffn(h, g, ffn_w, which, g2, *, mix=(), mix_w=(), final=False, post_w=None, post_splits=()):
    t, d = h.shape
    dff = ffn_w[0].shape[-1]
    tok = lambda width: pl.BlockSpec((TM_FFN, width), lambda i: (i, 0))
    pick = lambda rows, cols: pl.BlockSpec((None, None, rows, cols), lambda i: (*which, 0, 0),
                                           pipeline_mode=pl.Buffered(1))
    sh = jax.ShapeDtypeStruct
    in_specs = [tok(d)] + [tok(a.shape[1]) for a in mix] + [_const_spec(w.shape) for w in mix_w]
    in_specs += [_const_spec((1, d)), pick(d, dff), pick(d, dff), pick(dff, d), _const_spec((1, d))]
    args = [h, *mix, *mix_w, g.reshape(1, d), *ffn_w, g2.reshape(1, d)]
    out_specs, out_shape = [tok(d)], [sh((t, d), F32)]
    if post_splits:
        in_specs.append(_const_spec(post_w.shape))
        args.append(post_w)
        out_specs += [tok(width) for width, _ in post_splits]
        out_shape += [sh((t, width), BF16) for width, _ in post_splits]
    res = pl.pallas_call(
        functools.partial(_ffn_kernel, n_mix=len(mix), final=final, post_splits=tuple(post_splits)),
        grid=(t // TM_FFN,),
        in_specs=in_specs,
        out_specs=out_specs,
        out_shape=out_shape,
        compiler_params=_cparams(1),
        name="ffn",
    )(*args)
    return res if post_splits else res[0]


def _fold_store(scr, scr4, blk, outs, n_slab):
    o1, o4, o16 = outs
    for c in range(n_slab):
        slab = blk[:, c * LANES:(c + 1) * LANES]
        o1[c] = slab.astype(BF16)
        scr[c] = slab
    rows4, rows16 = TM // 4, TM // 16
    for r4 in range(4):
        for c in range(n_slab):
            f4 = scr[c, pl.ds(r4, rows4, stride=4), :]
            o4[c, 0, r4] = f4.astype(BF16)
            scr4[c] = f4
            for q in range(4):
                o16[c, 0, 4 * q + r4] = scr4[c, pl.ds(q, rows16, stride=4), :].astype(BF16)


def _proj_even_kernel(h_ref, g_ref, w_ref, cqg_ref, wqb_ref, wqi_ref,
                      q1, k1, v1, q4, k4, v4, q16, k16, v16, qb, kb, vb, qi, ki, wi, scr, scr4,
                      *, aw, bw, rank):
    u = _rms(h_ref[...], g_ref[...]).astype(BF16)
    p = _dot(u, w_ref[...])
    proj = lambda c0, width: p[:, c0:c0 + width]
    n_slab = aw // LANES
    scale = Q_SCALE
    _fold_store(scr, scr4, proj(0, aw) * scale, (q1, q4, q16), n_slab)
    _fold_store(scr, scr4, proj(aw, aw), (k1, k4, k16), n_slab)
    _fold_store(scr, scr4, proj(2 * aw, aw), (v1, v4, v16), n_slab)
    c0 = 3 * aw
    kb[...] = proj(c0, bw).astype(BF16)
    vb[0] = proj(c0 + bw, bw).T.astype(BF16)
    c1 = c0 + 2 * bw
    cq = _rms(proj(c1, rank), cqg_ref[...]).astype(BF16)
    qb[...] = (_dot(cq, wqb_ref[...]) * scale).astype(BF16)
    qi[...] = _dot(cq, wqi_ref[...]).astype(BF16)
    c2 = c1 + rank
    nidx = IDX_HEADS * IDX_DIM
    ki[...] = proj(c2, nidx).astype(BF16)
    idx_scale = (IDX_DIM ** -0.5) * (IDX_HEADS ** -0.5)
    wi[...] = proj(c2 + nidx, LANES) * idx_scale


def _proj_even(h, g, w_in, cq_g, wq_b, wq_idx, batch, seq):
    t, d = h.shape
    bw = wq_b.shape[1]
    rank = wq_b.shape[0]
    aw = (w_in.shape[1] - rank - 2 * bw - IDX_DIM - IDX_HEADS) // 3
    nidx = IDX_HEADS * IDX_DIM
    offs = np.cumsum([0, aw, aw, aw, rank, bw, bw, IDX_DIM, IDX_HEADS])
    col = lambda i: w_in[:, offs[i]:offs[i + 1]]
    w_kidx = jnp.tile(col(6), (1, IDX_HEADS))
    w_widx = jnp.pad(col(7), ((0, 0), (0, LANES - IDX_HEADS)))
    w_all = jnp.concatenate([col(0), col(1), col(2), col(4), col(5), col(3), w_kidx, w_widx], axis=1).astype(BF16)
    ncol = w_all.shape[1]
    spt = seq // TM
    tok = lambda width: pl.BlockSpec((TM, width), lambda i: (i, 0))
    npair = aw // LANES
    pair_nat = pl.BlockSpec((npair, TM, LANES), lambda i: (0, i, 0))
    fold = lambda dil: pl.BlockSpec((npair, 1, dil, TM // dil, LANES), lambda i: (0, i // spt, 0, i % spt, 0))
    sh = jax.ShapeDtypeStruct
    nat = sh((npair, t, LANES), BF16)
    f4 = sh((npair, batch, 4, seq // 4, LANES), BF16)
    f16 = sh((npair, batch, 16, seq // 16, LANES), BF16)
    return pl.pallas_call(
        functools.partial(_proj_even_kernel, aw=aw, bw=bw, rank=rank),
        grid=(t // TM,),
        in_specs=[tok(d), _const_spec((1, d)), _const_spec((d, ncol)), _const_spec((1, rank)),
                  _const_spec((rank, bw)), _const_spec((rank, nidx))],
        out_specs=[pair_nat, pair_nat, pair_nat, fold(4), fold(4), fold(4), fold(16), fold(16), fold(16),
                   tok(bw), tok(bw), pl.BlockSpec((1, bw, TM), lambda i: (i // spt, 0, i % spt)),
                   tok(nidx), tok(nidx), tok(LANES)],
        out_shape=[nat, nat, nat, f4, f4, f4, f16, f16, f16,
                   sh((t, bw), BF16), sh((t, bw), BF16), sh((batch, bw, seq), BF16),
                   sh((t, nidx), BF16), sh((t, nidx), BF16), sh((t, LANES), F32)],
        scratch_shapes=[pltpu.VMEM((aw // LANES, TM, LANES), F32), pltpu.VMEM((aw // LANES, TM // 4, LANES), F32)],
        compiler_params=_cparams(1),
        name="proj_even",
    )(h, g.reshape(1, d), w_all, cq_g.reshape(1, rank), wq_b.astype(BF16), wq_idx.astype(BF16))


def _stack_heads(q):
    lane_b = _lane_id_bf16(q.shape)
    z = jnp.zeros_like(q)
    return jnp.concatenate([jnp.where(lane_b < HEAD_DIM, q, z), jnp.where(lane_b >= HEAD_DIM, q, z)], axis=0)


def _band_block(q2, kw, vw, bias2):
    s = _dot_nt(q2, kw) + bias2
    m = jnp.max(s, axis=-1, keepdims=True)
    p = jnp.exp2(s - m)
    den = jnp.sum(p, axis=-1, keepdims=True)
    return _dot(p.astype(BF16), vw), m, den


def _attn_a_kernel(q1, k1, v1, q4, k4, v4, q16, k16, v16, bias_ref, o_ref, acc_s, m_s, den_s, *, seq):
    lo = lax.broadcasted_iota(I32, (BLOCK, LANES), 1) < HEAD_DIM
    shape = (BLOCK, LANES)
    refs = ((q1, k1, v1), (q4, k4, v4), (q16, k16, v16))
    for br, (_, dil) in enumerate(A_BRANCHES):
        qr, kr, vr = refs[br]
        for r in range(dil):
            at = (lambda ref, s, n: ref[0, pl.ds(s, n), :]) if dil == 1 else \
                 (lambda ref, s, n, r=r: ref[0, r, pl.ds(s, n), :])
            for n in range(seq // dil // BLOCK):
                q2 = _stack_heads(at(qr, n * BLOCK, BLOCK))
                if n == 0:
                    acc, m, den = _band_block(q2, at(kr, 0, BLOCK), at(vr, 0, BLOCK), bias_ref[br, 0, :, BLOCK:])
                else:
                    k0 = (n - 1) * BLOCK
                    acc, m, den = _band_block(q2, at(kr, k0, 2 * BLOCK), at(vr, k0, 2 * BLOCK), bias_ref[br, 0])
                rows = pl.ds(n * BLOCK, BLOCK) if dil == 1 else pl.ds(n * BLOCK * dil + r, BLOCK, stride=dil)
                acc_s[br, rows, :] = jnp.where(lo, acc[:BLOCK], acc[BLOCK:])
                m_s[br, rows, :] = jnp.where(lo, jnp.broadcast_to(m[:BLOCK], shape), jnp.broadcast_to(m[BLOCK:], shape))
                den_s[br, rows, :] = jnp.where(lo, jnp.broadcast_to(den[:BLOCK], shape),
                                               jnp.broadcast_to(den[BLOCK:], shape))

    chunk = 2 * BLOCK

    def merge(c, carry):
        rows = pl.ds(pl.multiple_of(c * chunk, chunk), chunk)
        ms = [m_s[br, rows, :] for br in range(3)]
        mx = jnp.maximum(jnp.maximum(ms[0], ms[1]), ms[2])
        num = jnp.zeros((chunk, LANES), F32)
        den = jnp.zeros((chunk, LANES), F32)
        for br in range(3):
            e = jnp.exp2(ms[br] - mx)
            num = num + e * acc_s[br, rows, :]
            den = den + e * den_s[br, rows, :]
        o_ref[0, rows, :] = (num / den).astype(BF16)
        return carry

    lax.fori_loop(0, seq // chunk, merge, 0)


def _attn_a(qkv, bias, batch, seq):
    q1, k1, v1, q4, k4, v4, q16, k16, v16 = qkv
    npair = q1.shape[0]
    aw = npair * LANES
    nat_in = pl.BlockSpec((None, 1, seq, LANES), lambda b, p: (p, b, 0, 0))
    nat = pl.BlockSpec((1, seq, LANES), lambda b, p: (b, 0, p))
    fold = lambda dil: pl.BlockSpec((None, 1, dil, seq // dil, LANES), lambda b, p: (p, b, 0, 0, 0))
    r3 = lambda a: a.reshape(npair, batch, seq, LANES)
    return pl.pallas_call(
        functools.partial(_attn_a_kernel, seq=seq),
        grid=(batch, npair),
        in_specs=[nat_in, nat_in, nat_in, fold(4), fold(4), fold(4), fold(16), fold(16), fold(16),
                  pl.BlockSpec((3, 1, 2 * BLOCK, 2 * BLOCK), lambda b, p: (0, p, 0, 0))],
        out_specs=nat,
        out_shape=jax.ShapeDtypeStruct((batch, seq, aw), BF16),
        scratch_shapes=[pltpu.VMEM((3, seq, LANES), F32)] * 3,
        compiler_params=_cparams(2),
        name="attn_dilated",
    )(r3(q1), r3(k1), r3(v1), q4, k4, v4, q16, k16, v16, bias)


def _attn_c_kernel(sink_ref, q_ref, k_ref, v_ref, bias_ref, o_ref, *, seq, group):
    g = pl.program_id(1)
    npair = group // 2
    lo = lax.broadcasted_iota(I32, (BLOCK, LANES), 1) < HEAD_DIM

    def block(q0, k0, width, bias2):
        qn = q_ref[0, pl.ds(q0, BLOCK), :]
        q2 = jnp.concatenate([_stack_heads(qn[:, p * LANES:(p + 1) * LANES]) for p in range(npair)], axis=0)
        acc, m, den = _band_block(q2, k_ref[0, pl.ds(k0, width), :], v_ref[0, pl.ds(k0, width), :], bias2)
        for p in range(npair):
            outs = []
            for hh in range(2):
                h = 2 * p + hh
                r = slice(h * BLOCK, (h + 1) * BLOCK)
                outs.append(acc[r] / (den[r] + jnp.exp2(sink_ref[g * group + h] * LOG2E - m[r])))
            o_ref[0, pl.ds(q0, BLOCK), p * LANES:(p + 1) * LANES] = jnp.where(lo, outs[0], outs[1]).astype(BF16)

    block(0, 0, BLOCK, bias_ref[0, :, BLOCK:])

    def body(n, carry):
        block(pl.multiple_of(n * BLOCK, BLOCK), pl.multiple_of((n - 1) * BLOCK, BLOCK), 2 * BLOCK, bias_ref[0])
        return carry

    lax.fori_loop(1, seq // BLOCK, body, 0, unroll=C_UNROLL)


def _attn_c(q, k2, v2, bias, sinks, batch, seq, group):
    qw = q.shape[-1]
    n_kv = k2.shape[-1] // LANES
    gw = qw // n_kv
    return pl.pallas_call(
        functools.partial(_attn_c_kernel, seq=seq, group=group),
        grid=(batch, n_kv),
        in_specs=[pl.BlockSpec(memory_space=pltpu.SMEM),
                  pl.BlockSpec((1, seq, gw), lambda b, g: (b, 0, g)),
                  pl.BlockSpec((1, seq, LANES), lambda b, g: (b, 0, g)),
                  pl.BlockSpec((1, seq, LANES), lambda b, g: (b, 0, g)),
                  pl.BlockSpec((1, group * BLOCK, 2 * BLOCK), lambda b, g: (g, 0, 0))],
        out_specs=pl.BlockSpec((1, seq, gw), lambda b, g: (b, 0, g)),
        out_shape=jax.ShapeDtypeStruct((batch, seq, qw), BF16),
        compiler_params=_cparams(2),
        name="attn_swa",
    )(sinks, q, k2, v2, bias)


class _Job:
    def __init__(self, i, nchs, slot, qs_ref, wt_ref, q2_ref, sk_ref, mask_ref):
        self.i, self.nchs = i, nchs
        self.qs, self.wt, self.q2 = qs_ref.at[slot], wt_ref.at[slot], q2_ref.at[slot]
        self.sk, self.mask = sk_ref.at[slot], mask_ref.at[slot]
        key_pos = lax.broadcasted_iota(I32, (CW, BLOCK), 0) + (nchs - 1) * CW
        qry_pos = lax.broadcasted_iota(I32, (CW, BLOCK), 1) + i * BLOCK
        self.causal_last = key_pos <= qry_pos


def _dsa_scores(job, ki_ref):
    for c in range(job.nchs):
        st = _dot_nt(ki_ref[0, c * CW:(c + 1) * CW, :], job.qs[...])
        score = None
        for h in range(IDX_HEADS):
            term = jnp.maximum(st[:, h * BLOCK:(h + 1) * BLOCK], 0.0) * job.wt[h:h + 1, :]
            score = term if score is None else score + term
        if c == job.nchs - 1:
            score = jnp.where(job.causal_last, score, -jnp.inf)
        job.sk[c] = score


def _key_to_float(key):
    sk = key ^ INT_MIN
    return lax.bitcast_convert_type(sk ^ (lax.shift_right_arithmetic(sk, 31) & 0x7FFFFFFF), F32)


def _dsa_count(job, pred):
    grp = 8 * 8
    acc = jnp.zeros((grp, BLOCK), F32)
    for c in range(job.nchs):
        for g in range(CW // grp):
            acc = jnp.where(pred(job.sk[c, g * grp:(g + 1) * grp, :]), acc + 1.0, acc)
    return jnp.sum(acc, axis=0, keepdims=True)


def _dsa_select(job, thr, n_ge, topk):
    for c in range(job.nchs):
        sel = job.sk[c] >= thr
        if c == job.nchs - 1:
            sel = sel & job.causal_last
        job.mask[c] = jnp.where(sel, 0.0, NEG)
    return jnp.max(jnp.where((n_ge > float(topk)) & (thr > -jnp.inf), 1.0, 0.0))


def _dsa_break_ties(job, thr, topk):
    need = float(topk) - _dsa_count(job, lambda x: x > thr)
    r_i = lax.broadcasted_iota(I32, (BLOCK, BLOCK), 0)
    c_i = lax.broadcasted_iota(I32, (BLOCK, BLOCK), 1)
    lower = jnp.where(c_i <= r_i, 1.0, 0.0).astype(BF16)
    carry = jnp.zeros((1, BLOCK), F32)
    for c in range(job.nchs):
        for t in range(CW // BLOCK):
            rows = slice(t * BLOCK, (t + 1) * BLOCK)
            xs = job.sk[c, rows, :]
            eq = xs == thr
            eqf = jnp.where(eq, 1.0, 0.0)
            rank = _dot(lower, eqf.astype(BF16)) + carry
            keep = (xs > thr) | (eq & (rank <= need))
            causal = r_i + (c * CW + t * BLOCK) <= c_i + job.i * BLOCK
            job.mask[c, rows, :] = jnp.where(keep & causal, 0.0, NEG)
            carry = carry + jnp.sum(eqf, axis=0, keepdims=True)


def _dsa_attend(job, kb_ref, vt_ref, bias_ref, o_ref, pairs):
    i, nchs, mask_ref, q2_ref = job.i, job.nchs, job.mask, job.q2
    tpc = CW // BLOCK
    out_rows = pl.ds(pl.multiple_of(i * BLOCK, BLOCK), BLOCK)
    first_half = lax.broadcasted_iota(I32, (LANES, BLOCK), 0) < HEAD_DIM
    for p in pairs:
        lanes = slice(p * LANES, (p + 1) * LANES)
        logits = []
        for c in range(nchs):
            s = _dot_nt(kb_ref[0, c * CW:(c + 1) * CW, lanes], q2_ref[p])
            msk = mask_ref[c]
            per_head = []
            for hh in range(2):
                bias = jnp.concatenate(
                    [bias_ref[2 * p + hh, jnp.maximum(i - (c * tpc + t), 0)] for t in range(tpc)], axis=0)
                per_head.append(s[:, hh * BLOCK:(hh + 1) * BLOCK] + bias + msk)
            logits.append(per_head)
        pes = [[None, None] for _ in range(nchs)]
        for hh in range(2):
            m = None
            for c in range(nchs):
                mc = jnp.max(logits[c][hh], axis=0, keepdims=True)
                m = mc if m is None else jnp.maximum(m, mc)
            for c in range(nchs):
                pes[c][hh] = jnp.exp2(logits[c][hh] - m).astype(BF16)
        out_t = jnp.zeros((LANES + ONES_ROWS, 2 * BLOCK), F32)
        ones = jnp.ones((ONES_ROWS, CW), BF16)
        for c in range(nchs):
            vt1 = jnp.concatenate([vt_ref[0, lanes, c * CW:(c + 1) * CW], ones], axis=0)
            out_t = out_t + _dot(vt1, jnp.concatenate(pes[c], axis=1))
        o_t = out_t[:LANES] / out_t[LANES:LANES + 1]
        o_ref[0, out_rows, lanes] = jnp.where(first_half, o_t[:, :BLOCK], o_t[:, BLOCK:]).T.astype(BF16)


def _dsa_kernel(qi_ref, wi_ref, ki_ref, qb_ref, kb_ref, vt_ref, bias_ref, o_ref,
                qs_ref, wt_ref, q2_ref, sk_ref, mask_ref, *, topk, n_heads, n_blocks):
    j = pl.program_id(1)
    npair = n_heads // 2
    tpc = CW // BLOCK
    span = 2 * n_blocks // DSA_BLOCKS
    blocks, counts = [], []
    for base in range(0, n_blocks, span):
        blocks += [base + j, base + span - 1 - j]
        counts.append(lambda v, base=base: (base // tpc + v + 1, (base + span) // tpc - v))

    for slot, i in enumerate(blocks):
        rows = pl.ds(pl.multiple_of(i * BLOCK, BLOCK), BLOCK)
        qi = qi_ref[0, rows, :]
        lane_i = _lane_id_bf16(qi.shape)
        for h in range(IDX_HEADS):
            in_head = (lane_i >= h * IDX_DIM) & (lane_i < (h + 1) * IDX_DIM)
            qs_ref[slot, h * BLOCK:(h + 1) * BLOCK, :] = jnp.where(in_head, qi, jnp.zeros_like(qi))
        wt_ref[slot] = wi_ref[0, rows, :].T
        for p in range(npair):
            q2_ref[slot, p] = _stack_heads(qb_ref[0, rows, p * LANES:(p + 1) * LANES])

    for v in range(span // (2 * tpc)):
        @pl.when(j // tpc == v)
        def _(v=v):
            nchs_all = [n for count in counts for n in count(v)]
            jobs = [_Job(i, nchs, slot, qs_ref, wt_ref, q2_ref, sk_ref, mask_ref)
                    for slot, (i, nchs) in enumerate(zip(blocks, nchs_all))]
            for job in jobs:
                _dsa_scores(job, ki_ref)

            def search(it, state):
                bit = lax.shift_left(jnp.int32(1), 31 - it)
                out = []
                for job, (ans, n_ans) in zip(jobs, state):
                    n = _dsa_count(job, lambda x, c=_key_to_float(ans | bit): x >= c)
                    keep = n >= float(topk)
                    out.append((jnp.where(keep, ans | bit, ans), jnp.where(keep, n, n_ans)))
                return tuple(out)

            start = tuple((jnp.zeros((1, BLOCK), I32), jnp.zeros((1, BLOCK), F32)) for _ in jobs)
            found = lax.fori_loop(0, 32, search, start)
            thrs = [jnp.where(ans == 0, -jnp.inf, _key_to_float(ans)) for ans, _ in found]
            surplus = [_dsa_select(job, thr, n_ge, topk) for job, thr, (_, n_ge) in zip(jobs, thrs, found)]

            @pl.when(functools.reduce(jnp.maximum, surplus) > 0.0)
            def _():
                for job, thr in zip(jobs, thrs):
                    _dsa_break_ties(job, thr, topk)

            for p in range(npair):
                for job in sorted(jobs, key=lambda jb: -jb.nchs):
                    _dsa_attend(job, kb_ref, vt_ref, bias_ref, o_ref, (p,))


def _dsa(qi, wi, ki, qb, kb, vt, bias, batch, seq, topk):
    bw = qb.shape[-1]
    nidx = qi.shape[-1]
    nb = seq // BLOCK
    n_heads = bw // HEAD_DIM
    npair = n_heads // 2
    assert (2 * nb // DSA_BLOCKS) % (2 * (CW // BLOCK)) == 0
    full = lambda width: pl.BlockSpec((1, seq, width), lambda b, j: (b, 0, 0))
    r3 = lambda a: a.reshape(batch, seq, a.shape[-1])
    vm = lambda shape, dtype: pltpu.VMEM((DSA_BLOCKS,) + shape, dtype)
    out = pl.pallas_call(
        functools.partial(_dsa_kernel, topk=topk, n_heads=n_heads, n_blocks=nb),
        grid=(batch, nb // DSA_BLOCKS),
        in_specs=[full(nidx), full(LANES), full(nidx), full(bw), full(bw),
                  pl.BlockSpec((1, bw, seq), lambda b, j: (b, 0, 0)),
                  _const_spec((n_heads, nb, BLOCK, BLOCK))],
        out_specs=full(bw),
        out_shape=jax.ShapeDtypeStruct((batch, seq, bw), BF16),
        scratch_shapes=[vm((IDX_HEADS * BLOCK, nidx), BF16), vm((LANES, BLOCK), F32),
                        vm((npair, 2 * BLOCK, LANES), BF16),
                        vm((seq // CW, CW, BLOCK), F32), vm((seq // CW, CW, BLOCK), F32)],
        compiler_params=_cparams(2),
        name="dsa",
    )(r3(qi), r3(wi), r3(ki), r3(qb), r3(kb), vt, bias)
    return out.reshape(batch * seq, bw)


def _split_rows(w_out, parts):
    ws, r0 = [], 0
    for a in parts:
        ws.append(w_out[r0:r0 + a.shape[1]].astype(BF16))
        r0 += a.shape[1]
    return ws


def _odd_proj_weights(w_in, qw, kvw):
    d = w_in.shape[0]
    dup = lambda w: jnp.repeat(w.reshape(d, kvw // HEAD_DIM, 1, HEAD_DIM), 2, axis=2).reshape(d, 2 * kvw)
    return jnp.concatenate([w_in[:, :qw], dup(w_in[:, qw:qw + kvw]), dup(w_in[:, qw + kvw:])], axis=1).astype(BF16)


def kernel(x, norm_g, final_g, rel_bias_table, ffn_w1, ffn_w3, ffn_w2, hyb_w_in, hyb_cq_g, hyb_wq_b,
           hyb_wq_idx, hyb_w_out, swa_w_in, swa_sinks, swa_w_out):
    batch, seq, d = x.shape
    depth = norm_g.shape[0]
    n_heads = d // HEAD_DIM
    a_heads = n_heads // 2
    topk = min(TOPK_MAX, seq // 4)
    assert seq % (BLOCK * 16) == 0 and seq % CW == 0 and (batch * seq) % TM_FFN == 0 and seq % TM == 0

    bias_a = _build_bias(
        rel_bias_table,
        np.concatenate([_band_bucket_idx(dil, win // dil) for win, dil in A_BRANCHES], axis=0),
        0, a_heads).reshape(a_heads // 2, 2, 3, BLOCK, 2 * BLOCK).transpose(2, 0, 1, 3, 4)
    bias_a = bias_a.reshape(3, a_heads // 2, 2 * BLOCK, 2 * BLOCK)
    bias_b = _build_bias(rel_bias_table, _dense_bucket_idx(seq), a_heads, n_heads - a_heads)
    bias_b = bias_b.reshape(n_heads - a_heads, seq // BLOCK, BLOCK, BLOCK)
    bias_c = _build_bias(rel_bias_table, _band_bucket_idx(1, C_WINDOW - 1), 0, n_heads)

    ffn_w = (ffn_w1.astype(BF16), ffn_w3.astype(BF16), ffn_w2.astype(BF16))
    h = x.reshape(batch * seq, d)
    for layer in range(depth):
        g = norm_g[layer]
        i = layer // 2
        ffn_a = (g[0], ffn_w, (layer, 0))
        ffn_b = (g[2], ffn_w, (layer, 1))
        if layer % 2 == 0:
            h = _ffn(h, *ffn_a, final_g)
            outs = _proj_even(h, g[1], hyb_w_in[i], hyb_cq_g[i], hyb_wq_b[i], hyb_wq_idx[i], batch, seq)
            o_a = _attn_a(outs[:9], bias_a, batch, seq)
            qb, kb, vt, qi, ki, wi = outs[9:]
            o_b = _dsa(qi, wi, ki, qb, kb, vt, bias_b, batch, seq, topk)
            mix, w_out = [o_a.reshape(batch * seq, -1), o_b], hyb_w_out[i]
        else:
            kvw = (swa_w_in.shape[-1] - d) // 2
            n_kv = kvw // HEAD_DIM
            group = n_heads // n_kv
            h, q, k2, v2 = _ffn(h, *ffn_a, g[1], post_w=_odd_proj_weights(swa_w_in[i], d, kvw),
                                post_splits=((d, Q_SCALE), (2 * kvw, 1.0), (2 * kvw, 1.0)))
            r3 = lambda a: a.reshape(batch, seq, a.shape[-1])
            o_c = _attn_c(r3(q), r3(k2), r3(v2), bias_c.reshape(n_kv, group * BLOCK, 2 * BLOCK),
                          swa_sinks[i].astype(F32), batch, seq, group)
            mix, w_out = [o_c.reshape(batch * seq, -1)], swa_w_out[i]
        h = _ffn(h, *ffn_b, final_g, mix=mix, mix_w=_split_rows(w_out, mix), final=layer == depth - 1)
    return h.reshape(batch, seq, d)
```
